```python
import math
import jax, jax.numpy as jnp
from jax import lax
import numpy as np

D_MODEL = 1024
BATCH = 4
SEQ = 8192
DEPTH = 1
DEC_BATCH = 16
DEC_SEQ = 32
PAST_LEN = 2048

CHUNK = 64
GLA_HEADS = 4
GLA_DK = 64
GLA_DV = 128
GLA_LR = 16
GLA_GATE_TAU = 16.0
GDN_HEADS = 4
GDN_DK = 128
GDN_DV = 128
CONV_W = 4
GLA_QK = GLA_HEADS * GLA_DK
GLA_V = GLA_HEADS * GLA_DV
GDN_QK = GDN_HEADS * GDN_DK
GDN_V = GDN_HEADS * GDN_DV
GDN_QKV = 2 * GDN_QK + GDN_V
D_MIX = GLA_V + GDN_V
IN_SIZES = (GLA_QK, GLA_QK, GLA_V, GLA_LR, GLA_V, GDN_QKV, GDN_HEADS, GDN_HEADS, GDN_V)
N_IN = GLA_QK * 2 + GLA_V * 2 + GLA_LR + GDN_QKV + 2 * GDN_HEADS + GDN_V
PEER_HEADS = 8
PEER_QDIM = 256
PEER_HALF = PEER_QDIM // 2
N_KEYS = 128
N_EXPERTS = N_KEYS * N_KEYS
PEER_TOPK = 16
PEER_BLOCK = 256
NORM_EPS = 1e-6

kernel_name = "hymba_gla_gdn_peer_stream_step"


def rms_norm(x, w):
    xf = x.astype(jnp.float32)
    y = xf * lax.rsqrt(jnp.mean(xf * xf, axis=-1, keepdims=True) + NORM_EPS)
    return (y * w.astype(jnp.float32)).astype(x.dtype)


def l2_norm(x):
    xf = x.astype(jnp.float32)
    return (xf * lax.rsqrt(jnp.sum(xf * xf, axis=-1, keepdims=True) + NORM_EPS)).astype(x.dtype)


def _split_cols(x, sizes):
    idx, acc = [], 0
    for s in sizes[:-1]:
        acc += s
        idx.append(acc)
    return jnp.split(x, idx, axis=-1)


def _to_chunks(x):
    b, l = x.shape[:2]
    n = -(-l // CHUNK)
    x = jnp.pad(x, [(0, 0), (0, n * CHUNK - l)] + [(0, 0)] * (x.ndim - 2))
    x = x.reshape((b, n, CHUNK) + x.shape[2:])
    return jnp.moveaxis(x, (1, 2), (0, 3))


def _from_chunks(x, l):
    x = jnp.moveaxis(x, (0, 3), (1, 2))
    n, c = x.shape[1], x.shape[2]
    x = x.reshape((x.shape[0], n * c) + x.shape[3:])
    return x[:, :l]


def _gla_scan(q, k, v, log_a, s0):
    l = q.shape[1]
    xs = tuple(_to_chunks(t.astype(jnp.float32)) for t in (q, k, v, log_a))
    tril = jnp.tril(jnp.ones((CHUNK, CHUNK), dtype=bool))[:, :, None]

    def step(S, inp):
        qc, kc, vc, ac = inp
        G = jnp.cumsum(ac, axis=2)
        rel = G[:, :, :, None, :] - G[:, :, None, :, :]
        dec = jnp.where(tril, jnp.exp(jnp.where(tril, rel, 0.0)), 0.0)
        att = jnp.sum(qc[:, :, :, None, :] * kc[:, :, None, :, :] * dec, axis=-1)
        o = (jnp.einsum('bhcd,bhde->bhce', qc * jnp.exp(G), S)
             + jnp.einsum('bhij,bhje->bhie', att, vc))
        g_last = G[:, :, -1:, :]
        S = (S * jnp.exp(g_last[:, :, 0, :, None])
             + jnp.einsum('bhjd,bhje->bhde', kc * jnp.exp(g_last - G), vc))
        return S, o

    s_fin, o = lax.scan(step, s0.astype(jnp.float32), xs)
    return _from_chunks(o, l).astype(q.dtype), s_fin.astype(s0.dtype)


def _gdn_scan(q, k, v, log_a, beta, s0):
    l = q.shape[1]
    dv = v.shape[-1]
    xs = tuple(_to_chunks(t.astype(jnp.float32)) for t in (q, k, v, log_a, beta))
    tril = jnp.tril(jnp.ones((CHUNK, CHUNK), dtype=bool))
    strict = jnp.tril(jnp.ones((CHUNK, CHUNK), dtype=bool), k=-1)
    eye = jnp.eye(CHUNK, dtype=jnp.float32)

    def step(S, inp):
        qc, kc, vc, ac, bc = inp
        G = jnp.cumsum(ac, axis=-1)
        rel = G[..., :, None] - G[..., None, :]
        dec = jnp.where(tril, jnp.exp(jnp.where(tril, rel, 0.0)), 0.0)
        kk = jnp.einsum('bhid,bhjd->bhij', kc, kc)
        low = jnp.where(strict, bc[..., :, None] * kk * dec, 0.0)
        rhs = jnp.concatenate([vc * bc[..., None], kc * (bc * jnp.exp(G))[..., None]], axis=-1)
        sol = lax.linalg.triangular_solve(eye + low, rhs, left_side=True, lower=True,
                                          unit_diagonal=True)
        u_c, w_c = sol[..., :dv], sol[..., dv:]
        v_new = u_c - jnp.einsum('bhcd,bhde->bhce', w_c, S)
        qk = jnp.einsum('bhid,bhjd->bhij', qc, kc) * dec
        o = (jnp.einsum('bhcd,bhde->bhce', qc * jnp.exp(G)[..., None], S)
             + jnp.einsum('bhij,bhje->bhie', qk, v_new))
        g_last = G[..., -1:]
        S = (S * jnp.exp(g_last)[..., None]
             + jnp.einsum('bhjd,bhje->bhde', kc * jnp.exp(g_last - G)[..., None], v_new))
        return S, o

    s_fin, o = lax.scan(step, s0.astype(jnp.float32), xs)
    return _from_chunks(o, l).astype(q.dtype), s_fin.astype(s0.dtype)


def _peer(h, peer_wq, peer_k1, peer_k2, peer_u, peer_v):
    b, l, d = h.shape
    t = b * l
    nb = -(-t // PEER_BLOCK)
    xt = jnp.pad(h.reshape(t, d), ((0, nb * PEER_BLOCK - t), (0, 0)))
    xb = xt.reshape(nb, PEER_BLOCK, d)

    def block(xblk):
        q = (xblk @ peer_wq).reshape(PEER_BLOCK, PEER_HEADS, PEER_QDIM)
        s1 = jnp.einsum('thd,hnd->thn', q[..., :PEER_HALF], peer_k1)
        s2 = jnp.einsum('thd,hnd->thn', q[..., PEER_HALF:], peer_k2)
        v1, i1 = lax.top_k(s1, PEER_TOPK)
        v2, i2 = lax.top_k(s2, PEER_TOPK)
        cand = (v1[..., :, None] + v2[..., None, :]).reshape(PEER_BLOCK, PEER_HEADS, PEER_TOPK * PEER_TOPK)
        cidx = (i1[..., :, None] * N_KEYS + i2[..., None, :]).reshape(PEER_BLOCK, PEER_HEADS, PEER_TOPK * PEER_TOPK)
        sc, pos = lax.top_k(cand, PEER_TOPK)
        eidx = jnp.take_along_axis(cidx, pos, axis=-1)
        gate = jax.nn.softmax(sc.astype(jnp.float32), axis=-1)
        act = jax.nn.gelu(jnp.einsum('thkd,td->thk', peer_u[eidx], xblk))
        wgt = (gate * act.astype(jnp.float32)).astype(xblk.dtype)
        return jnp.einsum('thk,thkd->td', wgt, peer_v[eidx])

    yb = lax.map(block, xb)
    return yb.reshape(nb * PEER_BLOCK, d)[:t].reshape(b, l, d)


def _layer(x, s_gla, s_gdn, conv_buf, norm_mix_w, w_in, gla_w_gk2, gla_b_gk, gla_norm_w,
           gdn_conv_w, gdn_a_log, gdn_dt_bias, gdn_norm_w, w_out, norm_ffn_w,
           peer_wq, peer_k1, peer_k2, peer_u, peer_v):
    b, l, _ = x.shape
    h = rms_norm(x, norm_mix_w)
    proj = h @ w_in
    gq, gk, gv, glr, gg, dqkv, da, db, dz = _split_cols(proj, IN_SIZES)

    q1 = gq.reshape(b, l, GLA_HEADS, GLA_DK) * (GLA_DK ** -0.5)
    k1 = gk.reshape(b, l, GLA_HEADS, GLA_DK)
    v1 = gv.reshape(b, l, GLA_HEADS, GLA_DV)
    log_a1 = jax.nn.log_sigmoid((glr @ gla_w_gk2 + gla_b_gk).astype(jnp.float32)) / GLA_GATE_TAU
    log_a1 = log_a1.reshape(b, l, GLA_HEADS, GLA_DK)
    o1, s_gla_new = _gla_scan(q1, k1, v1, log_a1, s_gla)
    o1 = rms_norm(o1, gla_norm_w) * jax.nn.silu(gg).reshape(b, l, GLA_HEADS, GLA_DV)

    conv_in = jnp.concatenate([conv_buf.astype(dqkv.dtype), dqkv], axis=1)
    conv = conv_in[:, 0:l] * gdn_conv_w[0]
    for i in range(1, CONV_W):
        conv = conv + conv_in[:, i:i + l] * gdn_conv_w[i]
    conv = jax.nn.silu(conv)
    new_buf = conv_in[:, -(CONV_W - 1):]
    cq, ck, cv = _split_cols(conv, (GDN_QK, GDN_QK, GDN_V))
    q2 = l2_norm(cq.reshape(b, l, GDN_HEADS, GDN_DK)) * (GDN_DK ** -0.5)
    k2 = l2_norm(ck.reshape(b, l, GDN_HEADS, GDN_DK))
    v2 = cv.reshape(b, l, GDN_HEADS, GDN_DV)
    beta = jax.nn.sigmoid(db.astype(jnp.float32))
    log_a2 = -jnp.exp(gdn_a_log.astype(jnp.float32)) * jax.nn.softplus((da + gdn_dt_bias).astype(jnp.float32))
    o2, s_gdn_new = _gdn_scan(q2, k2, v2, log_a2, beta, s_gdn)
    o2 = rms_norm(o2, gdn_norm_w) * jax.nn.silu(dz).reshape(b, l, GDN_HEADS, GDN_DV)

    mix = jnp.concatenate([o1.reshape(b, l, GLA_V), o2.reshape(b, l, GDN_V)], axis=-1) @ w_out
    x = x + mix
    x = x + _peer(rms_norm(x, norm_ffn_w), peer_wq, peer_k1, peer_k2, peer_u, peer_v)
    return x, s_gla_new, s_gdn_new, new_buf


def setup_inputs(seed: int = 0) -> dict:
    key = jax.random.key(seed)
    ks = jax.random.split(key, 24)
    f32 = jnp.float32
    nrm = lambda k, shape, scale: jax.random.normal(k, shape, f32) * scale
    dt = jnp.exp(jax.random.uniform(ks[12], (DEPTH, GDN_HEADS), f32, math.log(1e-3), math.log(1e-1)))
    return {
        "x_prompt": nrm(ks[0], (BATCH, SEQ, D_MODEL), 1.0),
        "x_sample": nrm(ks[1], (DEC_BATCH, DEC_SEQ, D_MODEL), 1.0),
        "state_gla": nrm(ks[2], (DEPTH, DEC_BATCH, GLA_HEADS, GLA_DK, GLA_DV), 0.1),
        "state_gdn": nrm(ks[3], (DEPTH, DEC_BATCH, GDN_HEADS, GDN_DK, GDN_DV), 0.1),
        "state_gdn_conv": nrm(ks[4], (DEPTH, DEC_BATCH, CONV_W - 1, GDN_QKV), 1.0),
        "norm_mix_w": 1.0 + nrm(ks[5], (DEPTH, D_MODEL), 0.02),
        "w_in": nrm(ks[6], (DEPTH, D_MODEL, N_IN), D_MODEL ** -0.5),
        "gla_w_gk2": nrm(ks[7], (DEPTH, GLA_LR, GLA_QK), GLA_LR ** -0.5),
        "gla_b_gk": nrm(ks[8], (DEPTH, GLA_QK), 0.1),
        "gla_norm_w": 1.0 + nrm(ks[9], (DEPTH, GLA_DV), 0.02),
        "gdn_conv_w": nrm(ks[10], (DEPTH, CONV_W, GDN_QKV), CONV_W ** -0.5),
        "gdn_a_log": jnp.log(jax.random.uniform(ks[11], (DEPTH, GDN_HEADS), f32, 1.0, 16.0)),
        "gdn_dt_bias": jnp.log(jnp.expm1(dt)),
        "gdn_norm_w": 1.0 + nrm(ks[13], (DEPTH, GDN_DV), 0.02),
        "w_out": nrm(ks[14], (DEPTH, D_MIX, D_MODEL), D_MIX ** -0.5),
        "norm_ffn_w": 1.0 + nrm(ks[15], (DEPTH, D_MODEL), 0.02),
        "peer_wq": nrm(ks[16], (DEPTH, D_MODEL, PEER_HEADS * PEER_QDIM), D_MODEL ** -0.5),
        "peer_k1": nrm(ks[17], (DEPTH, PEER_HEADS, N_KEYS, PEER_HALF), PEER_HALF ** -0.5),
        "peer_k2": nrm(ks[18], (DEPTH, PEER_HEADS, N_KEYS, PEER_HALF), PEER_HALF ** -0.5),
        "peer_u": nrm(ks[19], (DEPTH, N_EXPERTS, D_MODEL), D_MODEL ** -0.5),
        "peer_v": nrm(ks[20], (DEPTH, N_EXPERTS, D_MODEL), D_MODEL ** -0.5),
        "norm_final_w": 1.0 + nrm(ks[21], (D_MODEL,), 0.02),
    }


def reference(x_prompt, x_sample, state_gla, state_gdn, state_gdn_conv, norm_mix_w, w_in,
              gla_w_gk2, gla_b_gk, gla_norm_w, gdn_conv_w, gdn_a_log, gdn_dt_bias, gdn_norm_w,
              w_out, norm_ffn_w, peer_wq, peer_k1, peer_k2, peer_u, peer_v, norm_final_w):
    yp, ys = x_prompt, x_sample
    gla_p, gdn_p, conv_p, gla_s, gdn_s, conv_s = [], [], [], [], [], []
    for l in range(DEPTH):
        lw = (norm_mix_w[l], w_in[l], gla_w_gk2[l], gla_b_gk[l], gla_norm_w[l], gdn_conv_w[l],
              gdn_a_log[l], gdn_dt_bias[l], gdn_norm_w[l], w_out[l], norm_ffn_w[l],
              peer_wq[l], peer_k1[l], peer_k2[l], peer_u[l], peer_v[l])
        z_gla = jnp.zeros((BATCH, GLA_HEADS, GLA_DK, GLA_DV), x_prompt.dtype)
        z_gdn = jnp.zeros((BATCH, GDN_HEADS, GDN_DK, GDN_DV), x_prompt.dtype)
        z_conv = jnp.zeros((BATCH, CONV_W - 1, GDN_QKV), x_prompt.dtype)
        yp, a, b_, c = _layer(yp, z_gla, z_gdn, z_conv, *lw)
        gla_p.append(a); gdn_p.append(b_); conv_p.append(c)
        ys, a, b_, c = _layer(ys, state_gla[l], state_gdn[l], state_gdn_conv[l], *lw)
        gla_s.append(a); gdn_s.append(b_); conv_s.append(c)
    y_prompt = rms_norm(yp, norm_final_w)
    y_sample = rms_norm(ys, norm_final_w)
    return (y_prompt, y_sample, jnp.stack(gla_p), jnp.stack(gdn_p), jnp.stack(conv_p),
            jnp.stack(gla_s), jnp.stack(gdn_s), jnp.stack(conv_s))
```

```python
import functools

import jax
import jax.numpy as jnp
from jax import lax
from jax.experimental import pallas as pl
from jax.experimental.pallas import tpu as pltpu

F32 = jnp.float32
BF16 = jnp.bfloat16
HI = lax.Precision.HIGHEST

NORM_EPS = 1e-6
CHUNK = 64
GLA_HEADS, GLA_DK, GLA_DV, GLA_LR, GLA_GATE_TAU = 4, 64, 128, 16, 16.0
GDN_HEADS, GDN_DK, GDN_DV, CONV_W = 4, 128, 128, 4
GLA_QK, GLA_V = GLA_HEADS * GLA_DK, GLA_HEADS * GLA_DV
GDN_QK, GDN_V = GDN_HEADS * GDN_DK, GDN_HEADS * GDN_DV
GDN_QKV = 2 * GDN_QK + GDN_V
PEER_HEADS, PEER_QDIM, N_KEYS, PEER_TOPK = 8, 256, 128, 16
PEER_HALF = PEER_QDIM // 2
N_SEL = PEER_HEADS * PEER_TOPK

LANES = 128
SUBLANES = 8
MIB = 1024 * 1024

SM_A0 = GLA_LR
SM_B0 = GLA_LR + GDN_HEADS


def _cparams(sem, vmem_mib):
    return pltpu.CompilerParams(dimension_semantics=sem, vmem_limit_bytes=vmem_mib * MIB)


def _softplus(x):
    return jnp.maximum(x, 0.0) + jnp.log(1.0 + jnp.exp(-jnp.abs(x)))


def _sigmoid(x):
    return 1.0 / (1.0 + jnp.exp(-x))


def _dot(a, b, prec=None):
    return jnp.dot(a, b, preferred_element_type=F32, precision=prec)


def _dot_nt(a, b, prec=None):
    return lax.dot_general(a, b, (((1,), (1,)), ((), ())), preferred_element_type=F32, precision=prec)


def _dot_tn(a, b, prec=None):
    return lax.dot_general(a, b, (((0,), (0,)), ((), ())), preferred_element_type=F32, precision=prec)


_W_COLS = (("gq", GLA_QK), ("gk", GLA_QK), ("gv", GLA_V), ("gg", GLA_V), ("dqkv", GDN_QKV),
           ("dz", GDN_V), ("sm", LANES))


def _inproj_kernel(x_ref, nw_ref, w_ref, w2_ref, b2_ref, avec_ref, dtb_ref,
                   gq_ref, gk_ref, gv_ref, gg_ref, dqkv_ref, dz_ref, la_ref, sm_ref):
    x = x_ref[...]
    ms = jnp.mean(x * x, axis=-1, keepdims=True)
    h = (x * lax.rsqrt(ms + NORM_EPS) * nw_ref[...]).astype(BF16)
    outs = dict(gq=gq_ref, gk=gk_ref, gv=gv_ref, gg=gg_ref, dqkv=dqkv_ref, dz=dz_ref)
    off = 0
    ps = None
    for name, width in _W_COLS:
        p = _dot(h, w_ref[:, off:off + width])
        off += width
        if name == "sm":
            ps = p
        elif name == "gq":
            gq_ref[...] = p * (GLA_DK ** -0.5)
        else:
            outs[name][...] = p
    z = _dot(ps.astype(BF16), w2_ref[...]) + b2_ref[...]
    la_ref[...] = -_softplus(-z) * (1.0 / GLA_GATE_TAU)
    lane = lax.broadcasted_iota(jnp.int32, ps.shape, 1)
    log_a = avec_ref[...] * _softplus(ps + dtb_ref[...])
    beta = _sigmoid(ps)
    is_a = (lane >= SM_A0) & (lane < SM_B0)
    is_b = (lane >= SM_B0) & (lane < SM_B0 + GDN_HEADS)
    sm_ref[...] = jnp.where(is_a, log_a, jnp.where(is_b, beta, ps))


def _inproj(x, nw, wcat, w2p, b2, avec, dtb, tm):
    t, d = x.shape
    n_w = wcat.shape[1]
    widths = dict(_W_COLS)
    names = ("gq", "gk", "gv", "gg", "dqkv", "dz")
    out_shape = [jax.ShapeDtypeStruct((t, widths[n]), F32) for n in names]
    out_shape += [jax.ShapeDtypeStruct((t, GLA_QK), F32), jax.ShapeDtypeStruct((t, LANES), F32)]
    row = lambda w: pl.BlockSpec((tm, w), lambda i: (i, 0))
    full = lambda a: pl.BlockSpec(a.shape, lambda i: (0,) * a.ndim)
    return pl.pallas_call(
        _inproj_kernel,
        out_shape=out_shape,
        grid=(t // tm,),
        in_specs=[row(d), full(nw), full(wcat), full(w2p), full(b2), full(avec), full(dtb)],
        out_specs=[row(widths[n]) for n in names] + [row(GLA_QK), row(LANES)],
        compiler_params=_cparams(("parallel",), 48),
        name="inproj",
    )(x, nw, wcat, w2p, b2, avec, dtb)


def _iota2(n, m, axis):
    return lax.broadcasted_iota(jnp.int32, (n, m), axis)


def _gla_kernel(q_ref, k_ref, v_ref, la_ref, gg_ref, s0_ref, nw_ref, o_ref, sfin_ref, st_ref, *, chunk):
    c = pl.program_id(1)
    nc = pl.num_programs(1)

    @pl.when(c == 0)
    def _():
        st_ref[...] = s0_ref[0]

    row = _iota2(chunk, chunk, 0)
    col = _iota2(chunk, chunk, 1)
    tril = (row >= col).astype(F32)
    q = q_ref[...]
    k = k_ref[...]
    g = _dot(tril, la_ref[...], HI)
    lane_head = _iota2(chunk, GLA_QK, 1) // GLA_DK

    levels = []
    b = chunk // 2
    while b >= 1:
        sel_r = (col == (row // b) * b).astype(F32)
        sel_c = (col == jnp.minimum((row // b + 1) * b, chunk - 1)).astype(F32)
        ql = q * jnp.exp(g - _dot(sel_r, g, HI))
        kl = k * jnp.exp(_dot(sel_c, g, HI) - g)
        mask = (row // (2 * b) == col // (2 * b)) & (row // b == col // b + 1)
        levels.append((ql, kl, mask))
        b //= 2
    qk_diag = q * k
    g_last = g[chunk - 1:chunk, :]
    q_in = q * jnp.exp(g)
    k_out = k * jnp.exp(g_last - g)
    decay_out = jnp.exp(g_last)

    for h in range(GLA_HEADS):
        hm = (lane_head == h).astype(F32)
        att = jnp.zeros((chunk, chunk), F32)
        for ql, kl, mask in levels:
            att = att + jnp.where(mask, _dot_nt(ql * hm, kl, HI), 0.0)
        dsum = jnp.sum(qk_diag * hm, axis=-1, keepdims=True)
        vh = v_ref[:, h * GLA_DV:(h + 1) * GLA_DV]
        st = st_ref[h]
        dk0 = h * GLA_DK
        o = (_dot_nt(q_in[:, dk0:dk0 + GLA_DK], st, HI) + _dot(att, vh, HI) + dsum * vh)
        st_ref[h] = (st * decay_out[:, dk0:dk0 + GLA_DK]
                     + _dot_tn(vh, k_out[:, dk0:dk0 + GLA_DK], HI))
        ms = jnp.mean(o * o, axis=-1, keepdims=True)
        on = o * lax.rsqrt(ms + NORM_EPS) * nw_ref[...]
        gate = gg_ref[:, h * GLA_DV:(h + 1) * GLA_DV]
        o_ref[:, h * GLA_DV:(h + 1) * GLA_DV] = on * (gate * _sigmoid(gate))

    @pl.when(c == nc - 1)
    def _():
        sfin_ref[0] = st_ref[...]


def _gla(gq, gk, gv, la, gg, s0t, nw, n_seq, seq_len, chunk):
    nc = seq_len // chunk
    t = n_seq * seq_len
    blk = lambda w: pl.BlockSpec((chunk, w), lambda b, c: (b * nc + c, 0))
    st_spec = pl.BlockSpec((1, GLA_HEADS, GLA_DV, GLA_DK), lambda b, c: (b, 0, 0, 0))
    return pl.pallas_call(
        functools.partial(_gla_kernel, chunk=chunk),
        out_shape=[jax.ShapeDtypeStruct((t, GLA_V), F32),
                   jax.ShapeDtypeStruct((n_seq, GLA_HEADS, GLA_DV, GLA_DK), F32)],
        grid=(n_seq, nc),
        in_specs=[blk(GLA_QK), blk(GLA_QK), blk(GLA_V), blk(GLA_QK), blk(GLA_V), st_spec,
                  pl.BlockSpec((1, GLA_DV), lambda b, c: (0, 0))],
        out_specs=[blk(GLA_V), st_spec],
        scratch_shapes=[pltpu.VMEM((GLA_HEADS, GLA_DV, GLA_DK), F32)],
        compiler_params=_cparams(("parallel", "arbitrary"), 32),
        name="gla_scan",
    )(gq, gk, gv, la, gg, s0t, nw)


_INV_BASE = 16


def _unit_lower_inverse(low, row, col, n):
    eye = (row == col).astype(F32)
    base = min(_INV_BASE, n)
    ld = jnp.where(row // base == col // base, low, 0.0)
    inv = eye - ld
    pw = ld
    p = 2
    while p < base:
        pw = _dot(pw, pw, HI)
        inv = inv + _dot(inv, pw, HI)
        p *= 2
    b = base
    while b < n:
        cb = jnp.where((row // (2 * b) == col // (2 * b)) & (row // b != col // b), low, 0.0)
        inv = inv - _dot(inv, _dot(cb, inv, HI), HI)
        b *= 2
    return inv


def _gdn_kernel(x_ref, ab_ref, dz_ref, tail0_ref, cw_ref, s0_ref, nw_ref,
                o_ref, sfin_ref, tail_out_ref, st_ref, tail_ref, *, chunk):
    c = pl.program_id(1)
    nc = pl.num_programs(1)

    @pl.when(c == 0)
    def _():
        st_ref[...] = s0_ref[0]
        tail_ref[...] = tail0_ref[0]

    x = x_ref[...]
    ext = jnp.concatenate([tail_ref[...], x], axis=0)
    base = SUBLANES - (CONV_W - 1)
    conv = ext[base:base + chunk] * cw_ref[0:1, :]
    for i in range(1, CONV_W):
        conv = conv + ext[base + i:base + i + chunk] * cw_ref[i:i + 1, :]
    tail_ref[...] = x[chunk - SUBLANES:chunk]
    conv = conv * _sigmoid(conv)

    row = _iota2(chunk, chunk, 0)
    col = _iota2(chunk, chunk, 1)
    lower = row >= col
    tril = lower.astype(F32)
    ab = ab_ref[...]
    g_col = _dot(tril, ab, HI)
    g_row = _dot_nt(ab.T, tril, HI)

    for h in range(GDN_HEADS):
        cq = conv[:, h * GDN_DK:(h + 1) * GDN_DK]
        ck = conv[:, GDN_QK + h * GDN_DK:GDN_QK + (h + 1) * GDN_DK]
        v = conv[:, 2 * GDN_QK + h * GDN_DV:2 * GDN_QK + (h + 1) * GDN_DV]
        q = cq * lax.rsqrt(jnp.sum(cq * cq, axis=-1, keepdims=True) + NORM_EPS) * (GDN_DK ** -0.5)
        k = ck * lax.rsqrt(jnp.sum(ck * ck, axis=-1, keepdims=True) + NORM_EPS)
        gc = g_col[:, SM_A0 + h:SM_A0 + h + 1]
        gr = g_row[SM_A0 + h:SM_A0 + h + 1, :]
        beta = ab[:, SM_B0 + h:SM_B0 + h + 1]
        dec = jnp.where(lower, jnp.exp(jnp.where(lower, gc - gr, 0.0)), 0.0)
        kk = _dot_nt(k, k, HI)
        low = jnp.where(row > col, beta * kk * dec, 0.0)
        tinv = _unit_lower_inverse(low, row, col, chunk)
        eg = jnp.exp(gc)
        u = _dot(tinv, v * beta, HI)
        w = _dot(tinv, k * (beta * eg), HI)
        st = st_ref[h]
        v_new = u - _dot(w, st, HI)
        qk = _dot_nt(q, k, HI) * dec
        o = _dot(q * eg, st, HI) + _dot(qk, v_new, HI)
        g_last = gc[chunk - 1:chunk, :]
        st_ref[h] = st * jnp.exp(g_last) + _dot_tn(k * jnp.exp(g_last - gc), v_new, HI)
        ms = jnp.mean(o * o, axis=-1, keepdims=True)
        on = o * lax.rsqrt(ms + NORM_EPS) * nw_ref[...]
        gate = dz_ref[:, h * GDN_DV:(h + 1) * GDN_DV]
        o_ref[:, h * GDN_DV:(h + 1) * GDN_DV] = on * (gate * _sigmoid(gate))

    @pl.when(c == nc - 1)
    def _():
        sfin_ref[0] = st_ref[...]
        tail_out_ref[0] = tail_ref[...]


def _gdn(dqkv, ab, dz, tail0, cw, s0, nw, n_seq, seq_len, chunk):
    nc = seq_len // chunk
    t = n_seq * seq_len
    blk = lambda w: pl.BlockSpec((chunk, w), lambda b, c: (b * nc + c, 0))
    st_spec = pl.BlockSpec((1, GDN_HEADS, GDN_DK, GDN_DV), lambda b, c: (b, 0, 0, 0))
    tail_spec = pl.BlockSpec((1, SUBLANES, GDN_QKV), lambda b, c: (b, 0, 0))
    return pl.pallas_call(
        functools.partial(_gdn_kernel, chunk=chunk),
        out_shape=[jax.ShapeDtypeStruct((t, GDN_V), F32),
                   jax.ShapeDtypeStruct((n_seq, GDN_HEADS, GDN_DK, GDN_DV), F32),
                   jax.ShapeDtypeStruct((n_seq, SUBLANES, GDN_QKV), F32)],
        grid=(n_seq, nc),
        in_specs=[blk(GDN_QKV), blk(LANES), blk(GDN_V), tail_spec,
                  pl.BlockSpec((SUBLANES, GDN_QKV), lambda b, c: (0, 0)), st_spec,
                  pl.BlockSpec((1, GDN_DV), lambda b, c: (0, 0))],
        out_specs=[blk(GDN_V), st_spec, tail_spec],
        scratch_shapes=[pltpu.VMEM((GDN_HEADS, GDN_DK, GDN_DV), F32),
                        pltpu.VMEM((SUBLANES, GDN_QKV), F32)],
        compiler_params=_cparams(("parallel", "arbitrary"), 32),
        name="gdn_scan",
    )(dqkv, ab, dz, tail0, cw, s0, nw)


def _topk_cols(s, k, payload=None):
    n = s.shape[0]
    rid = lax.broadcasted_iota(jnp.int32, s.shape, 0).astype(F32)
    vals, idxs = [], []
    for _ in range(k):
        m = jnp.max(s, axis=0, keepdims=True)
        am = jnp.min(jnp.where(s == m, rid, float(n)), axis=0, keepdims=True)
        sel = rid == am
        vals.append(m)
        if payload is None:
            idxs.append(am)
        else:
            idxs.append(jnp.sum(jnp.where(sel, payload, 0.0), axis=0, keepdims=True))
        s = jnp.where(sel, -jnp.inf, s)
    return jnp.concatenate(vals, axis=0), jnp.concatenate(idxs, axis=0)


def _post_kernel(o1_ref, o2_ref, x_ref, wo_ref, nw_ref, wq_ref, k1_ref, k2_ref,
                 x2_ref, hn_ref, idx_ref, gate_ref):
    mix = (_dot(o1_ref[...].astype(BF16), wo_ref[0:GLA_V, :])
           + _dot(o2_ref[...].astype(BF16), wo_ref[GLA_V:GLA_V + GDN_V, :]))
    x2 = x_ref[...] + mix
    x2_ref[...] = x2
    ms = jnp.mean(x2 * x2, axis=-1, keepdims=True)
    hn = x2 * lax.rsqrt(ms + NORM_EPS) * nw_ref[...]
    hn_ref[...] = hn
    qry = _dot(hn.astype(BF16), wq_ref[...])
    idx_rows, gate_rows = [], []
    for h in range(PEER_HEADS):
        q1 = qry[:, h * PEER_QDIM:h * PEER_QDIM + PEER_HALF].astype(BF16)
        q2 = qry[:, h * PEER_QDIM + PEER_HALF:(h + 1) * PEER_QDIM].astype(BF16)
        s1 = _dot_nt(k1_ref[h], q1)
        s2 = _dot_nt(k2_ref[h], q2)
        v1, i1 = _topk_cols(s1, PEER_TOPK)
        v2, i2 = _topk_cols(s2, PEER_TOPK)
        cand = jnp.concatenate([v1[a:a + 1, :] + v2 for a in range(PEER_TOPK)], axis=0)
        cidx = jnp.concatenate([i1[a:a + 1, :] * float(N_KEYS) + i2 for a in range(PEER_TOPK)], axis=0)
        sc, eidx = _topk_cols(cand, PEER_TOPK, payload=cidx)
        e = jnp.exp(sc - jnp.max(sc, axis=0, keepdims=True))
        gate_rows.append(e / jnp.sum(e, axis=0, keepdims=True))
        idx_rows.append(eidx)
    idx_ref[...] = jnp.concatenate(idx_rows, axis=0).T.astype(jnp.int32)
    gate_ref[...] = jnp.concatenate(gate_rows, axis=0).T


def _post(o1, o2, x, wo, nw, wq, k1, k2, tm):
    t, d = x.shape
    row = lambda w: pl.BlockSpec((tm, w), lambda i: (i, 0))
    full = lambda a: pl.BlockSpec(a.shape, lambda i: (0,) * a.ndim)
    return pl.pallas_call(
        _post_kernel,
        out_shape=[jax.ShapeDtypeStruct((t, d), F32), jax.ShapeDtypeStruct((t, d), F32),
                   jax.ShapeDtypeStruct((t, N_SEL), jnp.int32), jax.ShapeDtypeStruct((t, N_SEL), F32)],
        grid=(t // tm,),
        in_specs=[row(GLA_V), row(GDN_V), row(d), full(wo), full(nw), full(wq), full(k1), full(k2)],
        out_specs=[row(d), row(d), row(N_SEL), row(N_SEL)],
        compiler_params=_cparams(("parallel",), 48),
        name="post_topk",
    )(o1, o2, x, wo, nw, wq, k1, k2)


_HI_MASK = -65536


def _expert_row(tab_ref, e, half_rows, half_shift):
    j = e & (half_rows - 1)
    sh = (1 - (e >> half_shift)) * 16
    bits = lax.shift_left(tab_ref[j], jnp.full((SUBLANES, LANES), sh, jnp.int32)) & _HI_MASK
    return lax.bitcast_convert_type(bits, F32)


def _fold_sublanes(vs, sub):
    m4 = sub < 4
    cur = []
    for a, b in zip(vs[0::2], vs[1::2]):
        cur.append(jnp.where(m4, a, b) + pltpu.roll(jnp.where(m4, b, a), 4, 0))
    for half in (2, 1):
        m = (sub % (2 * half)) < half
        nxt = []
        for a, b in zip(cur[0::2], cur[1::2]):
            ta = a + pltpu.roll(a, SUBLANES - half, 0)
            tb = b + pltpu.roll(b, half, 0)
            nxt.append(jnp.where(m, ta, tb))
        cur = nxt
    return cur[0]


def _peer_u_kernel(idx_ref, hn_ref, gate_ref, tab_ref, w_ref, *, tb, half_rows, half_shift):
    sub = lax.broadcasted_iota(jnp.int32, (SUBLANES, LANES), 0)
    ones = jnp.ones((SUBLANES, LANES), F32)
    perm = (0, 4, 2, 6, 1, 5, 3, 7)

    def body(t, carry):
        x = hn_ref[t]
        folded = []
        for g in range(N_SEL // SUBLANES):
            prods = []
            for r in range(SUBLANES):
                e = idx_ref[t, g * SUBLANES + perm[r]]
                prods.append(_expert_row(tab_ref, e, half_rows, half_shift) * x)
            folded.append(_fold_sublanes(prods, sub))
        part = jnp.concatenate(folded, axis=0)
        act = _dot_nt(ones, part, HI)[0:1, :]
        gelu = 0.5 * act * (1.0 + jnp.tanh(0.7978845608028654 * (act + 0.044715 * act * act * act)))
        w_ref[pl.ds(t, 1), :] = gate_ref[pl.ds(t, 1), :] * gelu
        return carry

    lax.fori_loop(0, tb, body, 0)


def _peer_u(idx, hn3, gate, tab, tb):
    t = idx.shape[0]
    half_rows = tab.shape[0]
    return pl.pallas_call(
        functools.partial(_peer_u_kernel, tb=tb, half_rows=half_rows,
                          half_shift=half_rows.bit_length() - 1),
        out_shape=jax.ShapeDtypeStruct((t, N_SEL), F32),
        grid=(t // tb,),
        in_specs=[pl.BlockSpec((tb, N_SEL), lambda i: (i, 0), memory_space=pltpu.SMEM),
                  pl.BlockSpec((tb, SUBLANES, LANES), lambda i: (i, 0, 0)),
                  pl.BlockSpec((tb, N_SEL), lambda i: (i, 0)),
                  pl.BlockSpec(memory_space=pltpu.VMEM)],
        out_specs=pl.BlockSpec((tb, N_SEL), lambda i: (i, 0)),
        compiler_params=_cparams(("parallel",), 48),
        name="peer_u",
    )(idx, hn3, gate, tab)


def _peer_v_kernel(idx_ref, w_ref, x2_ref, nw_ref, tab_ref, y_ref, *, tb, half_rows, half_shift, n_acc):
    def body(t, carry):
        accs = [jnp.zeros((SUBLANES, LANES), F32) for _ in range(n_acc)]
        for kk in range(N_SEL):
            e = idx_ref[t, kk]
            accs[kk % n_acc] = accs[kk % n_acc] + _expert_row(tab_ref, e, half_rows, half_shift) * w_ref[t, kk]
        tot = accs[0]
        for a in accs[1:]:
            tot = tot + a
        y_ref[t] = x2_ref[t] + tot
        return carry

    lax.fori_loop(0, tb, body, 0)
    x3 = y_ref[...]
    ss = jnp.sum(jnp.sum(x3 * x3, axis=2, keepdims=True), axis=1, keepdims=True)
    y_ref[...] = x3 * lax.rsqrt(ss * (1.0 / (SUBLANES * LANES)) + NORM_EPS) * nw_ref[...][None]


def _peer_v(idx, wgt, x23, nw3, tab, tb):
    t = idx.shape[0]
    half_rows = tab.shape[0]
    smem = lambda: pl.BlockSpec((tb, N_SEL), lambda i: (i, 0), memory_space=pltpu.SMEM)
    return pl.pallas_call(
        functools.partial(_peer_v_kernel, tb=tb, half_rows=half_rows,
                          half_shift=half_rows.bit_length() - 1, n_acc=4),
        out_shape=jax.ShapeDtypeStruct((t, SUBLANES, LANES), F32),
        grid=(t // tb,),
        in_specs=[smem(), smem(),
                  pl.BlockSpec((tb, SUBLANES, LANES), lambda i: (i, 0, 0)),
                  pl.BlockSpec((SUBLANES, LANES), lambda i: (0, 0)),
                  pl.BlockSpec(memory_space=pltpu.VMEM)],
        out_specs=pl.BlockSpec((tb, SUBLANES, LANES), lambda i: (i, 0, 0)),
        compiler_params=_cparams(("parallel",), 48),
        name="peer_v",
    )(idx, wgt, x23, nw3, tab)


def _pack_table(tab):
    n, d = tab.shape
    bits = lax.bitcast_convert_type(tab.astype(BF16), jnp.uint16).astype(jnp.uint32)
    packed = (bits[n // 2:] << 16) | bits[:n // 2]
    return lax.bitcast_convert_type(packed, jnp.int32).reshape(n // 2, d // LANES, LANES)


def _prep_weights(norm_mix_w, w_in, gla_w_gk2, gla_b_gk, gla_norm_w, gdn_conv_w, gdn_a_log,
                  gdn_dt_bias, gdn_norm_w, w_out, norm_ffn_w, peer_wq, peer_k1, peer_k2, peer_u, peer_v):
    d = w_in.shape[0]
    o_glr = 2 * GLA_QK + GLA_V
    o_gg = o_glr + GLA_LR
    o_qkv = o_gg + GLA_V
    o_da = o_qkv + GDN_QKV
    o_dz = o_da + 2 * GDN_HEADS
    small = jnp.concatenate([w_in[:, o_glr:o_gg], w_in[:, o_da:o_dz],
                             jnp.zeros((d, LANES - GLA_LR - 2 * GDN_HEADS), w_in.dtype)], axis=1)
    wcat = jnp.concatenate([w_in[:, :o_glr], w_in[:, o_gg:o_qkv], w_in[:, o_qkv:o_da],
                            w_in[:, o_dz:], small], axis=1).astype(BF16)
    w2p = jnp.zeros((LANES, GLA_QK), F32).at[:GLA_LR].set(gla_w_gk2).astype(BF16)
    avec = jnp.zeros((1, LANES), F32).at[0, SM_A0:SM_B0].set(-jnp.exp(gdn_a_log))
    dtb = jnp.zeros((1, LANES), F32).at[0, SM_A0:SM_B0].set(gdn_dt_bias)
    cw = jnp.zeros((SUBLANES, GDN_QKV), F32).at[:CONV_W].set(gdn_conv_w)
    return dict(
        nmix=norm_mix_w.reshape(1, d), wcat=wcat, w2p=w2p, b2=gla_b_gk.reshape(1, GLA_QK), avec=avec, dtb=dtb,
        gla_nw=gla_norm_w.reshape(1, GLA_DV), cw=cw, gdn_nw=gdn_norm_w.reshape(1, GDN_DV),
        wo=w_out.astype(BF16), nffn=norm_ffn_w.reshape(1, d), wq=peer_wq.astype(BF16),
        k1=peer_k1.astype(BF16), k2=peer_k2.astype(BF16), tab_u=_pack_table(peer_u), tab_v=_pack_table(peer_v))


def _layer(x, s_gla, s_gdn, conv_buf, w, nfw3, chunk, tm, tb):
    n_seq, seq_len, d = x.shape
    t = n_seq * seq_len
    xf = x.reshape(t, d)
    gq, gk, gv, gg, dqkv, dz, la, sm = _inproj(xf, w["nmix"], w["wcat"], w["w2p"], w["b2"], w["avec"],
                                               w["dtb"], tm)
    o1, gla_t = _gla(gq, gk, gv, la, gg, jnp.swapaxes(s_gla, -1, -2), w["gla_nw"], n_seq, seq_len, chunk)
    tail0 = jnp.pad(conv_buf, ((0, 0), (SUBLANES - (CONV_W - 1), 0), (0, 0)))
    o2, gdn_new, tail = _gdn(dqkv, sm, dz, tail0, w["cw"], s_gdn, w["gdn_nw"], n_seq, seq_len, chunk)
    x2, hn, idx, gate = _post(o1, o2, xf, w["wo"], w["nffn"], w["wq"], w["k1"], w["k2"], tm)
    wgt = _peer_u(idx, hn.reshape(t, SUBLANES, LANES), gate, w["tab_u"], tb)
    y = _peer_v(idx, wgt, x2.reshape(t, SUBLANES, LANES), nfw3, w["tab_v"], tb)
    return (y.reshape(n_seq, seq_len, d), jnp.swapaxes(gla_t, -1, -2), gdn_new,
            tail[:, SUBLANES - (CONV_W - 1):, :])


def kernel(x_prompt, x_sample, state_gla, state_gdn, state_gdn_conv, norm_mix_w, w_in, gla_w_gk2, gla_b_gk,
           gla_norm_w, gdn_conv_w, gdn_a_log, gdn_dt_bias, gdn_norm_w, w_out, norm_ffn_w, peer_wq, peer_k1,
           peer_k2, peer_u, peer_v, norm_final_w):
    depth = w_in.shape[0]
    assert depth == 1, "the final norm is fused into the last layer's PEER pass"
    n_p, l_p, d = x_prompt.shape
    n_s, l_s, _ = x_sample.shape
    nfw3 = norm_final_w.reshape(SUBLANES, LANES)
    w = _prep_weights(norm_mix_w[0], w_in[0], gla_w_gk2[0], gla_b_gk[0], gla_norm_w[0], gdn_conv_w[0],
                      gdn_a_log[0], gdn_dt_bias[0], gdn_norm_w[0], w_out[0], norm_ffn_w[0], peer_wq[0],
                      peer_k1[0], peer_k2[0], peer_u[0], peer_v[0])
    z_gla = jnp.zeros((n_p, GLA_HEADS, GLA_DK, GLA_DV), F32)
    z_gdn = jnp.zeros((n_p, GDN_HEADS, GDN_DK, GDN_DV), F32)
    z_conv = jnp.zeros((n_p, CONV_W - 1, GDN_QKV), F32)
    chunk_p = min(CHUNK, l_p)
    chunk_s = min(CHUNK, l_s)
    tm_p = min(256, n_p * l_p)
    tm_s = min(256, n_s * l_s)
    yp, gla_p, gdn_p, conv_p = _layer(x_prompt, z_gla, z_gdn, z_conv, w, nfw3, chunk_p, tm_p, min(64, tm_p))
    ys, gla_s, gdn_s, conv_s = _layer(x_sample, state_gla[0], state_gdn[0], state_gdn_conv[0], w, nfw3,
                                      chunk_s, tm_s, min(64, tm_s))
    return (yp, ys, gla_p[None], gdn_p[None], conv_p[None], gla_s[None], gdn_s[None], conv_s[None])
```

```python
import functools

import jax
import jax.numpy as jnp
from jax import lax
from jax.experimental import pallas as pl
from jax.experimental.pallas import tpu as pltpu

F32 = jnp.float32
BF16 = jnp.bfloat16
HI = lax.Precision.HIGHEST

NORM_EPS = 1e-6
CHUNK = 64
GLA_HEADS, GLA_DK, GLA_DV, GLA_LR, GLA_GATE_TAU = 4, 64, 128, 16, 16.0
GDN_HEADS, GDN_DK, GDN_DV, CONV_W = 4, 128, 128, 4
GLA_QK, GLA_V = GLA_HEADS * GLA_DK, GLA_HEADS * GLA_DV
GDN_QK, GDN_V = GDN_HEADS * GDN_DK, GDN_HEADS * GDN_DV
GDN_QKV = 2 * GDN_QK + GDN_V
PEER_HEADS, PEER_QDIM, N_KEYS, PEER_TOPK = 8, 256, 128, 16
PEER_HALF = PEER_QDIM // 2
N_SEL = PEER_HEADS * PEER_TOPK
HALF_EXPERTS = N_KEYS * N_KEYS // 2

LANES = 128
SUBLANES = 8
MIB = 1024 * 1024

SM_A0 = GLA_LR
SM_B0 = GLA_LR + GDN_HEADS


def _cparams(sem, vmem_mib):
    return pltpu.CompilerParams(dimension_semantics=sem, vmem_limit_bytes=vmem_mib * MIB)


def _softplus(x):
    return jnp.maximum(x, 0.0) + jnp.log(1.0 + jnp.exp(-jnp.abs(x)))


def _sigmoid(x):
    return 1.0 / (1.0 + jnp.exp(-x))


def _dot(a, b, prec=None):
    return jnp.dot(a, b, preferred_element_type=F32, precision=prec)


def _dot_nt(a, b, prec=None):
    return lax.dot_general(a, b, (((1,), (1,)), ((), ())), preferred_element_type=F32, precision=prec)


def _dot_tn(a, b, prec=None):
    return lax.dot_general(a, b, (((0,), (0,)), ((), ())), preferred_element_type=F32, precision=prec)


_W_COLS = (("gq", GLA_QK), ("gk", GLA_QK), ("gv", GLA_V), ("gg", GLA_V), ("dqkv", GDN_QKV),
           ("dz", GDN_V), ("sm", LANES))


def _inproj_kernel(x_ref, nw_ref, w_ref, w2_ref, b2_ref, avec_ref, dtb_ref,
                   gq_ref, gk_ref, gv_ref, gg_ref, dqkv_ref, dz_ref, la_ref, sm_ref):
    x = x_ref[...]
    ms = jnp.mean(x * x, axis=-1, keepdims=True)
    h = (x * lax.rsqrt(ms + NORM_EPS) * nw_ref[...]).astype(BF16)
    outs = dict(gq=gq_ref, gk=gk_ref, gv=gv_ref, gg=gg_ref, dqkv=dqkv_ref, dz=dz_ref)
    off = 0
    ps = None
    for name, width in _W_COLS:
        p = _dot(h, w_ref[:, off:off + width])
        off += width
        if name == "sm":
            ps = p
        elif name == "gq":
            gq_ref[...] = p * (GLA_DK ** -0.5)
        else:
            outs[name][...] = p
    z = _dot(ps.astype(BF16), w2_ref[...]) + b2_ref[...]
    la_ref[...] = -_softplus(-z) * (1.0 / GLA_GATE_TAU)
    lane = lax.broadcasted_iota(jnp.int32, ps.shape, 1)
    log_a = avec_ref[...] * _softplus(ps + dtb_ref[...])
    beta = _sigmoid(ps)
    is_a = (lane >= SM_A0) & (lane < SM_B0)
    is_b = (lane >= SM_B0) & (lane < SM_B0 + GDN_HEADS)
    sm_ref[...] = jnp.where(is_a, log_a, jnp.where(is_b, beta, ps))


def _inproj(x, nw, wcat, w2p, b2, avec, dtb, tm):
    t, d = x.shape
    n_w = wcat.shape[1]
    widths = dict(_W_COLS)
    names = ("gq", "gk", "gv", "gg", "dqkv", "dz")
    out_shape = [jax.ShapeDtypeStruct((t, widths[n]), F32) for n in names]
    out_shape += [jax.ShapeDtypeStruct((t, GLA_QK), F32), jax.ShapeDtypeStruct((t, LANES), F32)]
    row = lambda w: pl.BlockSpec((tm, w), lambda i: (i, 0))
    full = lambda a: pl.BlockSpec(a.shape, lambda i: (0,) * a.ndim)
    return pl.pallas_call(
        _inproj_kernel,
        out_shape=out_shape,
        grid=(t // tm,),
        in_specs=[row(d), full(nw), full(wcat), full(w2p), full(b2), full(avec), full(dtb)],
        out_specs=[row(widths[n]) for n in names] + [row(GLA_QK), row(LANES)],
        compiler_params=_cparams(("parallel",), 48),
        name="inproj",
    )(x, nw, wcat, w2p, b2, avec, dtb)


def _iota2(n, m, axis):
    return lax.broadcasted_iota(jnp.int32, (n, m), axis)


def _gla_kernel(q_ref, k_ref, v_ref, la_ref, gg_ref, s0_ref, nw_ref, o_ref, sfin_ref, st_ref, *, chunk):
    c = pl.program_id(1)
    nc = pl.num_programs(1)

    @pl.when(c == 0)
    def _():
        st_ref[...] = s0_ref[0]

    row = _iota2(chunk, chunk, 0)
    col = _iota2(chunk, chunk, 1)
    tril = (row >= col).astype(F32)
    q = q_ref[...]
    k = k_ref[...]
    g = _dot(tril, la_ref[...], HI)
    lane_head = _iota2(chunk, GLA_QK, 1) // GLA_DK

    levels = []
    b = chunk // 2
    while b >= 1:
        sel_r = (col == (row // b) * b).astype(F32)
        sel_c = (col == jnp.minimum((row // b + 1) * b, chunk - 1)).astype(F32)
        ql = q * jnp.exp(g - _dot(sel_r, g, HI))
        kl = k * jnp.exp(_dot(sel_c, g, HI) - g)
        mask = (row // (2 * b) == col // (2 * b)) & (row // b == col // b + 1)
        levels.append((ql, kl, mask))
        b //= 2
    qk_diag = q * k
    g_last = g[chunk - 1:chunk, :]
    q_in = q * jnp.exp(g)
    k_out = k * jnp.exp(g_last - g)
    decay_out = jnp.exp(g_last)

    for h in range(GLA_HEADS):
        hm = (lane_head == h).astype(F32)
        att = jnp.zeros((chunk, chunk), F32)
        for ql, kl, mask in levels:
            att = att + jnp.where(mask, _dot_nt(ql * hm, kl, HI), 0.0)
        dsum = jnp.sum(qk_diag * hm, axis=-1, keepdims=True)
        vh = v_ref[:, h * GLA_DV:(h + 1) * GLA_DV]
        st = st_ref[h]
        dk0 = h * GLA_DK
        o = (_dot_nt(q_in[:, dk0:dk0 + GLA_DK], st, HI) + _dot(att, vh, HI) + dsum * vh)
        st_ref[h] = (st * decay_out[:, dk0:dk0 + GLA_DK]
                     + _dot_tn(vh, k_out[:, dk0:dk0 + GLA_DK], HI))
        ms = jnp.mean(o * o, axis=-1, keepdims=True)
        on = o * lax.rsqrt(ms + NORM_EPS) * nw_ref[...]
        gate = gg_ref[:, h * GLA_DV:(h + 1) * GLA_DV]
        o_ref[:, h * GLA_DV:(h + 1) * GLA_DV] = on * (gate * _sigmoid(gate))

    @pl.when(c == nc - 1)
    def _():
        sfin_ref[0] = st_ref[...]


def _gla(gq, gk, gv, la, gg, s0t, nw, n_seq, seq_len, chunk):
    nc = seq_len // chunk
    t = n_seq * seq_len
    blk = lambda w: pl.BlockSpec((chunk, w), lambda b, c: (b * nc + c, 0))
    st_spec = pl.BlockSpec((1, GLA_HEADS, GLA_DV, GLA_DK), lambda b, c: (b, 0, 0, 0))
    return pl.pallas_call(
        functools.partial(_gla_kernel, chunk=chunk),
        out_shape=[jax.ShapeDtypeStruct((t, GLA_V), F32),
                   jax.ShapeDtypeStruct((n_seq, GLA_HEADS, GLA_DV, GLA_DK), F32)],
        grid=(n_seq, nc),
        in_specs=[blk(GLA_QK), blk(GLA_QK), blk(GLA_V), blk(GLA_QK), blk(GLA_V), st_spec,
                  pl.BlockSpec((1, GLA_DV), lambda b, c: (0, 0))],
        out_specs=[blk(GLA_V), st_spec],
        scratch_shapes=[pltpu.VMEM((GLA_HEADS, GLA_DV, GLA_DK), F32)],
        compiler_params=_cparams(("parallel", "arbitrary"), 32),
        name="gla_scan",
    )(gq, gk, gv, la, gg, s0t, nw)


_INV_BASE = 16


def _unit_lower_inverse(low, row, col, n):
    eye = (row == col).astype(F32)
    base = min(_INV_BASE, n)
    ld = jnp.where(row // base == col // base, low, 0.0)
    inv = eye - ld
    pw = ld
    p = 2
    while p < base:
        pw = _dot(pw, pw, HI)
        inv = inv + _dot(inv, pw, HI)
        p *= 2
    b = base
    while b < n:
        cb = jnp.where((row // (2 * b) == col // (2 * b)) & (row // b != col // b), low, 0.0)
        inv = inv - _dot(inv, _dot(cb, inv, HI), HI)
        b *= 2
    return inv


def _gdn_kernel(x_ref, ab_ref, dz_ref, tail0_ref, cw_ref, s0_ref, nw_ref,
                o_ref, sfin_ref, tail_out_ref, st_ref, tail_ref, *, chunk):
    c = pl.program_id(1)
    nc = pl.num_programs(1)

    @pl.when(c == 0)
    def _():
        st_ref[...] = s0_ref[0]
        tail_ref[...] = tail0_ref[0]

    x = x_ref[...]
    ext = jnp.concatenate([tail_ref[...], x], axis=0)
    base = SUBLANES - (CONV_W - 1)
    conv = ext[base:base + chunk] * cw_ref[0:1, :]
    for i in range(1, CONV_W):
        conv = conv + ext[base + i:base + i + chunk] * cw_ref[i:i + 1, :]
    tail_ref[...] = x[chunk - SUBLANES:chunk]
    conv = conv * _sigmoid(conv)

    row = _iota2(chunk, chunk, 0)
    col = _iota2(chunk, chunk, 1)
    lower = row >= col
    tril = lower.astype(F32)
    ab = ab_ref[...]
    g_col = _dot(tril, ab, HI)
    g_row = _dot_nt(ab.T, tril, HI)

    for h in range(GDN_HEADS):
        cq = conv[:, h * GDN_DK:(h + 1) * GDN_DK]
        ck = conv[:, GDN_QK + h * GDN_DK:GDN_QK + (h + 1) * GDN_DK]
        v = conv[:, 2 * GDN_QK + h * GDN_DV:2 * GDN_QK + (h + 1) * GDN_DV]
        q = cq * lax.rsqrt(jnp.sum(cq * cq, axis=-1, keepdims=True) + NORM_EPS) * (GDN_DK ** -0.5)
        k = ck * lax.rsqrt(jnp.sum(ck * ck, axis=-1, keepdims=True) + NORM_EPS)
        gc = g_col[:, SM_A0 + h:SM_A0 + h + 1]
        gr = g_row[SM_A0 + h:SM_A0 + h + 1, :]
        beta = ab[:, SM_B0 + h:SM_B0 + h + 1]
        dec = jnp.where(lower, jnp.exp(jnp.where(lower, gc - gr, 0.0)), 0.0)
        kk = _dot_nt(k, k, HI)
        low = jnp.where(row > col, beta * kk * dec, 0.0)
        tinv = _unit_lower_inverse(low, row, col, chunk)
        eg = jnp.exp(gc)
        u = _dot(tinv, v * beta, HI)
        w = _dot(tinv, k * (beta * eg), HI)
        st = st_ref[h]
        v_new = u - _dot(w, st, HI)
        qk = _dot_nt(q, k, HI) * dec
        o = _dot(q * eg, st, HI) + _dot(qk, v_new, HI)
        g_last = gc[chunk - 1:chunk, :]
        st_ref[h] = st * jnp.exp(g_last) + _dot_tn(k * jnp.exp(g_last - gc), v_new, HI)
        ms = jnp.mean(o * o, axis=-1, keepdims=True)
        on = o * lax.rsqrt(ms + NORM_EPS) * nw_ref[...]
        gate = dz_ref[:, h * GDN_DV:(h + 1) * GDN_DV]
        o_ref[:, h * GDN_DV:(h + 1) * GDN_DV] = on * (gate * _sigmoid(gate))

    @pl.when(c == nc - 1)
    def _():
        sfin_ref[0] = st_ref[...]
        tail_out_ref[0] = tail_ref[...]


def _gdn(dqkv, ab, dz, tail0, cw, s0, nw, n_seq, seq_len, chunk):
    nc = seq_len // chunk
    t = n_seq * seq_len
    blk = lambda w: pl.BlockSpec((chunk, w), lambda b, c: (b * nc + c, 0))
    st_spec = pl.BlockSpec((1, GDN_HEADS, GDN_DK, GDN_DV), lambda b, c: (b, 0, 0, 0))
    tail_spec = pl.BlockSpec((1, SUBLANES, GDN_QKV), lambda b, c: (b, 0, 0))
    return pl.pallas_call(
        functools.partial(_gdn_kernel, chunk=chunk),
        out_shape=[jax.ShapeDtypeStruct((t, GDN_V), F32),
                   jax.ShapeDtypeStruct((n_seq, GDN_HEADS, GDN_DK, GDN_DV), F32),
                   jax.ShapeDtypeStruct((n_seq, SUBLANES, GDN_QKV), F32)],
        grid=(n_seq, nc),
        in_specs=[blk(GDN_QKV), blk(LANES), blk(GDN_V), tail_spec,
                  pl.BlockSpec((SUBLANES, GDN_QKV), lambda b, c: (0, 0)), st_spec,
                  pl.BlockSpec((1, GDN_DV), lambda b, c: (0, 0))],
        out_specs=[blk(GDN_V), st_spec, tail_spec],
        scratch_shapes=[pltpu.VMEM((GDN_HEADS, GDN_DK, GDN_DV), F32),
                        pltpu.VMEM((SUBLANES, GDN_QKV), F32)],
        compiler_params=_cparams(("parallel", "arbitrary"), 32),
        name="gdn_scan",
    )(dqkv, ab, dz, tail0, cw, s0, nw)


def _topk_cols(s, k, payload=None):
    n = s.shape[0]
    rid = lax.broadcasted_iota(jnp.int32, s.shape, 0).astype(F32)
    vals, idxs = [], []
    for _ in range(k):
        m = jnp.max(s, axis=0, keepdims=True)
        am = jnp.min(jnp.where(s == m, rid, float(n)), axis=0, keepdims=True)
        sel = rid == am
        vals.append(m)
        if payload is None:
            idxs.append(am)
        else:
            idxs.append(jnp.sum(jnp.where(sel, payload, 0.0), axis=0, keepdims=True))
        s = jnp.where(sel, -jnp.inf, s)
    return jnp.concatenate(vals, axis=0), jnp.concatenate(idxs, axis=0)


def _post_kernel(o1_ref, o2_ref, x_ref, wo_ref, nw_ref, wq_ref, k1_ref, k2_ref,
                 x2_ref, hn_ref, j8_ref, sh_ref, gate_ref):
    mix = (_dot(o1_ref[...].astype(BF16), wo_ref[0:GLA_V, :])
           + _dot(o2_ref[...].astype(BF16), wo_ref[GLA_V:GLA_V + GDN_V, :]))
    x2 = x_ref[...] + mix
    x2_ref[...] = x2
    ms = jnp.mean(x2 * x2, axis=-1, keepdims=True)
    hn = x2 * lax.rsqrt(ms + NORM_EPS) * nw_ref[...]
    hn_ref[...] = hn
    qry = _dot(hn.astype(BF16), wq_ref[...])
    idx_rows, gate_rows = [], []
    for h in range(PEER_HEADS):
        q1 = qry[:, h * PEER_QDIM:h * PEER_QDIM + PEER_HALF].astype(BF16)
        q2 = qry[:, h * PEER_QDIM + PEER_HALF:(h + 1) * PEER_QDIM].astype(BF16)
        s1 = _dot_nt(k1_ref[h], q1)
        s2 = _dot_nt(k2_ref[h], q2)
        v1, i1 = _topk_cols(s1, PEER_TOPK)
        v2, i2 = _topk_cols(s2, PEER_TOPK)
        cand = jnp.concatenate([v1[a:a + 1, :] + v2 for a in range(PEER_TOPK)], axis=0)
        cidx = jnp.concatenate([i1[a:a + 1, :] * float(N_KEYS) + i2 for a in range(PEER_TOPK)], axis=0)
        sc, eidx = _topk_cols(cand, PEER_TOPK, payload=cidx)
        e = jnp.exp(sc - jnp.max(sc, axis=0, keepdims=True))
        gate_rows.append(e / jnp.sum(e, axis=0, keepdims=True))
        idx_rows.append(eidx)
    e = jnp.concatenate(idx_rows, axis=0).T.astype(jnp.int32)
    j8_ref[...] = (e & (HALF_EXPERTS - 1)) * SUBLANES
    sh_ref[...] = jnp.where(e >= HALF_EXPERTS, 0.0, 16.0)
    gate_ref[...] = jnp.concatenate(gate_rows, axis=0).T


def _post(o1, o2, x, wo, nw, wq, k1, k2, tm):
    t, d = x.shape
    row = lambda w: pl.BlockSpec((tm, w), lambda i: (i, 0))
    full = lambda a: pl.BlockSpec(a.shape, lambda i: (0,) * a.ndim)
    return pl.pallas_call(
        _post_kernel,
        out_shape=[jax.ShapeDtypeStruct((t, d), F32), jax.ShapeDtypeStruct((t, d), F32),
                   jax.ShapeDtypeStruct((t, N_SEL), jnp.int32), jax.ShapeDtypeStruct((t, N_SEL), F32),
                   jax.ShapeDtypeStruct((t, N_SEL), F32)],
        grid=(t // tm,),
        in_specs=[row(GLA_V), row(GDN_V), row(d), full(wo), full(nw), full(wq), full(k1), full(k2)],
        out_specs=[row(d), row(d), row(N_SEL), row(N_SEL), row(N_SEL)],
        compiler_params=_cparams(("parallel",), 48),
        name="post_topk",
    )(o1, o2, x, wo, nw, wq, k1, k2)


_HI_MASK = -65536
_PRE_TOKENS = 8


def _expert_row(tab_ref, j8, shift_row):
    words = tab_ref[pl.ds(pl.multiple_of(j8, SUBLANES), SUBLANES), :]
    bits = lax.shift_left(words, jnp.broadcast_to(shift_row, (SUBLANES, LANES))) & _HI_MASK
    return lax.bitcast_convert_type(bits, F32)


def _diag_rows(vals, eye):
    n = vals.shape[0]
    return (vals[:, None, :] * eye[None, :, :]).reshape(n * N_SEL, N_SEL)


def _split_bf16(x):
    hi = x.astype(BF16)
    return hi, (x - hi.astype(F32)).astype(BF16)


def _fold_sublanes(vs, sub):
    m4 = sub < 4
    cur = []
    for a, b in zip(vs[0::2], vs[1::2]):
        cur.append(jnp.where(m4, a, b) + pltpu.roll(jnp.where(m4, b, a), 4, 0))
    for half in (2, 1):
        m = (sub % (2 * half)) < half
        nxt = []
        for a, b in zip(cur[0::2], cur[1::2]):
            ta = a + pltpu.roll(a, SUBLANES - half, 0)
            tb = b + pltpu.roll(b, half, 0)
            nxt.append(jnp.where(m, ta, tb))
        cur = nxt
    return cur[0]


def _eye():
    return (_iota2(N_SEL, N_SEL, 0) == _iota2(N_SEL, N_SEL, 1)).astype(F32)


def _peer_u_kernel(j8_ref, sh_ref, hn_ref, gate_ref, tab_ref, w_ref, sd_ref, q_ref, *, tb):
    sub = lax.broadcasted_iota(jnp.int32, (SUBLANES, LANES), 0)
    eye = _eye()
    ones = jnp.ones((N_SEL, LANES), BF16)
    ones2 = jnp.ones((2 * N_SEL, LANES), BF16)
    perm = (0, 4, 2, 6, 1, 5, 3, 7)
    pre_rows = _PRE_TOKENS * N_SEL

    def pre(c, carry):
        toks = pl.ds(pl.multiple_of(c * _PRE_TOKENS, _PRE_TOKENS), _PRE_TOKENS)
        dst = pl.ds(pl.multiple_of(c * pre_rows, pre_rows), pre_rows)
        sd_ref[dst, :] = _dot(_diag_rows(sh_ref[toks, :], eye).astype(BF16), ones).astype(jnp.int32)
        return carry

    lax.fori_loop(0, tb // _PRE_TOKENS, pre, 0)

    def body(t, carry):
        x = hn_ref[t]
        base = t * N_SEL
        folded = []
        for g in range(N_SEL // SUBLANES):
            prods = []
            for r in range(SUBLANES):
                k = g * SUBLANES + perm[r]
                prods.append(_expert_row(tab_ref, j8_ref[t, k], sd_ref[pl.ds(base + k, 1), :]) * x)
            folded.append(_fold_sublanes(prods, sub))
        q_ref[pl.ds(pl.multiple_of(base, N_SEL), N_SEL), :] = jnp.concatenate(folded, axis=0)
        return carry

    lax.fori_loop(0, tb, body, 0)

    def post(c, carry):
        toks = pl.ds(pl.multiple_of(c * _PRE_TOKENS, _PRE_TOKENS), _PRE_TOKENS)
        src = pl.ds(pl.multiple_of(c * pre_rows, pre_rows), pre_rows)
        q_hi, q_lo = _split_bf16(q_ref[src, :])
        rep = _dot(jnp.concatenate([q_hi, q_lo], axis=1), ones2)
        act = jnp.sum(rep.reshape(_PRE_TOKENS, N_SEL, LANES) * eye[None, :, :], axis=1)
        gelu = 0.5 * act * (1.0 + jnp.tanh(0.7978845608028654 * (act + 0.044715 * act * act * act)))
        w_ref[toks, :] = gate_ref[toks, :] * gelu
        return carry

    lax.fori_loop(0, tb // _PRE_TOKENS, post, 0)


def _peer_u(j8, sh, hn3, gate, tab, tb):
    t = j8.shape[0]
    vec = lambda: pl.BlockSpec((tb, N_SEL), lambda i: (i, 0))
    return pl.pallas_call(
        functools.partial(_peer_u_kernel, tb=tb),
        out_shape=jax.ShapeDtypeStruct((t, N_SEL), F32),
        grid=(t // tb,),
        in_specs=[pl.BlockSpec((tb, N_SEL), lambda i: (i, 0), memory_space=pltpu.SMEM), vec(),
                  pl.BlockSpec((tb, SUBLANES, LANES), lambda i: (i, 0, 0)), vec(),
                  pl.BlockSpec(memory_space=pltpu.VMEM)],
        out_specs=vec(),
        scratch_shapes=[pltpu.VMEM((tb * N_SEL, LANES), jnp.int32), pltpu.VMEM((tb * N_SEL, LANES), F32)],
        compiler_params=_cparams(("parallel",), 56),
        name="peer_u",
    )(j8, sh, hn3, gate, tab)


def _peer_v_kernel(j8_ref, sh_ref, wgt_ref, x2_ref, nw_ref, tab_ref, y_ref, sd_ref, wd_ref, *, tb, n_acc):
    eye = _eye()
    ones = jnp.ones((N_SEL, LANES), BF16)
    ones2 = jnp.ones((2 * N_SEL, LANES), BF16)
    pre_rows = _PRE_TOKENS * N_SEL

    def pre(c, carry):
        toks = pl.ds(pl.multiple_of(c * _PRE_TOKENS, _PRE_TOKENS), _PRE_TOKENS)
        dst = pl.ds(pl.multiple_of(c * pre_rows, pre_rows), pre_rows)
        sd_ref[dst, :] = _dot(_diag_rows(sh_ref[toks, :], eye).astype(BF16), ones).astype(jnp.int32)
        w_hi, w_lo = _split_bf16(wgt_ref[toks, :])
        dw = jnp.concatenate([_diag_rows(w_hi.astype(F32), eye).astype(BF16),
                              _diag_rows(w_lo.astype(F32), eye).astype(BF16)], axis=1)
        wd_ref[dst, :] = _dot(dw, ones2)
        return carry

    lax.fori_loop(0, tb // _PRE_TOKENS, pre, 0)

    def body(t, carry):
        base = t * N_SEL
        accs = [jnp.zeros((SUBLANES, LANES), F32) for _ in range(n_acc)]
        for k in range(N_SEL):
            row = _expert_row(tab_ref, j8_ref[t, k], sd_ref[pl.ds(base + k, 1), :])
            accs[k % n_acc] = accs[k % n_acc] + row * wd_ref[pl.ds(base + k, 1), :]
        tot = accs[0]
        for a in accs[1:]:
            tot = tot + a
        y_ref[t] = x2_ref[t] + tot
        return carry

    lax.fori_loop(0, tb, body, 0)
    x3 = y_ref[...]
    ss = jnp.sum(jnp.sum(x3 * x3, axis=2, keepdims=True), axis=1, keepdims=True)
    y_ref[...] = x3 * lax.rsqrt(ss * (1.0 / (SUBLANES * LANES)) + NORM_EPS) * nw_ref[...][None]


def _peer_v(j8, sh, wgt, x23, nw3, tab, tb):
    t = j8.shape[0]
    vec = lambda: pl.BlockSpec((tb, N_SEL), lambda i: (i, 0))
    return pl.pallas_call(
        functools.partial(_peer_v_kernel, tb=tb, n_acc=4),
        out_shape=jax.ShapeDtypeStruct((t, SUBLANES, LANES), F32),
        grid=(t // tb,),
        in_specs=[pl.BlockSpec((tb, N_SEL), lambda i: (i, 0), memory_space=pltpu.SMEM), vec(), vec(),
                  pl.BlockSpec((tb, SUBLANES, LANES), lambda i: (i, 0, 0)),
                  pl.BlockSpec((SUBLANES, LANES), lambda i: (0, 0)),
                  pl.BlockSpec(memory_space=pltpu.VMEM)],
        out_specs=pl.BlockSpec((tb, SUBLANES, LANES), lambda i: (i, 0, 0)),
        scratch_shapes=[pltpu.VMEM((tb * N_SEL, LANES), jnp.int32), pltpu.VMEM((tb * N_SEL, LANES), F32)],
        compiler_params=_cparams(("parallel",), 56),
        name="peer_v",
    )(j8, sh, wgt, x23, nw3, tab)


def _pack_table(tab):
    n, d = tab.shape
    bits = lax.bitcast_convert_type(tab.astype(BF16), jnp.uint16).astype(jnp.uint32)
    packed = (bits[n // 2:] << 16) | bits[:n // 2]
    return lax.bitcast_convert_type(packed, jnp.int32).reshape(n // 2 * (d // LANES), LANES)


def _prep_weights(norm_mix_w, w_in, gla_w_gk2, gla_b_gk, gla_norm_w, gdn_conv_w, gdn_a_log,
                  gdn_dt_bias, gdn_norm_w, w_out, norm_ffn_w, peer_wq, peer_k1, peer_k2, peer_u, peer_v):
    d = w_in.shape[0]
    o_glr = 2 * GLA_QK + GLA_V
    o_gg = o_glr + GLA_LR
    o_qkv = o_gg + GLA_V
    o_da = o_qkv + GDN_QKV
    o_dz = o_da + 2 * GDN_HEADS
    small = jnp.concatenate([w_in[:, o_glr:o_gg], w_in[:, o_da:o_dz],
                             jnp.zeros((d, LANES - GLA_LR - 2 * GDN_HEADS), w_in.dtype)], axis=1)
    wcat = jnp.concatenate([w_in[:, :o_glr], w_in[:, o_gg:o_qkv], w_in[:, o_qkv:o_da],
                            w_in[:, o_dz:], small], axis=1).astype(BF16)
    w2p = jnp.zeros((LANES, GLA_QK), F32).at[:GLA_LR].set(gla_w_gk2).astype(BF16)
    avec = jnp.zeros((1, LANES), F32).at[0, SM_A0:SM_B0].set(-jnp.exp(gdn_a_log))
    dtb = jnp.zeros((1, LANES), F32).at[0, SM_A0:SM_B0].set(gdn_dt_bias)
    cw = jnp.zeros((SUBLANES, GDN_QKV), F32).at[:CONV_W].set(gdn_conv_w)
    return dict(
        nmix=norm_mix_w.reshape(1, d), wcat=wcat, w2p=w2p, b2=gla_b_gk.reshape(1, GLA_QK), avec=avec, dtb=dtb,
        gla_nw=gla_norm_w.reshape(1, GLA_DV), cw=cw, gdn_nw=gdn_norm_w.reshape(1, GDN_DV),
        wo=w_out.astype(BF16), nffn=norm_ffn_w.reshape(1, d), wq=peer_wq.astype(BF16),
        k1=peer_k1.astype(BF16), k2=peer_k2.astype(BF16), tab_u=_pack_table(peer_u), tab_v=_pack_table(peer_v))


def _layer(x, s_gla, s_gdn, conv_buf, w, nfw3, chunk, tm, tb):
    n_seq, seq_len, d = x.shape
    t = n_seq * seq_len
    xf = x.reshape(t, d)
    gq, gk, gv, gg, dqkv, dz, la, sm = _inproj(xf, w["nmix"], w["wcat"], w["w2p"], w["b2"], w["avec"],
                                               w["dtb"], tm)
    o1, gla_t = _gla(gq, gk, gv, la, gg, jnp.swapaxes(s_gla, -1, -2), w["gla_nw"], n_seq, seq_len, chunk)
    tail0 = jnp.pad(conv_buf, ((0, 0), (SUBLANES - (CONV_W - 1), 0), (0, 0)))
    o2, gdn_new, tail = _gdn(dqkv, sm, dz, tail0, w["cw"], s_gdn, w["gdn_nw"], n_seq, seq_len, chunk)
    x2, hn, j8, sh, gate = _post(o1, o2, xf, w["wo"], w["nffn"], w["wq"], w["k1"], w["k2"], tm)
    wgt = _peer_u(j8, sh, hn.reshape(t, SUBLANES, LANES), gate, w["tab_u"], tb)
    y = _peer_v(j8, sh, wgt, x2.reshape(t, SUBLANES, LANES), nfw3, w["tab_v"], tb)
    return (y.reshape(n_seq, seq_len, d), jnp.swapaxes(gla_t, -1, -2), gdn_new,
            tail[:, SUBLANES - (CONV_W - 1):, :])


def kernel(x_prompt, x_sample, state_gla, state_gdn, state_gdn_conv, norm_mix_w, w_in, gla_w_gk2, gla_b_gk,
           gla_norm_w, gdn_conv_w, gdn_a_log, gdn_dt_bias, gdn_norm_w, w_out, norm_ffn_w, peer_wq, peer_k1,
           peer_k2, peer_u, peer_v, norm_final_w):
    depth = w_in.shape[0]
    assert depth == 1, "the final norm is fused into the last layer's PEER pass"
    n_p, l_p, d = x_prompt.shape
    n_s, l_s, _ = x_sample.shape
    nfw3 = norm_final_w.reshape(SUBLANES, LANES)
    w = _prep_weights(norm_mix_w[0], w_in[0], gla_w_gk2[0], gla_b_gk[0], gla_norm_w[0], gdn_conv_w[0],
                      gdn_a_log[0], gdn_dt_bias[0], gdn_norm_w[0], w_out[0], norm_ffn_w[0], peer_wq[0],
                      peer_k1[0], peer_k2[0], peer_u[0], peer_v[0])
    z_gla = jnp.zeros((n_p, GLA_HEADS, GLA_DK, GLA_DV), F32)
    z_gdn = jnp.zeros((n_p, GDN_HEADS, GDN_DK, GDN_DV), F32)
    z_conv = jnp.zeros((n_p, CONV_W - 1, GDN_QKV), F32)
    chunk_p = min(CHUNK, l_p)
    chunk_s = min(CHUNK, l_s)
    tm_p = min(256, n_p * l_p)
    tm_s = min(256, n_s * l_s)
    yp, gla_p, gdn_p, conv_p = _layer(x_prompt, z_gla, z_gdn, z_conv, w, nfw3, chunk_p, tm_p, min(64, tm_p))
    ys, gla_s, gdn_s, conv_s = _layer(x_sample, state_gla[0], state_gdn[0], state_gdn_conv[0], w, nfw3,
                                      chunk_s, tm_s, min(64, tm_s))
    return (yp, ys, gla_p[None], gdn_p[None], conv_p[None], gla_s[None], gdn_s[None], conv_s[None])
```

```python
import functools

import jax
import jax.numpy as jnp
from jax import lax
from jax.experimental import pallas as pl
from jax.experimental.pallas import tpu as pltpu

F32 = jnp.float32
BF16 = jnp.bfloat16
HI = lax.Precision.HIGHEST

NORM_EPS = 1e-6
CHUNK = 64
GLA_HEADS, GLA_DK, GLA_DV, GLA_LR, GLA_GATE_TAU = 4, 64, 128, 16, 16.0
GDN_HEADS, GDN_DK, GDN_DV, CONV_W = 4, 128, 128, 4
GLA_QK, GLA_V = GLA_HEADS * GLA_DK, GLA_HEADS * GLA_DV
GDN_QK, GDN_V = GDN_HEADS * GDN_DK, GDN_HEADS * GDN_DV
GDN_QKV = 2 * GDN_QK + GDN_V
PEER_HEADS, PEER_QDIM, N_KEYS, PEER_TOPK = 8, 256, 128, 16
PEER_HALF = PEER_QDIM // 2
N_SEL = PEER_HEADS * PEER_TOPK
HALF_EXPERTS = N_KEYS * N_KEYS // 2

LANES = 128
SUBLANES = 8
MIB = 1024 * 1024

SM_A0 = GLA_LR
SM_B0 = GLA_LR + GDN_HEADS


def _cparams(sem, vmem_mib):
    return pltpu.CompilerParams(dimension_semantics=sem, vmem_limit_bytes=vmem_mib * MIB)


def _softplus(x):
    return jnp.maximum(x, 0.0) + jnp.log(1.0 + jnp.exp(-jnp.abs(x)))


def _sigmoid(x):
    return 1.0 / (1.0 + jnp.exp(-x))


def _dot(a, b, prec=None):
    return jnp.dot(a, b, preferred_element_type=F32, precision=prec)


def _dot_nt(a, b, prec=None):
    return lax.dot_general(a, b, (((1,), (1,)), ((), ())), preferred_element_type=F32, precision=prec)


def _dot_tn(a, b, prec=None):
    return lax.dot_general(a, b, (((0,), (0,)), ((), ())), preferred_element_type=F32, precision=prec)


def _split_bf16(x):
    hi = x.astype(BF16)
    return hi, (x - hi.astype(F32)).astype(BF16)


def _split3_bf16(x):
    hi = x.astype(BF16)
    r = x - hi.astype(F32)
    mid = r.astype(BF16)
    return hi, mid, (r - mid.astype(F32)).astype(BF16)


def _dot3(a, b):
    ah, al = _split_bf16(a)
    bh, bl = _split_bf16(b)
    return _dot(jnp.concatenate([ah, ah, al], axis=1), jnp.concatenate([bh, bl, bh], axis=0))


def _dot3_nt(a, b):
    ah, al = _split_bf16(a)
    bh, bl = _split_bf16(b)
    return _dot_nt(jnp.concatenate([ah, ah, al], axis=1), jnp.concatenate([bh, bl, bh], axis=1))


def _dot3_tn(a, b):
    ah, al = _split_bf16(a)
    bh, bl = _split_bf16(b)
    return _dot_tn(jnp.concatenate([ah, ah, al], axis=0), jnp.concatenate([bh, bl, bh], axis=0))


def _dot_exact_lhs(a01, b):
    a16 = a01.astype(BF16)
    return _dot(jnp.concatenate([a16, a16, a16], axis=1), jnp.concatenate(_split3_bf16(b), axis=0))


def _dot_nt_exact_rhs(a, b01):
    b16 = b01.astype(BF16)
    return _dot_nt(jnp.concatenate(_split3_bf16(a), axis=1), jnp.concatenate([b16, b16, b16], axis=1))


_W_COLS = (("gq", GLA_QK), ("gk", GLA_QK), ("gv", GLA_V), ("gg", GLA_V), ("dqkv", GDN_QKV),
           ("dz", GDN_V), ("sm", LANES))


def _inproj_kernel(x_ref, nw_ref, w_ref, w2_ref, b2_ref, avec_ref, dtb_ref,
                   gq_ref, gk_ref, gv_ref, gg_ref, dqkv_ref, dz_ref, la_ref, sm_ref):
    x = x_ref[...]
    ms = jnp.mean(x * x, axis=-1, keepdims=True)
    h = (x * lax.rsqrt(ms + NORM_EPS) * nw_ref[...]).astype(BF16)
    outs = dict(gq=gq_ref, gk=gk_ref, gv=gv_ref, gg=gg_ref, dqkv=dqkv_ref, dz=dz_ref)
    off = 0
    ps = None
    for name, width in _W_COLS:
        p = _dot(h, w_ref[:, off:off + width])
        off += width
        if name == "sm":
            ps = p
        elif name == "gq":
            gq_ref[...] = p * (GLA_DK ** -0.5)
        else:
            outs[name][...] = p
    z = _dot(ps.astype(BF16), w2_ref[...]) + b2_ref[...]
    la_ref[...] = -_softplus(-z) * (1.0 / GLA_GATE_TAU)
    lane = lax.broadcasted_iota(jnp.int32, ps.shape, 1)
    log_a = avec_ref[...] * _softplus(ps + dtb_ref[...])
    beta = _sigmoid(ps)
    is_a = (lane >= SM_A0) & (lane < SM_B0)
    is_b = (lane >= SM_B0) & (lane < SM_B0 + GDN_HEADS)
    sm_ref[...] = jnp.where(is_a, log_a, jnp.where(is_b, beta, ps))


def _inproj(x, nw, wcat, w2p, b2, avec, dtb, tm):
    t, d = x.shape
    n_w = wcat.shape[1]
    widths = dict(_W_COLS)
    names = ("gq", "gk", "gv", "gg", "dqkv", "dz")
    out_shape = [jax.ShapeDtypeStruct((t, widths[n]), F32) for n in names]
    out_shape += [jax.ShapeDtypeStruct((t, GLA_QK), F32), jax.ShapeDtypeStruct((t, LANES), F32)]
    row = lambda w: pl.BlockSpec((tm, w), lambda i: (i, 0))
    full = lambda a: pl.BlockSpec(a.shape, lambda i: (0,) * a.ndim)
    return pl.pallas_call(
        _inproj_kernel,
        out_shape=out_shape,
        grid=(t // tm,),
        in_specs=[row(d), full(nw), full(wcat), full(w2p), full(b2), full(avec), full(dtb)],
        out_specs=[row(widths[n]) for n in names] + [row(GLA_QK), row(LANES)],
        compiler_params=_cparams(("parallel",), 48),
        name="inproj",
    )(x, nw, wcat, w2p, b2, avec, dtb)


def _iota2(n, m, axis):
    return lax.broadcasted_iota(jnp.int32, (n, m), axis)


def _gla_kernel(q_ref, k_ref, v_ref, la_ref, gg_ref, s0_ref, nw_ref, o_ref, sfin_ref, st_ref, *, chunk):
    c = pl.program_id(1)
    nc = pl.num_programs(1)

    @pl.when(c == 0)
    def _():
        st_ref[...] = s0_ref[0]

    row = _iota2(chunk, chunk, 0)
    col = _iota2(chunk, chunk, 1)
    q = q_ref[...]
    k = k_ref[...]

    blocks = []
    b = chunk // 2
    while b >= 1:
        blocks.append(b)
        b //= 2
    sel = [row >= col]
    for b in blocks:
        sel.append(col <= (row // b) * b)
        sel.append(col <= jnp.minimum((row // b + 1) * b, chunk - 1))
    g_all = _dot_exact_lhs(jnp.concatenate([m.astype(F32) for m in sel], axis=0), la_ref[...])
    g = g_all[0:chunk]
    levels = []
    for li, b in enumerate(blocks):
        ref_r = g_all[(1 + 2 * li) * chunk:(2 + 2 * li) * chunk]
        ref_c = g_all[(2 + 2 * li) * chunk:(3 + 2 * li) * chunk]
        mask = (row // (2 * b) == col // (2 * b)) & (row // b == col // b + 1)
        levels.append((q * jnp.exp(g - ref_r), k * jnp.exp(ref_c - g), mask))
    qk_diag = q * k
    g_last = g[chunk - 1:chunk, :]
    q_in = q * jnp.exp(g)
    k_out = k * jnp.exp(g_last - g)
    decay_out = jnp.exp(g_last)

    heads = range(GLA_HEADS)
    dks = [slice(h * GLA_DK, (h + 1) * GLA_DK) for h in heads]
    dvs = [slice(h * GLA_DV, (h + 1) * GLA_DV) for h in heads]
    atts = [jnp.zeros((chunk, chunk), F32) for _ in heads]
    for ql, kl, mask in levels:
        atts = [att + jnp.where(mask, _dot3_nt(ql[:, dk], kl[:, dk]), 0.0) for att, dk in zip(atts, dks)]
    vhs = [v_ref[:, dv] for dv in dvs]
    sts = [st_ref[h] for h in heads]
    os_ = []
    for h in heads:
        dsum = jnp.sum(qk_diag[:, dks[h]], axis=-1, keepdims=True)
        os_.append(_dot3_nt(q_in[:, dks[h]], sts[h]) + _dot3(atts[h], vhs[h]) + dsum * vhs[h])
    for h in heads:
        st_ref[h] = sts[h] * decay_out[:, dks[h]] + _dot3_tn(vhs[h], k_out[:, dks[h]])
    for h in heads:
        o = os_[h]
        ms = jnp.mean(o * o, axis=-1, keepdims=True)
        on = o * lax.rsqrt(ms + NORM_EPS) * nw_ref[...]
        gate = gg_ref[:, dvs[h]]
        o_ref[:, dvs[h]] = on * (gate * _sigmoid(gate))

    @pl.when(c == nc - 1)
    def _():
        sfin_ref[0] = st_ref[...]


def _gla(gq, gk, gv, la, gg, s0t, nw, n_seq, seq_len, chunk):
    nc = seq_len // chunk
    t = n_seq * seq_len
    blk = lambda w: pl.BlockSpec((chunk, w), lambda b, c: (b * nc + c, 0))
    st_spec = pl.BlockSpec((1, GLA_HEADS, GLA_DV, GLA_DK), lambda b, c: (b, 0, 0, 0))
    return pl.pallas_call(
        functools.partial(_gla_kernel, chunk=chunk),
        out_shape=[jax.ShapeDtypeStruct((t, GLA_V), F32),
                   jax.ShapeDtypeStruct((n_seq, GLA_HEADS, GLA_DV, GLA_DK), F32)],
        grid=(n_seq, nc),
        in_specs=[blk(GLA_QK), blk(GLA_QK), blk(GLA_V), blk(GLA_QK), blk(GLA_V), st_spec,
                  pl.BlockSpec((1, GLA_DV), lambda b, c: (0, 0))],
        out_specs=[blk(GLA_V), st_spec],
        scratch_shapes=[pltpu.VMEM((GLA_HEADS, GLA_DV, GLA_DK), F32)],
        compiler_params=_cparams(("parallel", "arbitrary"), 32),
        name="gla_scan",
    )(gq, gk, gv, la, gg, s0t, nw)


_INV_BASE = 16


def _unit_lower_inverses(lows, row, col, n):
    eye = (row == col).astype(F32)
    base = min(_INV_BASE, n)
    diag_blk = row // base == col // base
    pws = [jnp.where(diag_blk, low, 0.0) for low in lows]
    invs = [eye - ld for ld in pws]
    p = 2
    while p < base:
        pws = [_dot3(pw, pw) for pw in pws]
        invs = [inv + _dot3(inv, pw) for inv, pw in zip(invs, pws)]
        p *= 2
    b = base
    while b < n:
        sib = (row // (2 * b) == col // (2 * b)) & (row // b != col // b)
        tmps = [_dot3(jnp.where(sib, low, 0.0), inv) for low, inv in zip(lows, invs)]
        invs = [inv - _dot3(inv, t) for inv, t in zip(invs, tmps)]
        b *= 2
    return invs


def _gdn_kernel(x_ref, ab_ref, dz_ref, tail0_ref, cw_ref, s0_ref, nw_ref,
                o_ref, sfin_ref, tail_out_ref, st_ref, tail_ref, *, chunk):
    c = pl.program_id(1)
    nc = pl.num_programs(1)

    @pl.when(c == 0)
    def _():
        st_ref[...] = s0_ref[0]
        tail_ref[...] = tail0_ref[0]

    x = x_ref[...]
    ext = jnp.concatenate([tail_ref[...], x], axis=0)
    base = SUBLANES - (CONV_W - 1)
    conv = ext[base:base + chunk] * cw_ref[0:1, :]
    for i in range(1, CONV_W):
        conv = conv + ext[base + i:base + i + chunk] * cw_ref[i:i + 1, :]
    tail_ref[...] = x[chunk - SUBLANES:chunk]
    conv = conv * _sigmoid(conv)

    row = _iota2(chunk, chunk, 0)
    col = _iota2(chunk, chunk, 1)
    lower = row >= col
    tril = lower.astype(F32)
    ab = ab_ref[...]
    g_col = _dot_exact_lhs(tril, ab)
    g_row = _dot_nt_exact_rhs(ab.T, tril)

    heads = range(GDN_HEADS)
    qs, ks, vs, gcs, betas, decs = [], [], [], [], [], []
    for h in heads:
        cq = conv[:, h * GDN_DK:(h + 1) * GDN_DK]
        ck = conv[:, GDN_QK + h * GDN_DK:GDN_QK + (h + 1) * GDN_DK]
        vs.append(conv[:, 2 * GDN_QK + h * GDN_DV:2 * GDN_QK + (h + 1) * GDN_DV])
        qs.append(cq * lax.rsqrt(jnp.sum(cq * cq, axis=-1, keepdims=True) + NORM_EPS) * (GDN_DK ** -0.5))
        ks.append(ck * lax.rsqrt(jnp.sum(ck * ck, axis=-1, keepdims=True) + NORM_EPS))
        gc = g_col[:, SM_A0 + h:SM_A0 + h + 1]
        gr = g_row[SM_A0 + h:SM_A0 + h + 1, :]
        gcs.append(gc)
        betas.append(ab[:, SM_B0 + h:SM_B0 + h + 1])
        decs.append(jnp.where(lower, jnp.exp(jnp.where(lower, gc - gr, 0.0)), 0.0))
    kks = [_dot3_nt(k, k) for k in ks]
    qks = [_dot3_nt(q, k) * dec for q, k, dec in zip(qs, ks, decs)]
    lows = [jnp.where(row > col, beta * kk * dec, 0.0) for beta, kk, dec in zip(betas, kks, decs)]
    tinvs = _unit_lower_inverses(lows, row, col, chunk)
    egs = [jnp.exp(gc) for gc in gcs]
    uws = [_dot3(tinv, jnp.concatenate([v * beta, k * (beta * eg)], axis=1))
           for tinv, v, k, beta, eg in zip(tinvs, vs, ks, betas, egs)]
    sts = [st_ref[h] for h in heads]
    v_news = [uw[:, :GDN_DV] - _dot3(uw[:, GDN_DV:], st) for uw, st in zip(uws, sts)]
    os_ = [_dot3(q * eg, st) + _dot3(qk, v_new) for q, eg, st, qk, v_new in zip(qs, egs, sts, qks, v_news)]
    for h in heads:
        g_last = gcs[h][chunk - 1:chunk, :]
        st_ref[h] = sts[h] * jnp.exp(g_last) + _dot3_tn(ks[h] * jnp.exp(g_last - gcs[h]), v_news[h])
    for h in heads:
        o = os_[h]
        ms = jnp.mean(o * o, axis=-1, keepdims=True)
        on = o * lax.rsqrt(ms + NORM_EPS) * nw_ref[...]
        gate = dz_ref[:, h * GDN_DV:(h + 1) * GDN_DV]
        o_ref[:, h * GDN_DV:(h + 1) * GDN_DV] = on * (gate * _sigmoid(gate))

    @pl.when(c == nc - 1)
    def _():
        sfin_ref[0] = st_ref[...]
        tail_out_ref[0] = tail_ref[...]


def _gdn(dqkv, ab, dz, tail0, cw, s0, nw, n_seq, seq_len, chunk):
    nc = seq_len // chunk
    t = n_seq * seq_len
    blk = lambda w: pl.BlockSpec((chunk, w), lambda b, c: (b * nc + c, 0))
    st_spec = pl.BlockSpec((1, GDN_HEADS, GDN_DK, GDN_DV), lambda b, c: (b, 0, 0, 0))
    tail_spec = pl.BlockSpec((1, SUBLANES, GDN_QKV), lambda b, c: (b, 0, 0))
    return pl.pallas_call(
        functools.partial(_gdn_kernel, chunk=chunk),
        out_shape=[jax.ShapeDtypeStruct((t, GDN_V), F32),
                   jax.ShapeDtypeStruct((n_seq, GDN_HEADS, GDN_DK, GDN_DV), F32),
                   jax.ShapeDtypeStruct((n_seq, SUBLANES, GDN_QKV), F32)],
        grid=(n_seq, nc),
        in_specs=[blk(GDN_QKV), blk(LANES), blk(GDN_V), tail_spec,
                  pl.BlockSpec((SUBLANES, GDN_QKV), lambda b, c: (0, 0)), st_spec,
                  pl.BlockSpec((1, GDN_DV), lambda b, c: (0, 0))],
        out_specs=[blk(GDN_V), st_spec, tail_spec],
        scratch_shapes=[pltpu.VMEM((GDN_HEADS, GDN_DK, GDN_DV), F32),
                        pltpu.VMEM((SUBLANES, GDN_QKV), F32)],
        compiler_params=_cparams(("parallel", "arbitrary"), 32),
        name="gdn_scan",
    )(dqkv, ab, dz, tail0, cw, s0, nw)


def _topk_cols(s, k, payload=None):
    n = s.shape[0]
    rid = lax.broadcasted_iota(jnp.int32, s.shape, 0).astype(F32)
    vals, idxs = [], []
    for _ in range(k):
        m = jnp.max(s, axis=0, keepdims=True)
        am = jnp.min(jnp.where(s == m, rid, float(n)), axis=0, keepdims=True)
        sel = rid == am
        vals.append(m)
        if payload is None:
            idxs.append(am)
        else:
            idxs.append(jnp.sum(jnp.where(sel, payload, 0.0), axis=0, keepdims=True))
        s = jnp.where(sel, -jnp.inf, s)
    return jnp.concatenate(vals, axis=0), jnp.concatenate(idxs, axis=0)


def _post_kernel(o1_ref, o2_ref, x_ref, wo_ref, nw_ref, wq_ref, k1_ref, k2_ref,
                 x2_ref, hn_ref, j8_ref, sh_ref, gate_ref):
    mix = (_dot(o1_ref[...].astype(BF16), wo_ref[0:GLA_V, :])
           + _dot(o2_ref[...].astype(BF16), wo_ref[GLA_V:GLA_V + GDN_V, :]))
    x2 = x_ref[...] + mix
    x2_ref[...] = x2
    ms = jnp.mean(x2 * x2, axis=-1, keepdims=True)
    hn = x2 * lax.rsqrt(ms + NORM_EPS) * nw_ref[...]
    hn_ref[...] = hn
    qry = _dot(hn.astype(BF16), wq_ref[...])
    idx_rows, gate_rows = [], []
    for h in range(PEER_HEADS):
        q1 = qry[:, h * PEER_QDIM:h * PEER_QDIM + PEER_HALF].astype(BF16)
        q2 = qry[:, h * PEER_QDIM + PEER_HALF:(h + 1) * PEER_QDIM].astype(BF16)
        s1 = _dot_nt(k1_ref[h], q1)
        s2 = _dot_nt(k2_ref[h], q2)
        v1, i1 = _topk_cols(s1, PEER_TOPK)
        v2, i2 = _topk_cols(s2, PEER_TOPK)
        cand_rows, cidx_rows = [], []
        for a in range(PEER_TOPK // 2):
            n_valid = PEER_TOPK // (a + 1)
            n_rows = -(-n_valid // SUBLANES) * SUBLANES
            c = v1[a:a + 1, :] + v2[0:n_rows, :]
            if n_valid < n_rows:
                c = jnp.where(lax.broadcasted_iota(jnp.int32, c.shape, 0) < n_valid, c, -jnp.inf)
            cand_rows.append(c)
            cidx_rows.append(i1[a:a + 1, :] * float(N_KEYS) + i2[0:n_rows, :])
        cand_rows.append(v1[PEER_TOPK // 2:, :] + v2[0:1, :])
        cidx_rows.append(i1[PEER_TOPK // 2:, :] * float(N_KEYS) + i2[0:1, :])
        sc, eidx = _topk_cols(jnp.concatenate(cand_rows, axis=0), PEER_TOPK,
                              payload=jnp.concatenate(cidx_rows, axis=0))
        e = jnp.exp(sc - jnp.max(sc, axis=0, keepdims=True))
        gate_rows.append(e / jnp.sum(e, axis=0, keepdims=True))
        idx_rows.append(eidx)
    e = jnp.concatenate(idx_rows, axis=0).T.astype(jnp.int32)
    j8_ref[...] = (e & (HALF_EXPERTS - 1)) * SUBLANES
    sh_ref[...] = jnp.where(e >= HALF_EXPERTS, 0.0, 16.0)
    gate_ref[...] = jnp.concatenate(gate_rows, axis=0).T


def _post(o1, o2, x, wo, nw, wq, k1, k2, tm):
    t, d = x.shape
    row = lambda w: pl.BlockSpec((tm, w), lambda i: (i, 0))
    full = lambda a: pl.BlockSpec(a.shape, lambda i: (0,) * a.ndim)
    return pl.pallas_call(
        _post_kernel,
        out_shape=[jax.ShapeDtypeStruct((t, d), F32), jax.ShapeDtypeStruct((t, d), F32),
                   jax.ShapeDtypeStruct((t, N_SEL), jnp.int32), jax.ShapeDtypeStruct((t, N_SEL), F32),
                   jax.ShapeDtypeStruct((t, N_SEL), F32)],
        grid=(t // tm,),
        in_specs=[row(GLA_V), row(GDN_V), row(d), full(wo), full(nw), full(wq), full(k1), full(k2)],
        out_specs=[row(d), row(d), row(N_SEL), row(N_SEL), row(N_SEL)],
        compiler_params=_cparams(("parallel",), 48),
        name="post_topk",
    )(o1, o2, x, wo, nw, wq, k1, k2)


_HI_MASK = -65536
_PRE_TOKENS = 8


def _expert_row(tab_ref, j8, shift_row):
    words = tab_ref[pl.ds(pl.multiple_of(j8, SUBLANES), SUBLANES), :]
    bits = lax.shift_left(words, jnp.broadcast_to(shift_row, (SUBLANES, LANES))) & _HI_MASK
    return lax.bitcast_convert_type(bits, F32)


def _diag_rows(vals, eye):
    n = vals.shape[0]
    return (vals[:, None, :] * eye[None, :, :]).reshape(n * N_SEL, N_SEL)


def _fold_sublanes(vs, sub):
    m4 = sub < 4
    cur = []
    for a, b in zip(vs[0::2], vs[1::2]):
        cur.append(jnp.where(m4, a, b) + pltpu.roll(jnp.where(m4, b, a), 4, 0))
    for half in (2, 1):
        m = (sub % (2 * half)) < half
        nxt = []
        for a, b in zip(cur[0::2], cur[1::2]):
            ta = a + pltpu.roll(a, SUBLANES - half, 0)
            tb = b + pltpu.roll(b, half, 0)
            nxt.append(jnp.where(m, ta, tb))
        cur = nxt
    return cur[0]


def _eye():
    return (_iota2(N_SEL, N_SEL, 0) == _iota2(N_SEL, N_SEL, 1)).astype(F32)


def _peer_u_kernel(j8_ref, sh_ref, hn_ref, gate_ref, tab_ref, w_ref, sd_ref, q_ref, *, tb):
    sub = lax.broadcasted_iota(jnp.int32, (SUBLANES, LANES), 0)
    eye = _eye()
    ones = jnp.ones((N_SEL, LANES), BF16)
    ones2 = jnp.ones((2 * N_SEL, LANES), BF16)
    perm = (0, 4, 2, 6, 1, 5, 3, 7)
    pre_rows = _PRE_TOKENS * N_SEL

    def pre(c, carry):
        toks = pl.ds(pl.multiple_of(c * _PRE_TOKENS, _PRE_TOKENS), _PRE_TOKENS)
        dst = pl.ds(pl.multiple_of(c * pre_rows, pre_rows), pre_rows)
        sd_ref[dst, :] = _dot(_diag_rows(sh_ref[toks, :], eye).astype(BF16), ones).astype(jnp.int32)
        return carry

    lax.fori_loop(0, tb // _PRE_TOKENS, pre, 0)

    def body(t, carry):
        x = hn_ref[t]
        base = t * N_SEL
        folded = []
        for g in range(N_SEL // SUBLANES):
            prods = []
            for r in range(SUBLANES):
                k = g * SUBLANES + perm[r]
                prods.append(_expert_row(tab_ref, j8_ref[t, k], sd_ref[pl.ds(base + k, 1), :]) * x)
            folded.append(_fold_sublanes(prods, sub))
        q_ref[pl.ds(pl.multiple_of(base, N_SEL), N_SEL), :] = jnp.concatenate(folded, axis=0)
        return carry

    lax.fori_loop(0, tb, body, 0)

    def post(c, carry):
        toks = pl.ds(pl.multiple_of(c * _PRE_TOKENS, _PRE_TOKENS), _PRE_TOKENS)
        src = pl.ds(pl.multiple_of(c * pre_rows, pre_rows), pre_rows)
        q_hi, q_lo = _split_bf16(q_ref[src, :])
        rep = _dot(jnp.concatenate([q_hi, q_lo], axis=1), ones2)
        act = jnp.sum(rep.reshape(_PRE_TOKENS, N_SEL, LANES) * eye[None, :, :], axis=1)
        gelu = 0.5 * act * (1.0 + jnp.tanh(0.7978845608028654 * (act + 0.044715 * act * act * act)))
        w_ref[toks, :] = gate_ref[toks, :] * gelu
        return carry

    lax.fori_loop(0, tb // _PRE_TOKENS, post, 0)


def _peer_u(j8, sh, hn3, gate, tab, tb):
    t = j8.shape[0]
    vec = lambda: pl.BlockSpec((tb, N_SEL), lambda i: (i, 0))
    return pl.pallas_call(
        functools.partial(_peer_u_kernel, tb=tb),
        out_shape=jax.ShapeDtypeStruct((t, N_SEL), F32),
        grid=(t // tb,),
        in_specs=[pl.BlockSpec((tb, N_SEL), lambda i: (i, 0), memory_space=pltpu.SMEM), vec(),
                  pl.BlockSpec((tb, SUBLANES, LANES), lambda i: (i, 0, 0)), vec(),
                  pl.BlockSpec(memory_space=pltpu.VMEM)],
        out_specs=vec(),
        scratch_shapes=[pltpu.VMEM((tb * N_SEL, LANES), jnp.int32), pltpu.VMEM((tb * N_SEL, LANES), F32)],
        compiler_params=_cparams(("parallel",), 56),
        name="peer_u",
    )(j8, sh, hn3, gate, tab)


def _peer_v_kernel(j8_ref, sh_ref, wgt_ref, x2_ref, nw_ref, tab_ref, y_ref, sd_ref, wd_ref, *, tb, n_acc):
    eye = _eye()
    blk = (_iota2(2 * N_SEL, 2 * LANES, 0) // N_SEL == _iota2(2 * N_SEL, 2 * LANES, 1) // LANES).astype(BF16)
    pre_rows = _PRE_TOKENS * N_SEL

    def pre(c, carry):
        toks = pl.ds(pl.multiple_of(c * _PRE_TOKENS, _PRE_TOKENS), _PRE_TOKENS)
        dst = pl.ds(pl.multiple_of(c * pre_rows, pre_rows), pre_rows)
        diag = jnp.concatenate([_diag_rows(sh_ref[toks, :], eye).astype(BF16),
                                _diag_rows(wgt_ref[toks, :], eye).astype(BF16)], axis=1)
        rep = _dot(diag, blk)
        sd_ref[dst, :] = rep[:, :LANES].astype(jnp.int32)
        wd_ref[dst, :] = rep[:, LANES:]
        return carry

    lax.fori_loop(0, tb // _PRE_TOKENS, pre, 0)

    def body(t, carry):
        base = t * N_SEL
        accs = [jnp.zeros((SUBLANES, LANES), F32) for _ in range(n_acc)]
        for k in range(N_SEL):
            row = _expert_row(tab_ref, j8_ref[t, k], sd_ref[pl.ds(base + k, 1), :])
            accs[k % n_acc] = accs[k % n_acc] + row * wd_ref[pl.ds(base + k, 1), :]
        tot = accs[0]
        for a in accs[1:]:
            tot = tot + a
        y_ref[t] = x2_ref[t] + tot
        return carry

    lax.fori_loop(0, tb, body, 0)
    x3 = y_ref[...]
    ss = jnp.sum(jnp.sum(x3 * x3, axis=2, keepdims=True), axis=1, keepdims=True)
    y_ref[...] = x3 * lax.rsqrt(ss * (1.0 / (SUBLANES * LANES)) + NORM_EPS) * nw_ref[...][None]


def _peer_v(j8, sh, wgt, x23, nw3, tab, tb):
    t = j8.shape[0]
    vec = lambda: pl.BlockSpec((tb, N_SEL), lambda i: (i, 0))
    return pl.pallas_call(
        functools.partial(_peer_v_kernel, tb=tb, n_acc=4),
        out_shape=jax.ShapeDtypeStruct((t, SUBLANES, LANES), F32),
        grid=(t // tb,),
        in_specs=[pl.BlockSpec((tb, N_SEL), lambda i: (i, 0), memory_space=pltpu.SMEM), vec(), vec(),
                  pl.BlockSpec((tb, SUBLANES, LANES), lambda i: (i, 0, 0)),
                  pl.BlockSpec((SUBLANES, LANES), lambda i: (0, 0)),
                  pl.BlockSpec(memory_space=pltpu.VMEM)],
        out_specs=pl.BlockSpec((tb, SUBLANES, LANES), lambda i: (i, 0, 0)),
        scratch_shapes=[pltpu.VMEM((tb * N_SEL, LANES), jnp.int32), pltpu.VMEM((tb * N_SEL, LANES), F32)],
        compiler_params=_cparams(("parallel",), 56),
        name="peer_v",
    )(j8, sh, wgt, x23, nw3, tab)


def _pack_table(tab):
    n, d = tab.shape
    bits = lax.bitcast_convert_type(tab.astype(BF16), jnp.uint16).astype(jnp.uint32)
    packed = (bits[n // 2:] << 16) | bits[:n // 2]
    return lax.bitcast_convert_type(packed, jnp.int32).reshape(n // 2 * (d // LANES), LANES)


def _prep_weights(norm_mix_w, w_in, gla_w_gk2, gla_b_gk, gla_norm_w, gdn_conv_w, gdn_a_log,
                  gdn_dt_bias, gdn_norm_w, w_out, norm_ffn_w, peer_wq, peer_k1, peer_k2, peer_u, peer_v):
    d = w_in.shape[0]
    o_glr = 2 * GLA_QK + GLA_V
    o_gg = o_glr + GLA_LR
    o_qkv = o_gg + GLA_V
    o_da = o_qkv + GDN_QKV
    o_dz = o_da + 2 * GDN_HEADS
    small = jnp.concatenate([w_in[:, o_glr:o_gg], w_in[:, o_da:o_dz],
                             jnp.zeros((d, LANES - GLA_LR - 2 * GDN_HEADS), w_in.dtype)], axis=1)
    wcat = jnp.concatenate([w_in[:, :o_glr], w_in[:, o_gg:o_qkv], w_in[:, o_qkv:o_da],
                            w_in[:, o_dz:], small], axis=1).astype(BF16)
    w2p = jnp.zeros((LANES, GLA_QK), F32).at[:GLA_LR].set(gla_w_gk2).astype(BF16)
    avec = jnp.zeros((1, LANES), F32).at[0, SM_A0:SM_B0].set(-jnp.exp(gdn_a_log))
    dtb = jnp.zeros((1, LANES), F32).at[0, SM_A0:SM_B0].set(gdn_dt_bias)
    cw = jnp.zeros((SUBLANES, GDN_QKV), F32).at[:CONV_W].set(gdn_conv_w)
    return dict(
        nmix=norm_mix_w.reshape(1, d), wcat=wcat, w2p=w2p, b2=gla_b_gk.reshape(1, GLA_QK), avec=avec, dtb=dtb,
        gla_nw=gla_norm_w.reshape(1, GLA_DV), cw=cw, gdn_nw=gdn_norm_w.reshape(1, GDN_DV),
        wo=w_out.astype(BF16), nffn=norm_ffn_w.reshape(1, d), wq=peer_wq.astype(BF16),
        k1=peer_k1.astype(BF16), k2=peer_k2.astype(BF16), tab_u=_pack_table(peer_u), tab_v=_pack_table(peer_v))


def _layer(x, s_gla, s_gdn, conv_buf, w, nfw3, chunk, tm, tb):
    n_seq, seq_len, d = x.shape
    t = n_seq * seq_len
    xf = x.reshape(t, d)
    gq, gk, gv, gg, dqkv, dz, la, sm = _inproj(xf, w["nmix"], w["wcat"], w["w2p"], w["b2"], w["avec"],
                                               w["dtb"], tm)
    o1, gla_t = _gla(gq, gk, gv, la, gg, jnp.swapaxes(s_gla, -1, -2), w["gla_nw"], n_seq, seq_len, chunk)
    tail0 = jnp.pad(conv_buf, ((0, 0), (SUBLANES - (CONV_W - 1), 0), (0, 0)))
    o2, gdn_new, tail = _gdn(dqkv, sm, dz, tail0, w["cw"], s_gdn, w["gdn_nw"], n_seq, seq_len, chunk)
    x2, hn, j8, sh, gate = _post(o1, o2, xf, w["wo"], w["nffn"], w["wq"], w["k1"], w["k2"], tm)
    wgt = _peer_u(j8, sh, hn.reshape(t, SUBLANES, LANES), gate, w["tab_u"], tb)
    y = _peer_v(j8, sh, wgt, x2.reshape(t, SUBLANES, LANES), nfw3, w["tab_v"], tb)
    return (y.reshape(n_seq, seq_len, d), jnp.swapaxes(gla_t, -1, -2), gdn_new,
            tail[:, SUBLANES - (CONV_W - 1):, :])


def kernel(x_prompt, x_sample, state_gla, state_gdn, state_gdn_conv, norm_mix_w, w_in, gla_w_gk2, gla_b_gk,
           gla_norm_w, gdn_conv_w, gdn_a_log, gdn_dt_bias, gdn_norm_w, w_out, norm_ffn_w, peer_wq, peer_k1,
           peer_k2, peer_u, peer_v, norm_final_w):
    depth = w_in.shape[0]
    assert depth == 1, "the final norm is fused into the last layer's PEER pass"
    n_p, l_p, d = x_prompt.shape
    n_s, l_s, _ = x_sample.shape
    nfw3 = norm_final_w.reshape(SUBLANES, LANES)
    w = _prep_weights(norm_mix_w[0], w_in[0], gla_w_gk2[0], gla_b_gk[0], gla_norm_w[0], gdn_conv_w[0],
                      gdn_a_log[0], gdn_dt_bias[0], gdn_norm_w[0], w_out[0], norm_ffn_w[0], peer_wq[0],
                      peer_k1[0], peer_k2[0], peer_u[0], peer_v[0])
    z_gla = jnp.zeros((n_p, GLA_HEADS, GLA_DK, GLA_DV), F32)
    z_gdn = jnp.zeros((n_p, GDN_HEADS, GDN_DK, GDN_DV), F32)
    z_conv = jnp.zeros((n_p, CONV_W - 1, GDN_QKV), F32)
    chunk_p = min(CHUNK, l_p)
    chunk_s = min(CHUNK, l_s)
    tm_p = min(256, n_p * l_p)
    tm_s = min(256, n_s * l_s)
    yp, gla_p, gdn_p, conv_p = _layer(x_prompt, z_gla, z_gdn, z_conv, w, nfw3, chunk_p, tm_p, min(64, tm_p))
    ys, gla_s, gdn_s, conv_s = _layer(x_sample, state_gla[0], state_gdn[0], state_gdn_conv[0], w, nfw3,
                                      chunk_s, tm_s, min(64, tm_s))
    return (yp, ys, gla_p[None], gdn_p[None], conv_p[None], gla_s[None], gdn_s[None], conv_s[None])
```

```python
import functools

import jax
import jax.numpy as jnp
from jax import lax
from jax.experimental import pallas as pl
from jax.experimental.pallas import tpu as pltpu

F32 = jnp.float32
BF16 = jnp.bfloat16
HI = lax.Precision.HIGHEST

NORM_EPS = 1e-6
CHUNK = 64
GLA_HEADS, GLA_DK, GLA_DV, GLA_LR, GLA_GATE_TAU = 4, 64, 128, 16, 16.0
GDN_HEADS, GDN_DK, GDN_DV, CONV_W = 4, 128, 128, 4
GLA_QK, GLA_V = GLA_HEADS * GLA_DK, GLA_HEADS * GLA_DV
GDN_QK, GDN_V = GDN_HEADS * GDN_DK, GDN_HEADS * GDN_DV
GDN_QKV = 2 * GDN_QK + GDN_V
PEER_HEADS, PEER_QDIM, N_KEYS, PEER_TOPK = 8, 256, 128, 16
PEER_HALF = PEER_QDIM // 2
N_SEL = PEER_HEADS * PEER_TOPK
HALF_EXPERTS = N_KEYS * N_KEYS // 2

LANES = 128
SUBLANES = 8
MIB = 1024 * 1024

SM_A0 = GLA_LR
SM_B0 = GLA_LR + GDN_HEADS


def _cparams(sem, vmem_mib):
    return pltpu.CompilerParams(dimension_semantics=sem, vmem_limit_bytes=vmem_mib * MIB)


def _softplus(x):
    return jnp.maximum(x, 0.0) + jnp.log(1.0 + jnp.exp(-jnp.abs(x)))


def _sigmoid(x):
    return 1.0 / (1.0 + jnp.exp(-x))


def _dot(a, b, prec=None):
    return jnp.dot(a, b, preferred_element_type=F32, precision=prec)


def _dot_nt(a, b, prec=None):
    return lax.dot_general(a, b, (((1,), (1,)), ((), ())), preferred_element_type=F32, precision=prec)


def _dot_tn(a, b, prec=None):
    return lax.dot_general(a, b, (((0,), (0,)), ((), ())), preferred_element_type=F32, precision=prec)


def _split_bf16(x):
    hi = x.astype(BF16)
    return hi, (x - hi.astype(F32)).astype(BF16)


def _split3_bf16(x):
    hi = x.astype(BF16)
    r = x - hi.astype(F32)
    mid = r.astype(BF16)
    return hi, mid, (r - mid.astype(F32)).astype(BF16)


def _dot3(a, b):
    ah, al = _split_bf16(a)
    bh, bl = _split_bf16(b)
    return _dot(jnp.concatenate([ah, ah, al], axis=1), jnp.concatenate([bh, bl, bh], axis=0))


def _dot3_nt(a, b):
    ah, al = _split_bf16(a)
    bh, bl = _split_bf16(b)
    return _dot_nt(jnp.concatenate([ah, ah, al], axis=1), jnp.concatenate([bh, bl, bh], axis=1))


def _dot3_tn(a, b):
    ah, al = _split_bf16(a)
    bh, bl = _split_bf16(b)
    return _dot_tn(jnp.concatenate([ah, ah, al], axis=0), jnp.concatenate([bh, bl, bh], axis=0))


def _dot_exact_lhs(a01, b):
    a16 = a01.astype(BF16)
    return _dot(jnp.concatenate([a16, a16, a16], axis=1), jnp.concatenate(_split3_bf16(b), axis=0))


def _dot_nt_exact_rhs(a, b01):
    b16 = b01.astype(BF16)
    return _dot_nt(jnp.concatenate(_split3_bf16(a), axis=1), jnp.concatenate([b16, b16, b16], axis=1))


_W_COLS = (("gq", GLA_QK), ("gk", GLA_QK), ("gv", GLA_V), ("gg", GLA_V), ("dqkv", GDN_QKV),
           ("dz", GDN_V), ("sm", LANES))


def _inproj_kernel(x_ref, nw_ref, w_ref, w2_ref, b2_ref, avec_ref, dtb_ref,
                   gq_ref, gk_ref, gv_ref, gg_ref, dqkv_ref, dz_ref, la_ref, sm_ref):
    x = x_ref[...]
    ms = jnp.mean(x * x, axis=-1, keepdims=True)
    h = (x * lax.rsqrt(ms + NORM_EPS) * nw_ref[...]).astype(BF16)
    outs = dict(gq=gq_ref, gk=gk_ref, gv=gv_ref, gg=gg_ref, dqkv=dqkv_ref, dz=dz_ref)
    off = 0
    ps = None
    for name, width in _W_COLS:
        p = _dot(h, w_ref[:, off:off + width])
        off += width
        if name == "sm":
            ps = p
        elif name == "gq":
            gq_ref[...] = p * (GLA_DK ** -0.5)
        else:
            outs[name][...] = p
    z = _dot(ps.astype(BF16), w2_ref[...]) + b2_ref[...]
    la_ref[...] = -_softplus(-z) * (1.0 / GLA_GATE_TAU)
    lane = lax.broadcasted_iota(jnp.int32, ps.shape, 1)
    log_a = avec_ref[...] * _softplus(ps + dtb_ref[...])
    beta = _sigmoid(ps)
    is_a = (lane >= SM_A0) & (lane < SM_B0)
    is_b = (lane >= SM_B0) & (lane < SM_B0 + GDN_HEADS)
    sm_ref[...] = jnp.where(is_a, log_a, jnp.where(is_b, beta, ps))


def _inproj(x, nw, wcat, w2p, b2, avec, dtb, tm):
    t, d = x.shape
    n_w = wcat.shape[1]
    widths = dict(_W_COLS)
    names = ("gq", "gk", "gv", "gg", "dqkv", "dz")
    out_shape = [jax.ShapeDtypeStruct((t, widths[n]), F32) for n in names]
    out_shape += [jax.ShapeDtypeStruct((t, GLA_QK), F32), jax.ShapeDtypeStruct((t, LANES), F32)]
    row = lambda w: pl.BlockSpec((tm, w), lambda i: (i, 0))
    full = lambda a: pl.BlockSpec(a.shape, lambda i: (0,) * a.ndim)
    return pl.pallas_call(
        _inproj_kernel,
        out_shape=out_shape,
        grid=(t // tm,),
        in_specs=[row(d), full(nw), full(wcat), full(w2p), full(b2), full(avec), full(dtb)],
        out_specs=[row(widths[n]) for n in names] + [row(GLA_QK), row(LANES)],
        compiler_params=_cparams(("parallel",), 48),
        name="inproj",
    )(x, nw, wcat, w2p, b2, avec, dtb)


def _iota2(n, m, axis):
    return lax.broadcasted_iota(jnp.int32, (n, m), axis)


def _gla_kernel(q_ref, k_ref, v_ref, la_ref, gg_ref, s0_ref, nw_ref, o_ref, sfin_ref, st_ref, *, chunk):
    c = pl.program_id(1)
    nc = pl.num_programs(1)

    @pl.when(c == 0)
    def _():
        st_ref[...] = s0_ref[0]

    row = _iota2(chunk, chunk, 0)
    col = _iota2(chunk, chunk, 1)
    q = q_ref[...]
    k = k_ref[...]

    blocks = []
    b = chunk // 2
    while b >= 1:
        blocks.append(b)
        b //= 2
    sel = [row >= col]
    for b in blocks:
        sel.append(col <= (row // b) * b)
        sel.append(col <= jnp.minimum((row // b + 1) * b, chunk - 1))
    g_all = _dot_exact_lhs(jnp.concatenate([m.astype(F32) for m in sel], axis=0), la_ref[...])
    g = g_all[0:chunk]
    levels = []
    for li, b in enumerate(blocks):
        ref_r = g_all[(1 + 2 * li) * chunk:(2 + 2 * li) * chunk]
        ref_c = g_all[(2 + 2 * li) * chunk:(3 + 2 * li) * chunk]
        mask = (row // (2 * b) == col // (2 * b)) & (row // b == col // b + 1)
        levels.append((q * jnp.exp(g - ref_r), k * jnp.exp(ref_c - g), mask))
    qk_diag = q * k
    g_last = g[chunk - 1:chunk, :]
    q_in = q * jnp.exp(g)
    k_out = k * jnp.exp(g_last - g)
    decay_out = jnp.exp(g_last)

    heads = range(GLA_HEADS)
    dks = [slice(h * GLA_DK, (h + 1) * GLA_DK) for h in heads]
    dvs = [slice(h * GLA_DV, (h + 1) * GLA_DV) for h in heads]
    atts = [jnp.zeros((chunk, chunk), F32) for _ in heads]
    for ql, kl, mask in levels:
        atts = [att + jnp.where(mask, _dot3_nt(ql[:, dk], kl[:, dk]), 0.0) for att, dk in zip(atts, dks)]
    vhs = [v_ref[:, dv] for dv in dvs]
    sts = [st_ref[h] for h in heads]
    os_ = []
    for h in heads:
        dsum = jnp.sum(qk_diag[:, dks[h]], axis=-1, keepdims=True)
        os_.append(_dot3_nt(q_in[:, dks[h]], sts[h]) + _dot3(atts[h], vhs[h]) + dsum * vhs[h])
    for h in heads:
        st_ref[h] = sts[h] * decay_out[:, dks[h]] + _dot3_tn(vhs[h], k_out[:, dks[h]])
    for h in heads:
        o = os_[h]
        ms = jnp.mean(o * o, axis=-1, keepdims=True)
        on = o * lax.rsqrt(ms + NORM_EPS) * nw_ref[...]
        gate = gg_ref[:, dvs[h]]
        o_ref[:, dvs[h]] = on * (gate * _sigmoid(gate))

    @pl.when(c == nc - 1)
    def _():
        sfin_ref[0] = st_ref[...]


def _gla(gq, gk, gv, la, gg, s0t, nw, n_seq, seq_len, chunk):
    nc = seq_len // chunk
    t = n_seq * seq_len
    blk = lambda w: pl.BlockSpec((chunk, w), lambda b, c: (b * nc + c, 0))
    st_spec = pl.BlockSpec((1, GLA_HEADS, GLA_DV, GLA_DK), lambda b, c: (b, 0, 0, 0))
    return pl.pallas_call(
        functools.partial(_gla_kernel, chunk=chunk),
        out_shape=[jax.ShapeDtypeStruct((t, GLA_V), F32),
                   jax.ShapeDtypeStruct((n_seq, GLA_HEADS, GLA_DV, GLA_DK), F32)],
        grid=(n_seq, nc),
        in_specs=[blk(GLA_QK), blk(GLA_QK), blk(GLA_V), blk(GLA_QK), blk(GLA_V), st_spec,
                  pl.BlockSpec((1, GLA_DV), lambda b, c: (0, 0))],
        out_specs=[blk(GLA_V), st_spec],
        scratch_shapes=[pltpu.VMEM((GLA_HEADS, GLA_DV, GLA_DK), F32)],
        compiler_params=_cparams(("parallel", "arbitrary"), 32),
        name="gla_scan",
    )(gq, gk, gv, la, gg, s0t, nw)


_INV_BASE = 16


def _unit_lower_inverses(lows, row, col, n):
    eye = (row == col).astype(F32)
    base = min(_INV_BASE, n)
    diag_blk = row // base == col // base
    pws = [jnp.where(diag_blk, low, 0.0) for low in lows]
    invs = [eye - ld for ld in pws]
    p = 2
    while p < base:
        pws = [_dot3(pw, pw) for pw in pws]
        invs = [inv + _dot3(inv, pw) for inv, pw in zip(invs, pws)]
        p *= 2
    b = base
    while b < n:
        sib = (row // (2 * b) == col // (2 * b)) & (row // b != col // b)
        tmps = [_dot3(jnp.where(sib, low, 0.0), inv) for low, inv in zip(lows, invs)]
        invs = [inv - _dot3(inv, t) for inv, t in zip(invs, tmps)]
        b *= 2
    return invs


def _gdn_kernel(x_ref, ab_ref, dz_ref, tail0_ref, cw_ref, s0_ref, nw_ref,
                o_ref, sfin_ref, tail_out_ref, st_ref, tail_ref, *, chunk):
    c = pl.program_id(1)
    nc = pl.num_programs(1)

    @pl.when(c == 0)
    def _():
        st_ref[...] = s0_ref[0]
        tail_ref[...] = tail0_ref[0]

    x = x_ref[...]
    ext = jnp.concatenate([tail_ref[...], x], axis=0)
    base = SUBLANES - (CONV_W - 1)
    conv = ext[base:base + chunk] * cw_ref[0:1, :]
    for i in range(1, CONV_W):
        conv = conv + ext[base + i:base + i + chunk] * cw_ref[i:i + 1, :]
    tail_ref[...] = x[chunk - SUBLANES:chunk]
    conv = conv * _sigmoid(conv)

    row = _iota2(chunk, chunk, 0)
    col = _iota2(chunk, chunk, 1)
    lower = row >= col
    tril = lower.astype(F32)
    ab = ab_ref[...]
    g_col = _dot_exact_lhs(tril, ab)
    g_row = _dot_nt_exact_rhs(ab.T, tril)

    heads = range(GDN_HEADS)
    qs, ks, vs, gcs, betas, decs = [], [], [], [], [], []
    for h in heads:
        cq = conv[:, h * GDN_DK:(h + 1) * GDN_DK]
        ck = conv[:, GDN_QK + h * GDN_DK:GDN_QK + (h + 1) * GDN_DK]
        vs.append(conv[:, 2 * GDN_QK + h * GDN_DV:2 * GDN_QK + (h + 1) * GDN_DV])
        qs.append(cq * lax.rsqrt(jnp.sum(cq * cq, axis=-1, keepdims=True) + NORM_EPS) * (GDN_DK ** -0.5))
        ks.append(ck * lax.rsqrt(jnp.sum(ck * ck, axis=-1, keepdims=True) + NORM_EPS))
        gc = g_col[:, SM_A0 + h:SM_A0 + h + 1]
        gr = g_row[SM_A0 + h:SM_A0 + h + 1, :]
        gcs.append(gc)
        betas.append(ab[:, SM_B0 + h:SM_B0 + h + 1])
        decs.append(jnp.where(lower, jnp.exp(jnp.where(lower, gc - gr, 0.0)), 0.0))
    kks = [_dot3_nt(k, k) for k in ks]
    qks = [_dot3_nt(q, k) * dec for q, k, dec in zip(qs, ks, decs)]
    lows = [jnp.where(row > col, beta * kk * dec, 0.0) for beta, kk, dec in zip(betas, kks, decs)]
    tinvs = _unit_lower_inverses(lows, row, col, chunk)
    egs = [jnp.exp(gc) for gc in gcs]
    uws = [_dot3(tinv, jnp.concatenate([v * beta, k * (beta * eg)], axis=1))
           for tinv, v, k, beta, eg in zip(tinvs, vs, ks, betas, egs)]
    sts = [st_ref[h] for h in heads]
    v_news = [uw[:, :GDN_DV] - _dot3(uw[:, GDN_DV:], st) for uw, st in zip(uws, sts)]
    os_ = [_dot3(q * eg, st) + _dot3(qk, v_new) for q, eg, st, qk, v_new in zip(qs, egs, sts, qks, v_news)]
    for h in heads:
        g_last = gcs[h][chunk - 1:chunk, :]
        st_ref[h] = sts[h] * jnp.exp(g_last) + _dot3_tn(ks[h] * jnp.exp(g_last - gcs[h]), v_news[h])
    for h in heads:
        o = os_[h]
        ms = jnp.mean(o * o, axis=-1, keepdims=True)
        on = o * lax.rsqrt(ms + NORM_EPS) * nw_ref[...]
        gate = dz_ref[:, h * GDN_DV:(h + 1) * GDN_DV]
        o_ref[:, h * GDN_DV:(h + 1) * GDN_DV] = on * (gate * _sigmoid(gate))

    @pl.when(c == nc - 1)
    def _():
        sfin_ref[0] = st_ref[...]
        tail_out_ref[0] = tail_ref[...]


def _gdn(dqkv, ab, dz, tail0, cw, s0, nw, n_seq, seq_len, chunk):
    nc = seq_len // chunk
    t = n_seq * seq_len
    blk = lambda w: pl.BlockSpec((chunk, w), lambda b, c: (b * nc + c, 0))
    st_spec = pl.BlockSpec((1, GDN_HEADS, GDN_DK, GDN_DV), lambda b, c: (b, 0, 0, 0))
    tail_spec = pl.BlockSpec((1, SUBLANES, GDN_QKV), lambda b, c: (b, 0, 0))
    return pl.pallas_call(
        functools.partial(_gdn_kernel, chunk=chunk),
        out_shape=[jax.ShapeDtypeStruct((t, GDN_V), F32),
                   jax.ShapeDtypeStruct((n_seq, GDN_HEADS, GDN_DK, GDN_DV), F32),
                   jax.ShapeDtypeStruct((n_seq, SUBLANES, GDN_QKV), F32)],
        grid=(n_seq, nc),
        in_specs=[blk(GDN_QKV), blk(LANES), blk(GDN_V), tail_spec,
                  pl.BlockSpec((SUBLANES, GDN_QKV), lambda b, c: (0, 0)), st_spec,
                  pl.BlockSpec((1, GDN_DV), lambda b, c: (0, 0))],
        out_specs=[blk(GDN_V), st_spec, tail_spec],
        scratch_shapes=[pltpu.VMEM((GDN_HEADS, GDN_DK, GDN_DV), F32),
                        pltpu.VMEM((SUBLANES, GDN_QKV), F32)],
        compiler_params=_cparams(("parallel", "arbitrary"), 32),
        name="gdn_scan",
    )(dqkv, ab, dz, tail0, cw, s0, nw)


def _topk_cols(s, k, payload=None):
    n = s.shape[0]
    rid = lax.broadcasted_iota(jnp.int32, s.shape, 0).astype(F32)
    vals, idxs = [], []
    for _ in range(k):
        m = jnp.max(s, axis=0, keepdims=True)
        am = jnp.min(jnp.where(s == m, rid, float(n)), axis=0, keepdims=True)
        sel = rid == am
        vals.append(m)
        if payload is None:
            idxs.append(am)
        else:
            idxs.append(jnp.sum(jnp.where(sel, payload, 0.0), axis=0, keepdims=True))
        s = jnp.where(sel, -jnp.inf, s)
    return jnp.concatenate(vals, axis=0), jnp.concatenate(idxs, axis=0)


def _post_kernel(o1_ref, o2_ref, x_ref, wo_ref, nw_ref, wq_ref, k1_ref, k2_ref,
                 x2_ref, hn_ref, j8_ref, sh_ref, gate_ref):
    mix = (_dot(o1_ref[...].astype(BF16), wo_ref[0:GLA_V, :])
           + _dot(o2_ref[...].astype(BF16), wo_ref[GLA_V:GLA_V + GDN_V, :]))
    x2 = x_ref[...] + mix
    x2_ref[...] = x2
    ms = jnp.mean(x2 * x2, axis=-1, keepdims=True)
    hn = x2 * lax.rsqrt(ms + NORM_EPS) * nw_ref[...]
    hn_ref[...] = hn
    qry = _dot(hn.astype(BF16), wq_ref[...])
    idx_rows, gate_rows = [], []
    for h in range(PEER_HEADS):
        q1 = qry[:, h * PEER_QDIM:h * PEER_QDIM + PEER_HALF].astype(BF16)
        q2 = qry[:, h * PEER_QDIM + PEER_HALF:(h + 1) * PEER_QDIM].astype(BF16)
        s1 = _dot_nt(k1_ref[h], q1)
        s2 = _dot_nt(k2_ref[h], q2)
        v1, i1 = _topk_cols(s1, PEER_TOPK)
        v2, i2 = _topk_cols(s2, PEER_TOPK)
        cand_rows, cidx_rows = [], []
        for a in range(PEER_TOPK // 2):
            n_valid = PEER_TOPK // (a + 1)
            n_rows = -(-n_valid // SUBLANES) * SUBLANES
            c = v1[a:a + 1, :] + v2[0:n_rows, :]
            if n_valid < n_rows:
                c = jnp.where(lax.broadcasted_iota(jnp.int32, c.shape, 0) < n_valid, c, -jnp.inf)
            cand_rows.append(c)
            cidx_rows.append(i1[a:a + 1, :] * float(N_KEYS) + i2[0:n_rows, :])
        cand_rows.append(v1[PEER_TOPK // 2:, :] + v2[0:1, :])
        cidx_rows.append(i1[PEER_TOPK // 2:, :] * float(N_KEYS) + i2[0:1, :])
        sc, eidx = _topk_cols(jnp.concatenate(cand_rows, axis=0), PEER_TOPK,
                              payload=jnp.concatenate(cidx_rows, axis=0))
        e = jnp.exp(sc - jnp.max(sc, axis=0, keepdims=True))
        gate_rows.append(e / jnp.sum(e, axis=0, keepdims=True))
        idx_rows.append(eidx)
    e = jnp.concatenate(idx_rows, axis=0).T.astype(jnp.int32)
    j8_ref[...] = (e & (HALF_EXPERTS - 1)) * SUBLANES
    sh_ref[...] = jnp.where(e >= HALF_EXPERTS, 0.0, 16.0)
    gate_ref[...] = jnp.concatenate(gate_rows, axis=0).T


def _post(o1, o2, x, wo, nw, wq, k1, k2, tm):
    t, d = x.shape
    row = lambda w: pl.BlockSpec((tm, w), lambda i: (i, 0))
    full = lambda a: pl.BlockSpec(a.shape, lambda i: (0,) * a.ndim)
    return pl.pallas_call(
        _post_kernel,
        out_shape=[jax.ShapeDtypeStruct((t, d), F32), jax.ShapeDtypeStruct((t, d), F32),
                   jax.ShapeDtypeStruct((t, N_SEL), jnp.int32), jax.ShapeDtypeStruct((t, N_SEL), F32),
                   jax.ShapeDtypeStruct((t, N_SEL), F32)],
        grid=(t // tm,),
        in_specs=[row(GLA_V), row(GDN_V), row(d), full(wo), full(nw), full(wq), full(k1), full(k2)],
        out_specs=[row(d), row(d), row(N_SEL), row(N_SEL), row(N_SEL)],
        compiler_params=_cparams(("parallel",), 48),
        name="post_topk",
    )(o1, o2, x, wo, nw, wq, k1, k2)


_HI_MASK = -65536
_IDX_SLOTS = 8


def _expert_row(tab_ref, j8, shift_row):
    words = tab_ref[pl.ds(pl.multiple_of(j8, SUBLANES), SUBLANES), :]
    bits = lax.shift_left(words, jnp.broadcast_to(shift_row, (SUBLANES, LANES))) & _HI_MASK
    return lax.bitcast_convert_type(bits, F32)


def _diag_rows(vals, eye):
    n = vals.shape[0]
    return (vals[:, None, :] * eye[None, :, :]).reshape(n * N_SEL, N_SEL)


def _fold_sublanes(vs, sub):
    m4 = sub < 4
    cur = []
    for a, b in zip(vs[0::2], vs[1::2]):
        cur.append(jnp.where(m4, a, b) + pltpu.roll(jnp.where(m4, b, a), 4, 0))
    for half in (2, 1):
        m = (sub % (2 * half)) < half
        nxt = []
        for a, b in zip(cur[0::2], cur[1::2]):
            ta = a + pltpu.roll(a, SUBLANES - half, 0)
            tb = b + pltpu.roll(b, half, 0)
            nxt.append(jnp.where(m, ta, tb))
        cur = nxt
    return cur[0]


def _eye():
    return (_iota2(N_SEL, N_SEL, 0) == _iota2(N_SEL, N_SEL, 1)).astype(F32)


def _peer_u_kernel(j8_ref, sh_ref, hn_ref, gate_ref, tab_ref, w_ref, sd0_ref, sd1_ref, q0_ref, q1_ref, *, tb):
    sub = lax.broadcasted_iota(jnp.int32, (SUBLANES, LANES), 0)
    eye = _eye()
    ones = jnp.ones((N_SEL, LANES), BF16)
    ones2 = jnp.ones((2 * N_SEL, LANES), BF16)
    perm = (0, 4, 2, 6, 1, 5, 3, 7)

    sd_refs = (sd0_ref, sd1_ref)
    q_refs = (q0_ref, q1_ref)

    def shift_rows(src_tok, dst_ref):
        diag = _diag_rows(sh_ref[pl.ds(src_tok, 1), :], eye).astype(BF16)
        dst_ref[...] = _dot(diag, ones).astype(jnp.int32)

    def finish(src_ref, out_tok):
        q_hi, q_lo = _split_bf16(src_ref[...])
        rep = _dot(jnp.concatenate([q_hi, q_lo], axis=1), ones2)
        act = jnp.sum(rep * eye, axis=0, keepdims=True)
        gelu = 0.5 * act * (1.0 + jnp.tanh(0.7978845608028654 * (act + 0.044715 * act * act * act)))
        w_ref[pl.ds(out_tok, 1), :] = gate_ref[pl.ds(out_tok, 1), :] * gelu

    q1_ref[...] = jnp.zeros((N_SEL, LANES), F32)
    shift_rows(0, sd0_ref)

    def body(i, carry):
        for p in range(2):
            t = 2 * i + p
            shift_rows(jnp.minimum(t + 1, tb - 1), sd_refs[1 - p])
            finish(q_refs[1 - p], jnp.maximum(t - 1, 0))
            x = hn_ref[t]
            folded = []
            for g in range(N_SEL // SUBLANES):
                prods = []
                for r in range(SUBLANES):
                    k = g * SUBLANES + perm[r]
                    prods.append(_expert_row(tab_ref, j8_ref[t, k], sd_refs[p][k:k + 1, :]) * x)
                folded.append(_fold_sublanes(prods, sub))
            q_refs[p][...] = jnp.concatenate(folded, axis=0)
        return carry

    lax.fori_loop(0, tb // 2, body, 0)
    finish(q_refs[(tb - 1) % 2], tb - 1)


def _peer_u(j8, sh, hn3, gate, tab, tb):
    t = j8.shape[0]
    vec = lambda: pl.BlockSpec((tb, N_SEL), lambda i: (i, 0))
    return pl.pallas_call(
        functools.partial(_peer_u_kernel, tb=tb),
        out_shape=jax.ShapeDtypeStruct((t, N_SEL), F32),
        grid=(t // tb,),
        in_specs=[pl.BlockSpec((tb, N_SEL), lambda i: (i, 0), memory_space=pltpu.SMEM), vec(),
                  pl.BlockSpec((tb, SUBLANES, LANES), lambda i: (i, 0, 0)), vec(),
                  pl.BlockSpec(memory_space=pltpu.VMEM)],
        out_specs=vec(),
        scratch_shapes=[pltpu.VMEM((N_SEL, LANES), jnp.int32), pltpu.VMEM((N_SEL, LANES), jnp.int32),
                        pltpu.VMEM((N_SEL, LANES), F32), pltpu.VMEM((N_SEL, LANES), F32)],
        compiler_params=_cparams(("arbitrary",), 56),
        name="peer_u",
    )(j8, sh, hn3, gate, tab)


def _peer_v_kernel(j8_ref, sh_ref, wgt_ref, x2_ref, nw_ref, tab_ref, y_ref, sd_ref, wd_ref, idx_ref, sem,
                   *, tb, n_acc):
    eye = _eye()
    blk = (_iota2(2 * N_SEL, 2 * LANES, 0) // N_SEL == _iota2(2 * N_SEL, 2 * LANES, 1) // LANES).astype(BF16)
    half_toks = _IDX_SLOTS // 2
    half_rows = half_toks * N_SEL

    def pre(c, carry):
        for h in range(2):
            tok0 = pl.multiple_of(c * _IDX_SLOTS, _IDX_SLOTS) + h * half_toks
            toks = pl.ds(tok0, half_toks)
            diag = jnp.concatenate([_diag_rows(sh_ref[toks, :], eye).astype(BF16),
                                    _diag_rows(wgt_ref[toks, :], eye).astype(BF16)], axis=1)
            rep = _dot(diag, blk)
            dst = pl.ds(pl.multiple_of(tok0 * N_SEL, half_rows), half_rows)
            sd_ref[dst, :] = rep[:, :LANES].astype(jnp.int32)
            wd_ref[dst, :] = rep[:, LANES:]
        return carry

    lax.fori_loop(0, tb // _IDX_SLOTS, pre, 0)

    def idx_copy(t, slot):
        return pltpu.make_async_copy(j8_ref.at[t], idx_ref.at[slot], sem.at[slot])

    for s in range(_IDX_SLOTS):
        idx_copy(s, s).start()

    def group(gi, carry):
        for s in range(_IDX_SLOTS):
            t = gi * _IDX_SLOTS + s
            idx_copy(t, s).wait()
            base = t * N_SEL
            accs = [jnp.zeros((SUBLANES, LANES), F32) for _ in range(n_acc)]
            for k in range(N_SEL):
                row = _expert_row(tab_ref, idx_ref[s, 0, k], sd_ref[pl.ds(base + k, 1), :])
                accs[k % n_acc] = accs[k % n_acc] + row * wd_ref[pl.ds(base + k, 1), :]
            tot = accs[0]
            for a in accs[1:]:
                tot = tot + a
            y_ref[t] = x2_ref[t] + tot

            @pl.when(t + _IDX_SLOTS < tb)
            def _():
                idx_copy(t + _IDX_SLOTS, s).start()
        return carry

    lax.fori_loop(0, tb // _IDX_SLOTS, group, 0)
    x3 = y_ref[...]
    ss = jnp.sum(jnp.sum(x3 * x3, axis=2, keepdims=True), axis=1, keepdims=True)
    y_ref[...] = x3 * lax.rsqrt(ss * (1.0 / (SUBLANES * LANES)) + NORM_EPS) * nw_ref[...][None]


def _peer_v(j8, sh, wgt, x23, nw3, tab, tb):
    t = j8.shape[0]
    vec = lambda: pl.BlockSpec((tb, N_SEL), lambda i: (i, 0))
    return pl.pallas_call(
        functools.partial(_peer_v_kernel, tb=tb, n_acc=4),
        out_shape=jax.ShapeDtypeStruct((t, SUBLANES, LANES), F32),
        grid=(t // tb,),
        in_specs=[pl.BlockSpec((tb, 1, N_SEL), lambda i: (i, 0, 0)), vec(), vec(),
                  pl.BlockSpec((tb, SUBLANES, LANES), lambda i: (i, 0, 0)),
                  pl.BlockSpec((SUBLANES, LANES), lambda i: (0, 0)),
                  pl.BlockSpec(memory_space=pltpu.VMEM)],
        out_specs=pl.BlockSpec((tb, SUBLANES, LANES), lambda i: (i, 0, 0)),
        scratch_shapes=[pltpu.VMEM((tb * N_SEL, LANES), jnp.int32), pltpu.VMEM((tb * N_SEL, LANES), F32),
                        pltpu.SMEM((_IDX_SLOTS, 1, N_SEL), jnp.int32),
                        pltpu.SemaphoreType.DMA((_IDX_SLOTS,))],
        compiler_params=_cparams(("arbitrary",), 56),
        name="peer_v",
    )(j8.reshape(t, 1, N_SEL), sh, wgt, x23, nw3, tab)


def _pack_table(tab):
    n, d = tab.shape
    bits = lax.bitcast_convert_type(tab.astype(BF16), jnp.uint16).astype(jnp.uint32)
    packed = (bits[n // 2:] << 16) | bits[:n // 2]
    return lax.bitcast_convert_type(packed, jnp.int32).reshape(n // 2 * (d // LANES), LANES)


def _prep_weights(norm_mix_w, w_in, gla_w_gk2, gla_b_gk, gla_norm_w, gdn_conv_w, gdn_a_log,
                  gdn_dt_bias, gdn_norm_w, w_out, norm_ffn_w, peer_wq, peer_k1, peer_k2, peer_u, peer_v):
    d = w_in.shape[0]
    o_glr = 2 * GLA_QK + GLA_V
    o_gg = o_glr + GLA_LR
    o_qkv = o_gg + GLA_V
    o_da = o_qkv + GDN_QKV
    o_dz = o_da + 2 * GDN_HEADS
    small = jnp.concatenate([w_in[:, o_glr:o_gg], w_in[:, o_da:o_dz],
                             jnp.zeros((d, LANES - GLA_LR - 2 * GDN_HEADS), w_in.dtype)], axis=1)
    wcat = jnp.concatenate([w_in[:, :o_glr], w_in[:, o_gg:o_qkv], w_in[:, o_qkv:o_da],
                            w_in[:, o_dz:], small], axis=1).astype(BF16)
    w2p = jnp.zeros((LANES, GLA_QK), F32).at[:GLA_LR].set(gla_w_gk2).astype(BF16)
    avec = jnp.zeros((1, LANES), F32).at[0, SM_A0:SM_B0].set(-jnp.exp(gdn_a_log))
    dtb = jnp.zeros((1, LANES), F32).at[0, SM_A0:SM_B0].set(gdn_dt_bias)
    cw = jnp.zeros((SUBLANES, GDN_QKV), F32).at[:CONV_W].set(gdn_conv_w)
    return dict(
        nmix=norm_mix_w.reshape(1, d), wcat=wcat, w2p=w2p, b2=gla_b_gk.reshape(1, GLA_QK), avec=avec, dtb=dtb,
        gla_nw=gla_norm_w.reshape(1, GLA_DV), cw=cw, gdn_nw=gdn_norm_w.reshape(1, GDN_DV),
        wo=w_out.astype(BF16), nffn=norm_ffn_w.reshape(1, d), wq=peer_wq.astype(BF16),
        k1=peer_k1.astype(BF16), k2=peer_k2.astype(BF16), tab_u=_pack_table(peer_u), tab_v=_pack_table(peer_v))


def _layer(x, s_gla, s_gdn, conv_buf, w, nfw3, chunk, tm, tb):
    n_seq, seq_len, d = x.shape
    t = n_seq * seq_len
    xf = x.reshape(t, d)
    gq, gk, gv, gg, dqkv, dz, la, sm = _inproj(xf, w["nmix"], w["wcat"], w["w2p"], w["b2"], w["avec"],
                                               w["dtb"], tm)
    o1, gla_t = _gla(gq, gk, gv, la, gg, jnp.swapaxes(s_gla, -1, -2), w["gla_nw"], n_seq, seq_len, chunk)
    tail0 = jnp.pad(conv_buf, ((0, 0), (SUBLANES - (CONV_W - 1), 0), (0, 0)))
    o2, gdn_new, tail = _gdn(dqkv, sm, dz, tail0, w["cw"], s_gdn, w["gdn_nw"], n_seq, seq_len, chunk)
    x2, hn, j8, sh, gate = _post(o1, o2, xf, w["wo"], w["nffn"], w["wq"], w["k1"], w["k2"], tm)
    wgt = _peer_u(j8, sh, hn.reshape(t, SUBLANES, LANES), gate, w["tab_u"], tb)
    y = _peer_v(j8, sh, wgt, x2.reshape(t, SUBLANES, LANES), nfw3, w["tab_v"], tb)
    return (y.reshape(n_seq, seq_len, d), jnp.swapaxes(gla_t, -1, -2), gdn_new,
            tail[:, SUBLANES - (CONV_W - 1):, :])


def kernel(x_prompt, x_sample, state_gla, state_gdn, state_gdn_conv, norm_mix_w, w_in, gla_w_gk2, gla_b_gk,
           gla_norm_w, gdn_conv_w, gdn_a_log, gdn_dt_bias, gdn_norm_w, w_out, norm_ffn_w, peer_wq, peer_k1,
           peer_k2, peer_u, peer_v, norm_final_w):
    depth = w_in.shape[0]
    assert depth == 1, "the final norm is fused into the last layer's PEER pass"
    n_p, l_p, d = x_prompt.shape
    n_s, l_s, _ = x_sample.shape
    nfw3 = norm_final_w.reshape(SUBLANES, LANES)
    w = _prep_weights(norm_mix_w[0], w_in[0], gla_w_gk2[0], gla_b_gk[0], gla_norm_w[0], gdn_conv_w[0],
                      gdn_a_log[0], gdn_dt_bias[0], gdn_norm_w[0], w_out[0], norm_ffn_w[0], peer_wq[0],
                      peer_k1[0], peer_k2[0], peer_u[0], peer_v[0])
    z_gla = jnp.zeros((n_p, GLA_HEADS, GLA_DK, GLA_DV), F32)
    z_gdn = jnp.zeros((n_p, GDN_HEADS, GDN_DK, GDN_DV), F32)
    z_conv = jnp.zeros((n_p, CONV_W - 1, GDN_QKV), F32)
    chunk_p = min(CHUNK, l_p)
    chunk_s = min(CHUNK, l_s)
    tm_p = min(256, n_p * l_p)
    tm_s = min(256, n_s * l_s)
    yp, gla_p, gdn_p, conv_p = _layer(x_prompt, z_gla, z_gdn, z_conv, w, nfw3, chunk_p, tm_p, min(64, tm_p))
    ys, gla_s, gdn_s, conv_s = _layer(x_sample, state_gla[0], state_gdn[0], state_gdn_conv[0], w, nfw3,
                                      chunk_s, tm_s, min(64, tm_s))
    return (yp, ys, gla_p[None], gdn_p[None], conv_p[None], gla_s[None], gdn_s[None], conv_s[None])
```

```python
import functools

import jax
import jax.numpy as jnp
from jax import lax
from jax.experimental import pallas as pl
from jax.experimental.pallas import tpu as pltpu

F32 = jnp.float32
BF16 = jnp.bfloat16
HI = lax.Precision.HIGHEST

NORM_EPS = 1e-6
CHUNK = 64
GLA_HEADS, GLA_DK, GLA_DV, GLA_LR, GLA_GATE_TAU = 4, 64, 128, 16, 16.0
GDN_HEADS, GDN_DK, GDN_DV, CONV_W = 4, 128, 128, 4
GLA_QK, GLA_V = GLA_HEADS * GLA_DK, GLA_HEADS * GLA_DV
GDN_QK, GDN_V = GDN_HEADS * GDN_DK, GDN_HEADS * GDN_DV
GDN_QKV = 2 * GDN_QK + GDN_V
PEER_HEADS, PEER_QDIM, N_KEYS, PEER_TOPK = 8, 256, 128, 16
PEER_HALF = PEER_QDIM // 2
N_SEL = PEER_HEADS * PEER_TOPK
HALF_EXPERTS = N_KEYS * N_KEYS // 2

LANES = 128
SUBLANES = 8
MIB = 1024 * 1024

SM_A0 = GLA_LR
SM_B0 = GLA_LR + GDN_HEADS


def _cparams(sem, vmem_mib):
    return pltpu.CompilerParams(dimension_semantics=sem, vmem_limit_bytes=vmem_mib * MIB)


def _softplus(x):
    return jnp.maximum(x, 0.0) + jnp.log(1.0 + jnp.exp(-jnp.abs(x)))


def _sigmoid(x):
    return 1.0 / (1.0 + jnp.exp(-x))


def _dot(a, b, prec=None):
    return jnp.dot(a, b, preferred_element_type=F32, precision=prec)


def _dot_nt(a, b, prec=None):
    return lax.dot_general(a, b, (((1,), (1,)), ((), ())), preferred_element_type=F32, precision=prec)


def _dot_tn(a, b, prec=None):
    return lax.dot_general(a, b, (((0,), (0,)), ((), ())), preferred_element_type=F32, precision=prec)


def _split_bf16(x):
    hi = x.astype(BF16)
    return hi, (x - hi.astype(F32)).astype(BF16)


def _split3_bf16(x):
    hi = x.astype(BF16)
    r = x - hi.astype(F32)
    mid = r.astype(BF16)
    return hi, mid, (r - mid.astype(F32)).astype(BF16)


def _dot3(a, b):
    ah, al = _split_bf16(a)
    bh, bl = _split_bf16(b)
    return _dot(jnp.concatenate([ah, ah, al], axis=1), jnp.concatenate([bh, bl, bh], axis=0))


def _dot3_nt(a, b):
    ah, al = _split_bf16(a)
    bh, bl = _split_bf16(b)
    return _dot_nt(jnp.concatenate([ah, ah, al], axis=1), jnp.concatenate([bh, bl, bh], axis=1))


def _dot3_tn(a, b):
    ah, al = _split_bf16(a)
    bh, bl = _split_bf16(b)
    return _dot_tn(jnp.concatenate([ah, ah, al], axis=0), jnp.concatenate([bh, bl, bh], axis=0))


def _dot_exact_lhs(a01, b):
    a16 = a01.astype(BF16)
    return _dot(jnp.concatenate([a16, a16, a16], axis=1), jnp.concatenate(_split3_bf16(b), axis=0))


def _dot_nt_exact_rhs(a, b01):
    b16 = b01.astype(BF16)
    return _dot_nt(jnp.concatenate(_split3_bf16(a), axis=1), jnp.concatenate([b16, b16, b16], axis=1))


_W_COLS = (("gq", GLA_QK), ("gk", GLA_QK), ("gv", GLA_V), ("gg", GLA_V), ("dqkv", GDN_QKV),
           ("dz", GDN_V), ("sm", LANES))


def _inproj_kernel(x_ref, nw_ref, w_ref, w2_ref, b2_ref, avec_ref, dtb_ref,
                   gq_ref, gk_ref, gv_ref, gg_ref, dqkv_ref, dz_ref, la_ref, sm_ref):
    x = x_ref[...]
    ms = jnp.mean(x * x, axis=-1, keepdims=True)
    h = (x * lax.rsqrt(ms + NORM_EPS) * nw_ref[...]).astype(BF16)
    outs = dict(gq=gq_ref, gk=gk_ref, gv=gv_ref, gg=gg_ref, dqkv=dqkv_ref, dz=dz_ref)
    off = 0
    ps = None
    for name, width in _W_COLS:
        p = _dot(h, w_ref[:, off:off + width])
        off += width
        if name == "sm":
            ps = p
        elif name == "gq":
            gq_ref[...] = p * (GLA_DK ** -0.5)
        else:
            outs[name][...] = p
    z = _dot(ps.astype(BF16), w2_ref[...]) + b2_ref[...]
    la_ref[...] = -_softplus(-z) * (1.0 / GLA_GATE_TAU)
    lane = lax.broadcasted_iota(jnp.int32, ps.shape, 1)
    log_a = avec_ref[...] * _softplus(ps + dtb_ref[...])
    beta = _sigmoid(ps)
    is_a = (lane >= SM_A0) & (lane < SM_B0)
    is_b = (lane >= SM_B0) & (lane < SM_B0 + GDN_HEADS)
    sm_ref[...] = jnp.where(is_a, log_a, jnp.where(is_b, beta, ps))


def _inproj(x, nw, wcat, w2p, b2, avec, dtb, tm):
    t, d = x.shape
    n_w = wcat.shape[1]
    widths = dict(_W_COLS)
    names = ("gq", "gk", "gv", "gg", "dqkv", "dz")
    out_shape = [jax.ShapeDtypeStruct((t, widths[n]), F32) for n in names]
    out_shape += [jax.ShapeDtypeStruct((t, GLA_QK), F32), jax.ShapeDtypeStruct((t, LANES), F32)]
    row = lambda w: pl.BlockSpec((tm, w), lambda i: (i, 0))
    full = lambda a: pl.BlockSpec(a.shape, lambda i: (0,) * a.ndim)
    return pl.pallas_call(
        _inproj_kernel,
        out_shape=out_shape,
        grid=(t // tm,),
        in_specs=[row(d), full(nw), full(wcat), full(w2p), full(b2), full(avec), full(dtb)],
        out_specs=[row(widths[n]) for n in names] + [row(GLA_QK), row(LANES)],
        compiler_params=_cparams(("parallel",), 48),
        name="inproj",
    )(x, nw, wcat, w2p, b2, avec, dtb)


def _iota2(n, m, axis):
    return lax.broadcasted_iota(jnp.int32, (n, m), axis)


def _gla_kernel(q_ref, k_ref, v_ref, la_ref, gg_ref, s0_ref, nw_ref, o_ref, sfin_ref, st_ref, *, chunk, n_sub):
    c = pl.program_id(1)
    nc = pl.num_programs(1)

    @pl.when(c == 0)
    def _():
        st_ref[...] = s0_ref[0]

    row = _iota2(chunk, chunk, 0)
    col = _iota2(chunk, chunk, 1)
    subs = range(n_sub)
    rows = [slice(j * chunk, (j + 1) * chunk) for j in subs]

    blocks = []
    b = chunk // 2
    while b >= 1:
        blocks.append(b)
        b //= 2
    sel = [row >= col]
    for b in blocks:
        sel.append(col <= (row // b) * b)
        sel.append(col <= jnp.minimum((row // b + 1) * b, chunk - 1))
    la_wide = jnp.concatenate([la_ref[r, :] for r in rows], axis=1)
    g_wide = _dot_exact_lhs(jnp.concatenate([m.astype(F32) for m in sel], axis=0), la_wide)

    heads = range(GLA_HEADS)
    dks = [slice(h * GLA_DK, (h + 1) * GLA_DK) for h in heads]
    dvs = [slice(h * GLA_DV, (h + 1) * GLA_DV) for h in heads]
    prep = []
    for j in subs:
        g_all = g_wide[:, j * GLA_QK:(j + 1) * GLA_QK]
        q = q_ref[rows[j], :]
        k = k_ref[rows[j], :]
        g = g_all[0:chunk]
        levels = []
        for li, b in enumerate(blocks):
            ref_r = g_all[(1 + 2 * li) * chunk:(2 + 2 * li) * chunk]
            ref_c = g_all[(2 + 2 * li) * chunk:(3 + 2 * li) * chunk]
            mask = (row // (2 * b) == col // (2 * b)) & (row // b == col // b + 1)
            levels.append((q * jnp.exp(g - ref_r), k * jnp.exp(ref_c - g), mask))
        g_last = g[chunk - 1:chunk, :]
        prep.append(dict(levels=levels, qk_diag=q * k, q_in=q * jnp.exp(g), k_out=k * jnp.exp(g_last - g),
                         decay_out=jnp.exp(g_last)))
    atts = [[jnp.zeros((chunk, chunk), F32) for _ in heads] for _ in subs]
    for li in range(len(blocks)):
        for j in subs:
            ql, kl, mask = prep[j]["levels"][li]
            atts[j] = [att + jnp.where(mask, _dot3_nt(ql[:, dk], kl[:, dk]), 0.0)
                       for att, dk in zip(atts[j], dks)]
    vhs = [[v_ref[rows[j], dv] for dv in dvs] for j in subs]
    intra = [[_dot3(atts[j][h], vhs[j][h])
              + jnp.sum(prep[j]["qk_diag"][:, dks[h]], axis=-1, keepdims=True) * vhs[j][h]
              for h in heads] for j in subs]
    kvs = [[_dot3_tn(vhs[j][h], prep[j]["k_out"][:, dks[h]]) for h in heads] for j in subs]
    sts = [st_ref[h] for h in heads]
    for j in subs:
        os_ = [_dot3_nt(prep[j]["q_in"][:, dks[h]], sts[h]) + intra[j][h] for h in heads]
        sts = [sts[h] * prep[j]["decay_out"][:, dks[h]] + kvs[j][h] for h in heads]
        for h in heads:
            o = os_[h]
            ms = jnp.mean(o * o, axis=-1, keepdims=True)
            on = o * lax.rsqrt(ms + NORM_EPS) * nw_ref[...]
            gate = gg_ref[rows[j], dvs[h]]
            o_ref[rows[j], dvs[h]] = on * (gate * _sigmoid(gate))
    for h in heads:
        st_ref[h] = sts[h]

    @pl.when(c == nc - 1)
    def _():
        sfin_ref[0] = st_ref[...]


def _chunks_per_step(seq_len, chunk):
    return 2 if seq_len % (2 * chunk) == 0 else 1


def _gla(gq, gk, gv, la, gg, s0t, nw, n_seq, seq_len, chunk):
    n_sub = _chunks_per_step(seq_len, chunk)
    nc = seq_len // (chunk * n_sub)
    t = n_seq * seq_len
    blk = lambda w: pl.BlockSpec((chunk * n_sub, w), lambda b, c: (b * nc + c, 0))
    st_spec = pl.BlockSpec((1, GLA_HEADS, GLA_DV, GLA_DK), lambda b, c: (b, 0, 0, 0))
    return pl.pallas_call(
        functools.partial(_gla_kernel, chunk=chunk, n_sub=n_sub),
        out_shape=[jax.ShapeDtypeStruct((t, GLA_V), F32),
                   jax.ShapeDtypeStruct((n_seq, GLA_HEADS, GLA_DV, GLA_DK), F32)],
        grid=(n_seq, nc),
        in_specs=[blk(GLA_QK), blk(GLA_QK), blk(GLA_V), blk(GLA_QK), blk(GLA_V), st_spec,
                  pl.BlockSpec((1, GLA_DV), lambda b, c: (0, 0))],
        out_specs=[blk(GLA_V), st_spec],
        scratch_shapes=[pltpu.VMEM((GLA_HEADS, GLA_DV, GLA_DK), F32)],
        compiler_params=_cparams(("parallel", "arbitrary"), 32),
        name="gla_scan",
    )(gq, gk, gv, la, gg, s0t, nw)


_INV_BASE = 16


def _unit_lower_inverses(lows, row, col, n):
    eye = (row == col).astype(F32)
    base = min(_INV_BASE, n)
    diag_blk = row // base == col // base
    pws = [jnp.where(diag_blk, low, 0.0) for low in lows]
    invs = [eye - ld for ld in pws]
    p = 2
    while p < base:
        pws = [_dot3(pw, pw) for pw in pws]
        invs = [inv + _dot3(inv, pw) for inv, pw in zip(invs, pws)]
        p *= 2
    b = base
    while b < n:
        sib = (row // (2 * b) == col // (2 * b)) & (row // b != col // b)
        tmps = [_dot3(jnp.where(sib, low, 0.0), inv) for low, inv in zip(lows, invs)]
        invs = [inv - _dot3(inv, t) for inv, t in zip(invs, tmps)]
        b *= 2
    return invs


def _gdn_kernel(x_ref, ab_ref, dz_ref, tail0_ref, cw_ref, s0_ref, nw_ref,
                o_ref, sfin_ref, tail_out_ref, st_ref, tail_ref, *, chunk, n_sub):
    c = pl.program_id(1)
    nc = pl.num_programs(1)

    @pl.when(c == 0)
    def _():
        st_ref[...] = s0_ref[0]
        tail_ref[...] = tail0_ref[0]

    n_rows = chunk * n_sub
    x = x_ref[...]
    ext = jnp.concatenate([tail_ref[...], x], axis=0)
    base = SUBLANES - (CONV_W - 1)
    conv = ext[base:base + n_rows] * cw_ref[0:1, :]
    for i in range(1, CONV_W):
        conv = conv + ext[base + i:base + i + n_rows] * cw_ref[i:i + 1, :]
    tail_ref[...] = x[n_rows - SUBLANES:n_rows]
    conv = conv * _sigmoid(conv)

    row = _iota2(chunk, chunk, 0)
    col = _iota2(chunk, chunk, 1)
    lower = row >= col
    tril = lower.astype(F32)

    heads = range(GDN_HEADS)
    subs = range(n_sub)
    qs, ks, vs, gcs, betas, decs = [], [], [], [], [], []
    for j in subs:
        rows = slice(j * chunk, (j + 1) * chunk)
        ab = ab_ref[rows, :]
        g_col = _dot_exact_lhs(tril, ab)
        g_row = _dot_nt_exact_rhs(ab.T, tril)
        for h in heads:
            cq = conv[rows, h * GDN_DK:(h + 1) * GDN_DK]
            ck = conv[rows, GDN_QK + h * GDN_DK:GDN_QK + (h + 1) * GDN_DK]
            vs.append(conv[rows, 2 * GDN_QK + h * GDN_DV:2 * GDN_QK + (h + 1) * GDN_DV])
            qs.append(cq * lax.rsqrt(jnp.sum(cq * cq, axis=-1, keepdims=True) + NORM_EPS) * (GDN_DK ** -0.5))
            ks.append(ck * lax.rsqrt(jnp.sum(ck * ck, axis=-1, keepdims=True) + NORM_EPS))
            gc = g_col[:, SM_A0 + h:SM_A0 + h + 1]
            gr = g_row[SM_A0 + h:SM_A0 + h + 1, :]
            gcs.append(gc)
            betas.append(ab[:, SM_B0 + h:SM_B0 + h + 1])
            decs.append(jnp.where(lower, jnp.exp(jnp.where(lower, gc - gr, 0.0)), 0.0))
    kks = [_dot3_nt(k, k) for k in ks]
    qks = [_dot3_nt(q, k) * dec for q, k, dec in zip(qs, ks, decs)]
    lows = [jnp.where(row > col, beta * kk * dec, 0.0) for beta, kk, dec in zip(betas, kks, decs)]
    tinvs = _unit_lower_inverses(lows, row, col, chunk)
    egs = [jnp.exp(gc) for gc in gcs]
    uws = [_dot3(tinv, jnp.concatenate([v * beta, k * (beta * eg)], axis=1))
           for tinv, v, k, beta, eg in zip(tinvs, vs, ks, betas, egs)]
    sts = [st_ref[h] for h in heads]
    for j in subs:
        rows = slice(j * chunk, (j + 1) * chunk)
        ids = [j * GDN_HEADS + h for h in heads]
        v_news = [uws[i][:, :GDN_DV] - _dot3(uws[i][:, GDN_DV:], st) for i, st in zip(ids, sts)]
        os_ = [_dot3(qs[i] * egs[i], st) + _dot3(qks[i], v_new) for i, st, v_new in zip(ids, sts, v_news)]
        new_sts = []
        for h, i in enumerate(ids):
            g_last = gcs[i][chunk - 1:chunk, :]
            new_sts.append(sts[h] * jnp.exp(g_last) + _dot3_tn(ks[i] * jnp.exp(g_last - gcs[i]), v_news[h]))
        sts = new_sts
        for h in heads:
            o = os_[h]
            ms = jnp.mean(o * o, axis=-1, keepdims=True)
            on = o * lax.rsqrt(ms + NORM_EPS) * nw_ref[...]
            gate = dz_ref[rows, h * GDN_DV:(h + 1) * GDN_DV]
            o_ref[rows, h * GDN_DV:(h + 1) * GDN_DV] = on * (gate * _sigmoid(gate))
    for h in heads:
        st_ref[h] = sts[h]

    @pl.when(c == nc - 1)
    def _():
        sfin_ref[0] = st_ref[...]
        tail_out_ref[0] = tail_ref[...]


def _gdn(dqkv, ab, dz, tail0, cw, s0, nw, n_seq, seq_len, chunk):
    n_sub = _chunks_per_step(seq_len, chunk)
    nc = seq_len // (chunk * n_sub)
    t = n_seq * seq_len
    blk = lambda w: pl.BlockSpec((chunk * n_sub, w), lambda b, c: (b * nc + c, 0))
    st_spec = pl.BlockSpec((1, GDN_HEADS, GDN_DK, GDN_DV), lambda b, c: (b, 0, 0, 0))
    tail_spec = pl.BlockSpec((1, SUBLANES, GDN_QKV), lambda b, c: (b, 0, 0))
    return pl.pallas_call(
        functools.partial(_gdn_kernel, chunk=chunk, n_sub=n_sub),
        out_shape=[jax.ShapeDtypeStruct((t, GDN_V), F32),
                   jax.ShapeDtypeStruct((n_seq, GDN_HEADS, GDN_DK, GDN_DV), F32),
                   jax.ShapeDtypeStruct((n_seq, SUBLANES, GDN_QKV), F32)],
        grid=(n_seq, nc),
        in_specs=[blk(GDN_QKV), blk(LANES), blk(GDN_V), tail_spec,
                  pl.BlockSpec((SUBLANES, GDN_QKV), lambda b, c: (0, 0)), st_spec,
                  pl.BlockSpec((1, GDN_DV), lambda b, c: (0, 0))],
        out_specs=[blk(GDN_V), st_spec, tail_spec],
        scratch_shapes=[pltpu.VMEM((GDN_HEADS, GDN_DK, GDN_DV), F32),
                        pltpu.VMEM((SUBLANES, GDN_QKV), F32)],
        compiler_params=_cparams(("parallel", "arbitrary"), 32),
        name="gdn_scan",
    )(dqkv, ab, dz, tail0, cw, s0, nw)


def _topk_cols(s, k, payload=None):
    n = s.shape[0]
    rid = lax.broadcasted_iota(jnp.int32, s.shape, 0).astype(F32)
    vals, idxs = [], []
    for _ in range(k):
        m = jnp.max(s, axis=0, keepdims=True)
        am = jnp.min(jnp.where(s == m, rid, float(n)), axis=0, keepdims=True)
        sel = rid == am
        vals.append(m)
        if payload is None:
            idxs.append(am)
        else:
            idxs.append(jnp.sum(jnp.where(sel, payload, 0.0), axis=0, keepdims=True))
        s = jnp.where(sel, -jnp.inf, s)
    return jnp.concatenate(vals, axis=0), jnp.concatenate(idxs, axis=0)


def _post_kernel(o1_ref, o2_ref, x_ref, wo_ref, nw_ref, wq_ref, k1_ref, k2_ref,
                 x2_ref, hn_ref, j8_ref, sh_ref, gate_ref):
    mix = (_dot(o1_ref[...].astype(BF16), wo_ref[0:GLA_V, :])
           + _dot(o2_ref[...].astype(BF16), wo_ref[GLA_V:GLA_V + GDN_V, :]))
    x2 = x_ref[...] + mix
    x2_ref[...] = x2
    ms = jnp.mean(x2 * x2, axis=-1, keepdims=True)
    hn = x2 * lax.rsqrt(ms + NORM_EPS) * nw_ref[...]
    hn_ref[...] = hn
    qry = _dot(hn.astype(BF16), wq_ref[...])
    idx_rows, gate_rows = [], []
    for h in range(PEER_HEADS):
        q1 = qry[:, h * PEER_QDIM:h * PEER_QDIM + PEER_HALF].astype(BF16)
        q2 = qry[:, h * PEER_QDIM + PEER_HALF:(h + 1) * PEER_QDIM].astype(BF16)
        s1 = _dot_nt(k1_ref[h], q1)
        s2 = _dot_nt(k2_ref[h], q2)
        v1, i1 = _topk_cols(s1, PEER_TOPK)
        v2, i2 = _topk_cols(s2, PEER_TOPK)
        cand_rows, cidx_rows = [], []
        for a in range(PEER_TOPK // 2):
            n_valid = PEER_TOPK // (a + 1)
            n_rows = -(-n_valid // SUBLANES) * SUBLANES
            c = v1[a:a + 1, :] + v2[0:n_rows, :]
            if n_valid < n_rows:
                c = jnp.where(lax.broadcasted_iota(jnp.int32, c.shape, 0) < n_valid, c, -jnp.inf)
            cand_rows.append(c)
            cidx_rows.append(i1[a:a + 1, :] * float(N_KEYS) + i2[0:n_rows, :])
        cand_rows.append(v1[PEER_TOPK // 2:, :] + v2[0:1, :])
        cidx_rows.append(i1[PEER_TOPK // 2:, :] * float(N_KEYS) + i2[0:1, :])
        sc, eidx = _topk_cols(jnp.concatenate(cand_rows, axis=0), PEER_TOPK,
                              payload=jnp.concatenate(cidx_rows, axis=0))
        e = jnp.exp(sc - jnp.max(sc, axis=0, keepdims=True))
        gate_rows.append(e / jnp.sum(e, axis=0, keepdims=True))
        idx_rows.append(eidx)
    e = jnp.concatenate(idx_rows, axis=0).T.astype(jnp.int32)
    j8_ref[...] = (e & (HALF_EXPERTS - 1)) * SUBLANES
    sh_ref[...] = jnp.where(e >= HALF_EXPERTS, 0.0, 16.0)
    gate_ref[...] = jnp.concatenate(gate_rows, axis=0).T


def _post(o1, o2, x, wo, nw, wq, k1, k2, tm):
    t, d = x.shape
    row = lambda w: pl.BlockSpec((tm, w), lambda i: (i, 0))
    full = lambda a: pl.BlockSpec(a.shape, lambda i: (0,) * a.ndim)
    return pl.pallas_call(
        _post_kernel,
        out_shape=[jax.ShapeDtypeStruct((t, d), F32), jax.ShapeDtypeStruct((t, d), F32),
                   jax.ShapeDtypeStruct((t, N_SEL), jnp.int32), jax.ShapeDtypeStruct((t, N_SEL), F32),
                   jax.ShapeDtypeStruct((t, N_SEL), F32)],
        grid=(t // tm,),
        in_specs=[row(GLA_V), row(GDN_V), row(d), full(wo), full(nw), full(wq), full(k1), full(k2)],
        out_specs=[row(d), row(d), row(N_SEL), row(N_SEL), row(N_SEL)],
        compiler_params=_cparams(("parallel",), 48),
        name="post_topk",
    )(o1, o2, x, wo, nw, wq, k1, k2)


_HI_MASK = -65536
_IDX_SLOTS = 8


def _expert_row(tab_ref, j8, shift_row):
    words = tab_ref[pl.ds(pl.multiple_of(j8, SUBLANES), SUBLANES), :]
    bits = lax.shift_left(words, jnp.broadcast_to(shift_row, (SUBLANES, LANES))) & _HI_MASK
    return lax.bitcast_convert_type(bits, F32)


def _diag_rows(vals, eye):
    n = vals.shape[0]
    return (vals[:, None, :] * eye[None, :, :]).reshape(n * N_SEL, N_SEL)


def _fold_sublanes(vs, sub):
    m4 = sub < 4
    cur = []
    for a, b in zip(vs[0::2], vs[1::2]):
        cur.append(jnp.where(m4, a, b) + pltpu.roll(jnp.where(m4, b, a), 4, 0))
    for half in (2, 1):
        m = (sub % (2 * half)) < half
        nxt = []
        for a, b in zip(cur[0::2], cur[1::2]):
            ta = a + pltpu.roll(a, SUBLANES - half, 0)
            tb = b + pltpu.roll(b, half, 0)
            nxt.append(jnp.where(m, ta, tb))
        cur = nxt
    return cur[0]


def _eye():
    return (_iota2(N_SEL, N_SEL, 0) == _iota2(N_SEL, N_SEL, 1)).astype(F32)


def _peer_u_kernel(j8_ref, sh_ref, hn_ref, gate_ref, tab_ref, w_ref, sd0_ref, sd1_ref, q0_ref, q1_ref, *, tb):
    sub = lax.broadcasted_iota(jnp.int32, (SUBLANES, LANES), 0)
    eye = _eye()
    ones = jnp.ones((N_SEL, LANES), BF16)
    ones2 = jnp.ones((2 * N_SEL, LANES), BF16)
    perm = (0, 4, 2, 6, 1, 5, 3, 7)

    sd_refs = (sd0_ref, sd1_ref)
    q_refs = (q0_ref, q1_ref)

    def shift_rows(src_tok, dst_ref):
        diag = _diag_rows(sh_ref[pl.ds(src_tok, 1), :], eye).astype(BF16)
        dst_ref[...] = _dot(diag, ones).astype(jnp.int32)

    def finish(src_ref, out_tok):
        q_hi, q_lo = _split_bf16(src_ref[...])
        rep = _dot(jnp.concatenate([q_hi, q_lo], axis=1), ones2)
        act = jnp.sum(rep * eye, axis=0, keepdims=True)
        gelu = 0.5 * act * (1.0 + jnp.tanh(0.7978845608028654 * (act + 0.044715 * act * act * act)))
        w_ref[pl.ds(out_tok, 1), :] = gate_ref[pl.ds(out_tok, 1), :] * gelu

    q1_ref[...] = jnp.zeros((N_SEL, LANES), F32)
    shift_rows(0, sd0_ref)

    def body(i, carry):
        for p in range(2):
            t = 2 * i + p
            shift_rows(jnp.minimum(t + 1, tb - 1), sd_refs[1 - p])
            finish(q_refs[1 - p], jnp.maximum(t - 1, 0))
            x = hn_ref[t]
            folded = []
            for g in range(N_SEL // SUBLANES):
                prods = []
                for r in range(SUBLANES):
                    k = g * SUBLANES + perm[r]
                    prods.append(_expert_row(tab_ref, j8_ref[t, k], sd_refs[p][k:k + 1, :]) * x)
                folded.append(_fold_sublanes(prods, sub))
            q_refs[p][...] = jnp.concatenate(folded, axis=0)
        return carry

    lax.fori_loop(0, tb // 2, body, 0)
    finish(q_refs[(tb - 1) % 2], tb - 1)


def _peer_u(j8, sh, hn3, gate, tab, tb):
    t = j8.shape[0]
    vec = lambda: pl.BlockSpec((tb, N_SEL), lambda i: (i, 0))
    return pl.pallas_call(
        functools.partial(_peer_u_kernel, tb=tb),
        out_shape=jax.ShapeDtypeStruct((t, N_SEL), F32),
        grid=(t // tb,),
        in_specs=[pl.BlockSpec((tb, N_SEL), lambda i: (i, 0), memory_space=pltpu.SMEM), vec(),
                  pl.BlockSpec((tb, SUBLANES, LANES), lambda i: (i, 0, 0)), vec(),
                  pl.BlockSpec(memory_space=pltpu.VMEM)],
        out_specs=vec(),
        scratch_shapes=[pltpu.VMEM((N_SEL, LANES), jnp.int32), pltpu.VMEM((N_SEL, LANES), jnp.int32),
                        pltpu.VMEM((N_SEL, LANES), F32), pltpu.VMEM((N_SEL, LANES), F32)],
        compiler_params=_cparams(("arbitrary",), 56),
        name="peer_u",
    )(j8, sh, hn3, gate, tab)


def _peer_v_kernel(j8_ref, sh_ref, wgt_ref, x2_ref, nw_ref, tab_ref, y_ref, sd_ref, wd_ref, idx_ref, sem,
                   *, tb, n_acc):
    eye = _eye()
    blk = (_iota2(2 * N_SEL, 2 * LANES, 0) // N_SEL == _iota2(2 * N_SEL, 2 * LANES, 1) // LANES).astype(BF16)
    half_toks = _IDX_SLOTS // 2
    half_rows = half_toks * N_SEL

    def pre(c, carry):
        for h in range(2):
            tok0 = pl.multiple_of(c * _IDX_SLOTS, _IDX_SLOTS) + h * half_toks
            toks = pl.ds(tok0, half_toks)
            diag = jnp.concatenate([_diag_rows(sh_ref[toks, :], eye).astype(BF16),
                                    _diag_rows(wgt_ref[toks, :], eye).astype(BF16)], axis=1)
            rep = _dot(diag, blk)
            dst = pl.ds(pl.multiple_of(tok0 * N_SEL, half_rows), half_rows)
            sd_ref[dst, :] = rep[:, :LANES].astype(jnp.int32)
            wd_ref[dst, :] = rep[:, LANES:]
        return carry

    lax.fori_loop(0, tb // _IDX_SLOTS, pre, 0)

    def idx_copy(t, slot):
        return pltpu.make_async_copy(j8_ref.at[t], idx_ref.at[slot], sem.at[slot])

    for s in range(_IDX_SLOTS):
        idx_copy(s, s).start()

    def group(gi, carry):
        for s in range(_IDX_SLOTS):
            t = gi * _IDX_SLOTS + s
            idx_copy(t, s).wait()
            base = t * N_SEL
            accs = [jnp.zeros((SUBLANES, LANES), F32) for _ in range(n_acc)]
            for k in range(N_SEL):
                row = _expert_row(tab_ref, idx_ref[s, 0, k], sd_ref[pl.ds(base + k, 1), :])
                accs[k % n_acc] = accs[k % n_acc] + row * wd_ref[pl.ds(base + k, 1), :]
            tot = accs[0]
            for a in accs[1:]:
                tot = tot + a
            y_ref[t] = x2_ref[t] + tot

            @pl.when(t + _IDX_SLOTS < tb)
            def _():
                idx_copy(t + _IDX_SLOTS, s).start()
        return carry

    lax.fori_loop(0, tb // _IDX_SLOTS, group, 0)
    x3 = y_ref[...]
    ss = jnp.sum(jnp.sum(x3 * x3, axis=2, keepdims=True), axis=1, keepdims=True)
    y_ref[...] = x3 * lax.rsqrt(ss * (1.0 / (SUBLANES * LANES)) + NORM_EPS) * nw_ref[...][None]


def _peer_v(j8, sh, wgt, x23, nw3, tab, tb):
    t = j8.shape[0]
    vec = lambda: pl.BlockSpec((tb, N_SEL), lambda i: (i, 0))
    return pl.pallas_call(
        functools.partial(_peer_v_kernel, tb=tb, n_acc=4),
        out_shape=jax.ShapeDtypeStruct((t, SUBLANES, LANES), F32),
        grid=(t // tb,),
        in_specs=[pl.BlockSpec((tb, 1, N_SEL), lambda i: (i, 0, 0)), vec(), vec(),
                  pl.BlockSpec((tb, SUBLANES, LANES), lambda i: (i, 0, 0)),
                  pl.BlockSpec((SUBLANES, LANES), lambda i: (0, 0)),
                  pl.BlockSpec(memory_space=pltpu.VMEM)],
        out_specs=pl.BlockSpec((tb, SUBLANES, LANES), lambda i: (i, 0, 0)),
        scratch_shapes=[pltpu.VMEM((tb * N_SEL, LANES), jnp.int32), pltpu.VMEM((tb * N_SEL, LANES), F32),
                        pltpu.SMEM((_IDX_SLOTS, 1, N_SEL), jnp.int32),
                        pltpu.SemaphoreType.DMA((_IDX_SLOTS,))],
        compiler_params=_cparams(("arbitrary",), 56),
        name="peer_v",
    )(j8.reshape(t, 1, N_SEL), sh, wgt, x23, nw3, tab)


_PACK_ROWS = 256


def _pack_kernel(lo_ref, hi_ref, o_ref):
    lo = lax.bitcast_convert_type(lo_ref[...].astype(BF16).astype(F32), jnp.int32)
    hi = lax.bitcast_convert_type(hi_ref[...].astype(BF16).astype(F32), jnp.int32)
    packed = hi | lax.shift_right_logical(lo, 16)
    rows, d = packed.shape
    for s in range(d // LANES):
        o_ref[pl.ds(s, rows, stride=d // LANES), :] = packed[:, s * LANES:(s + 1) * LANES]


def _pack_table(tab):
    n, d = tab.shape
    sub = d // LANES
    steps = n // 2 // _PACK_ROWS
    return pl.pallas_call(
        _pack_kernel,
        out_shape=jax.ShapeDtypeStruct((n // 2 * sub, LANES), jnp.int32),
        grid=(steps,),
        in_specs=[pl.BlockSpec((_PACK_ROWS, d), lambda i: (i, 0)),
                  pl.BlockSpec((_PACK_ROWS, d), lambda i: (i + steps, 0))],
        out_specs=pl.BlockSpec((_PACK_ROWS * sub, LANES), lambda i: (i, 0)),
        compiler_params=_cparams(("arbitrary",), 32),
        name="pack_table",
    )(tab, tab)


def _prep_weights(norm_mix_w, w_in, gla_w_gk2, gla_b_gk, gla_norm_w, gdn_conv_w, gdn_a_log,
                  gdn_dt_bias, gdn_norm_w, w_out, norm_ffn_w, peer_wq, peer_k1, peer_k2, peer_u, peer_v):
    d = w_in.shape[0]
    o_glr = 2 * GLA_QK + GLA_V
    o_gg = o_glr + GLA_LR
    o_qkv = o_gg + GLA_V
    o_da = o_qkv + GDN_QKV
    o_dz = o_da + 2 * GDN_HEADS
    small = jnp.concatenate([w_in[:, o_glr:o_gg], w_in[:, o_da:o_dz],
                             jnp.zeros((d, LANES - GLA_LR - 2 * GDN_HEADS), w_in.dtype)], axis=1)
    wcat = jnp.concatenate([w_in[:, :o_glr], w_in[:, o_gg:o_qkv], w_in[:, o_qkv:o_da],
                            w_in[:, o_dz:], small], axis=1).astype(BF16)
    w2p = jnp.zeros((LANES, GLA_QK), F32).at[:GLA_LR].set(gla_w_gk2).astype(BF16)
    avec = jnp.zeros((1, LANES), F32).at[0, SM_A0:SM_B0].set(-jnp.exp(gdn_a_log))
    dtb = jnp.zeros((1, LANES), F32).at[0, SM_A0:SM_B0].set(gdn_dt_bias)
    cw = jnp.zeros((SUBLANES, GDN_QKV), F32).at[:CONV_W].set(gdn_conv_w)
    return dict(
        nmix=norm_mix_w.reshape(1, d), wcat=wcat, w2p=w2p, b2=gla_b_gk.reshape(1, GLA_QK), avec=avec, dtb=dtb,
        gla_nw=gla_norm_w.reshape(1, GLA_DV), cw=cw, gdn_nw=gdn_norm_w.reshape(1, GDN_DV),
        wo=w_out.astype(BF16), nffn=norm_ffn_w.reshape(1, d), wq=peer_wq.astype(BF16),
        k1=peer_k1.astype(BF16), k2=peer_k2.astype(BF16), tab_u=_pack_table(peer_u), tab_v=_pack_table(peer_v))


def _layer(x, s_gla, s_gdn, conv_buf, w, nfw3, chunk, tm, tb):
    n_seq, seq_len, d = x.shape
    t = n_seq * seq_len
    xf = x.reshape(t, d)
    gq, gk, gv, gg, dqkv, dz, la, sm = _inproj(xf, w["nmix"], w["wcat"], w["w2p"], w["b2"], w["avec"],
                                               w["dtb"], tm)
    o1, gla_t = _gla(gq, gk, gv, la, gg, jnp.swapaxes(s_gla, -1, -2), w["gla_nw"], n_seq, seq_len, chunk)
    tail0 = jnp.pad(conv_buf, ((0, 0), (SUBLANES - (CONV_W - 1), 0), (0, 0)))
    o2, gdn_new, tail = _gdn(dqkv, sm, dz, tail0, w["cw"], s_gdn, w["gdn_nw"], n_seq, seq_len, chunk)
    x2, hn, j8, sh, gate = _post(o1, o2, xf, w["wo"], w["nffn"], w["wq"], w["k1"], w["k2"], tm)
    wgt = _peer_u(j8, sh, hn.reshape(t, SUBLANES, LANES), gate, w["tab_u"], tb)
    y = _peer_v(j8, sh, wgt, x2.reshape(t, SUBLANES, LANES), nfw3, w["tab_v"], tb)
    return (y.reshape(n_seq, seq_len, d), jnp.swapaxes(gla_t, -1, -2), gdn_new,
            tail[:, SUBLANES - (CONV_W - 1):, :])


def kernel(x_prompt, x_sample, state_gla, state_gdn, state_gdn_conv, norm_mix_w, w_in, gla_w_gk2, gla_b_gk,
           gla_norm_w, gdn_conv_w, gdn_a_log, gdn_dt_bias, gdn_norm_w, w_out, norm_ffn_w, peer_wq, peer_k1,
           peer_k2, peer_u, peer_v, norm_final_w):
    depth = w_in.shape[0]
    assert depth == 1, "the final norm is fused into the last layer's PEER pass"
    n_p, l_p, d = x_prompt.shape
    n_s, l_s, _ = x_sample.shape
    nfw3 = norm_final_w.reshape(SUBLANES, LANES)
    w = _prep_weights(norm_mix_w[0], w_in[0], gla_w_gk2[0], gla_b_gk[0], gla_norm_w[0], gdn_conv_w[0],
                      gdn_a_log[0], gdn_dt_bias[0], gdn_norm_w[0], w_out[0], norm_ffn_w[0], peer_wq[0],
                      peer_k1[0], peer_k2[0], peer_u[0], peer_v[0])
    z_gla = jnp.zeros((n_p, GLA_HEADS, GLA_DK, GLA_DV), F32)
    z_gdn = jnp.zeros((n_p, GDN_HEADS, GDN_DK, GDN_DV), F32)
    z_conv = jnp.zeros((n_p, CONV_W - 1, GDN_QKV), F32)
    chunk_p = min(CHUNK, l_p)
    chunk_s = min(CHUNK, l_s)
    tm_p = min(256, n_p * l_p)
    tm_s = min(256, n_s * l_s)
    yp, gla_p, gdn_p, conv_p = _layer(x_prompt, z_gla, z_gdn, z_conv, w, nfw3, chunk_p, tm_p, min(64, tm_p))
    ys, gla_s, gdn_s, conv_s = _layer(x_sample, state_gla[0], state_gdn[0], state_gdn_conv[0], w, nfw3,
                                      chunk_s, tm_s, min(64, tm_s))
    return (yp, ys, gla_p[None], gdn_p[None], conv_p[None], gla_s[None], gdn_s[None], conv_s[None])
```

```python
import functools

import jax
import jax.numpy as jnp
from jax import lax
from jax.experimental import pallas as pl
from jax.experimental.pallas import tpu as pltpu

F32 = jnp.float32
BF16 = jnp.bfloat16
HI = lax.Precision.HIGHEST

NORM_EPS = 1e-6
CHUNK = 64
GLA_HEADS, GLA_DK, GLA_DV, GLA_LR, GLA_GATE_TAU = 4, 64, 128, 16, 16.0
GDN_HEADS, GDN_DK, GDN_DV, CONV_W = 4, 128, 128, 4
GLA_QK, GLA_V = GLA_HEADS * GLA_DK, GLA_HEADS * GLA_DV
GDN_QK, GDN_V = GDN_HEADS * GDN_DK, GDN_HEADS * GDN_DV
GDN_QKV = 2 * GDN_QK + GDN_V
PEER_HEADS, PEER_QDIM, N_KEYS, PEER_TOPK = 8, 256, 128, 16
PEER_HALF = PEER_QDIM // 2
N_SEL = PEER_HEADS * PEER_TOPK
HALF_EXPERTS = N_KEYS * N_KEYS // 2

LANES = 128
SUBLANES = 8
MIB = 1024 * 1024

SM_A0 = GLA_LR
SM_B0 = GLA_LR + GDN_HEADS


def _cparams(sem, vmem_mib):
    return pltpu.CompilerParams(dimension_semantics=sem, vmem_limit_bytes=vmem_mib * MIB)


def _softplus(x):
    return jnp.maximum(x, 0.0) + jnp.log(1.0 + jnp.exp(-jnp.abs(x)))


def _sigmoid(x):
    return 1.0 / (1.0 + jnp.exp(-x))


def _dot(a, b, prec=None):
    return jnp.dot(a, b, preferred_element_type=F32, precision=prec)


def _dot_nt(a, b, prec=None):
    return lax.dot_general(a, b, (((1,), (1,)), ((), ())), preferred_element_type=F32, precision=prec)


def _dot_tn(a, b, prec=None):
    return lax.dot_general(a, b, (((0,), (0,)), ((), ())), preferred_element_type=F32, precision=prec)


def _split_bf16(x):
    hi = x.astype(BF16)
    return hi, (x - hi.astype(F32)).astype(BF16)


def _split3_bf16(x):
    hi = x.astype(BF16)
    r = x - hi.astype(F32)
    mid = r.astype(BF16)
    return hi, mid, (r - mid.astype(F32)).astype(BF16)


def _dot3(a, b):
    ah, al = _split_bf16(a)
    bh, bl = _split_bf16(b)
    return _dot(jnp.concatenate([ah, ah, al], axis=1), jnp.concatenate([bh, bl, bh], axis=0))


def _dot3_nt(a, b):
    ah, al = _split_bf16(a)
    bh, bl = _split_bf16(b)
    return _dot_nt(jnp.concatenate([ah, ah, al], axis=1), jnp.concatenate([bh, bl, bh], axis=1))


def _dot3_tn(a, b):
    ah, al = _split_bf16(a)
    bh, bl = _split_bf16(b)
    return _dot_tn(jnp.concatenate([ah, ah, al], axis=0), jnp.concatenate([bh, bl, bh], axis=0))


def _dot_exact_lhs(a01, b):
    a16 = a01.astype(BF16)
    return _dot(jnp.concatenate([a16, a16, a16], axis=1), jnp.concatenate(_split3_bf16(b), axis=0))


def _dot_nt_exact_rhs(a, b01):
    b16 = b01.astype(BF16)
    return _dot_nt(jnp.concatenate(_split3_bf16(a), axis=1), jnp.concatenate([b16, b16, b16], axis=1))


_W_COLS = (("gq", GLA_QK), ("gk", GLA_QK), ("gv", GLA_V), ("gg", GLA_V), ("dqkv", GDN_QKV),
           ("dz", GDN_V), ("sm", LANES))


def _inproj_kernel(x_ref, nw_ref, w_ref, w2_ref, b2_ref, avec_ref, dtb_ref,
                   gq_ref, gk_ref, gv_ref, gg_ref, dqkv_ref, dz_ref, la_ref, sm_ref):
    x = x_ref[...]
    ms = jnp.mean(x * x, axis=-1, keepdims=True)
    h = (x * lax.rsqrt(ms + NORM_EPS) * nw_ref[...]).astype(BF16)
    outs = dict(gq=gq_ref, gk=gk_ref, gv=gv_ref, gg=gg_ref, dqkv=dqkv_ref, dz=dz_ref)
    off = 0
    ps = None
    for name, width in _W_COLS:
        p = _dot(h, w_ref[:, off:off + width])
        off += width
        if name == "sm":
            ps = p
        elif name == "gq":
            gq_ref[...] = p * (GLA_DK ** -0.5)
        else:
            outs[name][...] = p
    z = _dot(ps.astype(BF16), w2_ref[...]) + b2_ref[...]
    la_ref[...] = -_softplus(-z) * (1.0 / GLA_GATE_TAU)
    lane = lax.broadcasted_iota(jnp.int32, ps.shape, 1)
    log_a = avec_ref[...] * _softplus(ps + dtb_ref[...])
    beta = _sigmoid(ps)
    is_a = (lane >= SM_A0) & (lane < SM_B0)
    is_b = (lane >= SM_B0) & (lane < SM_B0 + GDN_HEADS)
    sm_ref[...] = jnp.where(is_a, log_a, jnp.where(is_b, beta, ps))


def _inproj(x, nw, wcat, w2p, b2, avec, dtb, tm):
    t, d = x.shape
    n_w = wcat.shape[1]
    widths = dict(_W_COLS)
    names = ("gq", "gk", "gv", "gg", "dqkv", "dz")
    out_shape = [jax.ShapeDtypeStruct((t, widths[n]), F32) for n in names]
    out_shape += [jax.ShapeDtypeStruct((t, GLA_QK), F32), jax.ShapeDtypeStruct((t, LANES), F32)]
    row = lambda w: pl.BlockSpec((tm, w), lambda i: (i, 0))
    full = lambda a: pl.BlockSpec(a.shape, lambda i: (0,) * a.ndim)
    return pl.pallas_call(
        _inproj_kernel,
        out_shape=out_shape,
        grid=(t // tm,),
        in_specs=[row(d), full(nw), full(wcat), full(w2p), full(b2), full(avec), full(dtb)],
        out_specs=[row(widths[n]) for n in names] + [row(GLA_QK), row(LANES)],
        compiler_params=_cparams(("parallel",), 48),
        name="inproj",
    )(x, nw, wcat, w2p, b2, avec, dtb)


def _iota2(n, m, axis):
    return lax.broadcasted_iota(jnp.int32, (n, m), axis)


def _gla_kernel(q_ref, k_ref, v_ref, la_ref, gg_ref, s0_ref, nw_ref, o_ref, sfin_ref, st_ref, *, chunk, n_sub):
    c = pl.program_id(1)
    nc = pl.num_programs(1)

    @pl.when(c == 0)
    def _():
        st_ref[...] = s0_ref[0]

    row = _iota2(chunk, chunk, 0)
    col = _iota2(chunk, chunk, 1)
    subs = range(n_sub)
    rows = [slice(j * chunk, (j + 1) * chunk) for j in subs]

    blocks = []
    b = chunk // 2
    while b >= 1:
        blocks.append(b)
        b //= 2
    sel = [row >= col]
    for b in blocks:
        sel.append(col <= (row // b) * b)
        sel.append(col <= jnp.minimum((row // b + 1) * b, chunk - 1))
    la_wide = jnp.concatenate([la_ref[r, :] for r in rows], axis=1)
    g_wide = _dot_exact_lhs(jnp.concatenate([m.astype(F32) for m in sel], axis=0), la_wide)

    heads = range(GLA_HEADS)
    dks = [slice(h * GLA_DK, (h + 1) * GLA_DK) for h in heads]
    dvs = [slice(h * GLA_DV, (h + 1) * GLA_DV) for h in heads]
    prep = []
    for j in subs:
        g_all = g_wide[:, j * GLA_QK:(j + 1) * GLA_QK]
        q = q_ref[rows[j], :]
        k = k_ref[rows[j], :]
        g = g_all[0:chunk]
        levels = []
        for li, b in enumerate(blocks):
            ref_r = g_all[(1 + 2 * li) * chunk:(2 + 2 * li) * chunk]
            ref_c = g_all[(2 + 2 * li) * chunk:(3 + 2 * li) * chunk]
            mask = (row // (2 * b) == col // (2 * b)) & (row // b == col // b + 1)
            levels.append((q * jnp.exp(g - ref_r), k * jnp.exp(ref_c - g), mask))
        g_last = g[chunk - 1:chunk, :]
        prep.append(dict(levels=levels, qk_diag=q * k, q_in=q * jnp.exp(g), k_out=k * jnp.exp(g_last - g),
                         decay_out=jnp.exp(g_last)))
    atts = [[jnp.zeros((chunk, chunk), F32) for _ in heads] for _ in subs]
    for li in range(len(blocks)):
        for j in subs:
            ql, kl, mask = prep[j]["levels"][li]
            atts[j] = [att + jnp.where(mask, _dot3_nt(ql[:, dk], kl[:, dk]), 0.0)
                       for att, dk in zip(atts[j], dks)]
    vhs = [[v_ref[rows[j], dv] for dv in dvs] for j in subs]
    intra = [[_dot3(atts[j][h], vhs[j][h])
              + jnp.sum(prep[j]["qk_diag"][:, dks[h]], axis=-1, keepdims=True) * vhs[j][h]
              for h in heads] for j in subs]
    kvs = [[_dot3_tn(vhs[j][h], prep[j]["k_out"][:, dks[h]]) for h in heads] for j in subs]
    sts = [st_ref[h] for h in heads]
    for j in subs:
        os_ = [_dot3_nt(prep[j]["q_in"][:, dks[h]], sts[h]) + intra[j][h] for h in heads]
        sts = [sts[h] * prep[j]["decay_out"][:, dks[h]] + kvs[j][h] for h in heads]
        for h in heads:
            o = os_[h]
            ms = jnp.mean(o * o, axis=-1, keepdims=True)
            on = o * lax.rsqrt(ms + NORM_EPS) * nw_ref[...]
            gate = gg_ref[rows[j], dvs[h]]
            o_ref[rows[j], dvs[h]] = on * (gate * _sigmoid(gate))
    for h in heads:
        st_ref[h] = sts[h]

    @pl.when(c == nc - 1)
    def _():
        sfin_ref[0] = st_ref[...]


def _chunks_per_step(seq_len, chunk):
    return 2 if seq_len % (2 * chunk) == 0 else 1


def _gla(gq, gk, gv, la, gg, s0t, nw, n_seq, seq_len, chunk):
    n_sub = _chunks_per_step(seq_len, chunk)
    nc = seq_len // (chunk * n_sub)
    t = n_seq * seq_len
    blk = lambda w: pl.BlockSpec((chunk * n_sub, w), lambda b, c: (b * nc + c, 0))
    st_spec = pl.BlockSpec((1, GLA_HEADS, GLA_DV, GLA_DK), lambda b, c: (b, 0, 0, 0))
    return pl.pallas_call(
        functools.partial(_gla_kernel, chunk=chunk, n_sub=n_sub),
        out_shape=[jax.ShapeDtypeStruct((t, GLA_V), F32),
                   jax.ShapeDtypeStruct((n_seq, GLA_HEADS, GLA_DV, GLA_DK), F32)],
        grid=(n_seq, nc),
        in_specs=[blk(GLA_QK), blk(GLA_QK), blk(GLA_V), blk(GLA_QK), blk(GLA_V), st_spec,
                  pl.BlockSpec((1, GLA_DV), lambda b, c: (0, 0))],
        out_specs=[blk(GLA_V), st_spec],
        scratch_shapes=[pltpu.VMEM((GLA_HEADS, GLA_DV, GLA_DK), F32)],
        compiler_params=_cparams(("parallel", "arbitrary"), 32),
        name="gla_scan",
    )(gq, gk, gv, la, gg, s0t, nw)


_INV_BASE = 16


def _unit_lower_inverses(lows, row, col, n):
    eye = (row == col).astype(F32)
    base = min(_INV_BASE, n)
    diag_blk = row // base == col // base
    pws = [jnp.where(diag_blk, low, 0.0) for low in lows]
    invs = [eye - ld for ld in pws]
    p = 2
    while p < base:
        pws = [_dot3(pw, pw) for pw in pws]
        invs = [inv + _dot3(inv, pw) for inv, pw in zip(invs, pws)]
        p *= 2
    b = base
    while b < n:
        sib = (row // (2 * b) == col // (2 * b)) & (row // b != col // b)
        tmps = [_dot3(jnp.where(sib, low, 0.0), inv) for low, inv in zip(lows, invs)]
        invs = [inv - _dot3(inv, t) for inv, t in zip(invs, tmps)]
        b *= 2
    return invs


def _gdn_kernel(x_ref, ab_ref, dz_ref, tail0_ref, cw_ref, s0_ref, nw_ref,
                o_ref, sfin_ref, tail_out_ref, st_ref, tail_ref, *, chunk, n_sub):
    c = pl.program_id(1)
    nc = pl.num_programs(1)

    @pl.when(c == 0)
    def _():
        st_ref[...] = s0_ref[0]
        tail_ref[...] = tail0_ref[0]

    n_rows = chunk * n_sub
    x = x_ref[...]
    ext = jnp.concatenate([tail_ref[...], x], axis=0)
    base = SUBLANES - (CONV_W - 1)
    conv = ext[base:base + n_rows] * cw_ref[0:1, :]
    for i in range(1, CONV_W):
        conv = conv + ext[base + i:base + i + n_rows] * cw_ref[i:i + 1, :]
    tail_ref[...] = x[n_rows - SUBLANES:n_rows]
    conv = conv * _sigmoid(conv)

    row = _iota2(chunk, chunk, 0)
    col = _iota2(chunk, chunk, 1)
    lower = row >= col
    tril = lower.astype(F32)

    heads = range(GDN_HEADS)
    subs = range(n_sub)
    qs, ks, vs, gcs, betas, decs = [], [], [], [], [], []
    for j in subs:
        rows = slice(j * chunk, (j + 1) * chunk)
        ab = ab_ref[rows, :]
        g_col = _dot_exact_lhs(tril, ab)
        g_row = _dot_nt_exact_rhs(ab.T, tril)
        for h in heads:
            cq = conv[rows, h * GDN_DK:(h + 1) * GDN_DK]
            ck = conv[rows, GDN_QK + h * GDN_DK:GDN_QK + (h + 1) * GDN_DK]
            vs.append(conv[rows, 2 * GDN_QK + h * GDN_DV:2 * GDN_QK + (h + 1) * GDN_DV])
            qs.append(cq * lax.rsqrt(jnp.sum(cq * cq, axis=-1, keepdims=True) + NORM_EPS) * (GDN_DK ** -0.5))
            ks.append(ck * lax.rsqrt(jnp.sum(ck * ck, axis=-1, keepdims=True) + NORM_EPS))
            gc = g_col[:, SM_A0 + h:SM_A0 + h + 1]
            gr = g_row[SM_A0 + h:SM_A0 + h + 1, :]
            gcs.append(gc)
            betas.append(ab[:, SM_B0 + h:SM_B0 + h + 1])
            decs.append(jnp.where(lower, jnp.exp(jnp.where(lower, gc - gr, 0.0)), 0.0))
    kks = [_dot3_nt(k, k) for k in ks]
    qks = [_dot3_nt(q, k) * dec for q, k, dec in zip(qs, ks, decs)]
    lows = [jnp.where(row > col, beta * kk * dec, 0.0) for beta, kk, dec in zip(betas, kks, decs)]
    tinvs = _unit_lower_inverses(lows, row, col, chunk)
    egs = [jnp.exp(gc) for gc in gcs]
    uws = [_dot3(tinv, jnp.concatenate([v * beta, k * (beta * eg)], axis=1))
           for tinv, v, k, beta, eg in zip(tinvs, vs, ks, betas, egs)]
    sts = [st_ref[h] for h in heads]
    for j in subs:
        rows = slice(j * chunk, (j + 1) * chunk)
        ids = [j * GDN_HEADS + h for h in heads]
        v_news = [uws[i][:, :GDN_DV] - _dot3(uws[i][:, GDN_DV:], st) for i, st in zip(ids, sts)]
        os_ = [_dot3(qs[i] * egs[i], st) + _dot3(qks[i], v_new) for i, st, v_new in zip(ids, sts, v_news)]
        new_sts = []
        for h, i in enumerate(ids):
            g_last = gcs[i][chunk - 1:chunk, :]
            new_sts.append(sts[h] * jnp.exp(g_last) + _dot3_tn(ks[i] * jnp.exp(g_last - gcs[i]), v_news[h]))
        sts = new_sts
        for h in heads:
            o = os_[h]
            ms = jnp.mean(o * o, axis=-1, keepdims=True)
            on = o * lax.rsqrt(ms + NORM_EPS) * nw_ref[...]
            gate = dz_ref[rows, h * GDN_DV:(h + 1) * GDN_DV]
            o_ref[rows, h * GDN_DV:(h + 1) * GDN_DV] = on * (gate * _sigmoid(gate))
    for h in heads:
        st_ref[h] = sts[h]

    @pl.when(c == nc - 1)
    def _():
        sfin_ref[0] = st_ref[...]
        tail_out_ref[0] = tail_ref[...]


def _gdn(dqkv, ab, dz, tail0, cw, s0, nw, n_seq, seq_len, chunk):
    n_sub = _chunks_per_step(seq_len, chunk)
    nc = seq_len // (chunk * n_sub)
    t = n_seq * seq_len
    blk = lambda w: pl.BlockSpec((chunk * n_sub, w), lambda b, c: (b * nc + c, 0))
    st_spec = pl.BlockSpec((1, GDN_HEADS, GDN_DK, GDN_DV), lambda b, c: (b, 0, 0, 0))
    tail_spec = pl.BlockSpec((1, SUBLANES, GDN_QKV), lambda b, c: (b, 0, 0))
    return pl.pallas_call(
        functools.partial(_gdn_kernel, chunk=chunk, n_sub=n_sub),
        out_shape=[jax.ShapeDtypeStruct((t, GDN_V), F32),
                   jax.ShapeDtypeStruct((n_seq, GDN_HEADS, GDN_DK, GDN_DV), F32),
                   jax.ShapeDtypeStruct((n_seq, SUBLANES, GDN_QKV), F32)],
        grid=(n_seq, nc),
        in_specs=[blk(GDN_QKV), blk(LANES), blk(GDN_V), tail_spec,
                  pl.BlockSpec((SUBLANES, GDN_QKV), lambda b, c: (0, 0)), st_spec,
                  pl.BlockSpec((1, GDN_DV), lambda b, c: (0, 0))],
        out_specs=[blk(GDN_V), st_spec, tail_spec],
        scratch_shapes=[pltpu.VMEM((GDN_HEADS, GDN_DK, GDN_DV), F32),
                        pltpu.VMEM((SUBLANES, GDN_QKV), F32)],
        compiler_params=_cparams(("parallel", "arbitrary"), 32),
        name="gdn_scan",
    )(dqkv, ab, dz, tail0, cw, s0, nw)


def _topk_cols(s, k, payload=None):
    n = s.shape[0]
    rid = lax.broadcasted_iota(jnp.int32, s.shape, 0).astype(F32)
    vals, idxs = [], []
    for _ in range(k):
        m = jnp.max(s, axis=0, keepdims=True)
        am = jnp.min(jnp.where(s == m, rid, float(n)), axis=0, keepdims=True)
        sel = rid == am
        vals.append(m)
        if payload is None:
            idxs.append(am)
        else:
            idxs.append(jnp.sum(jnp.where(sel, payload, 0.0), axis=0, keepdims=True))
        s = jnp.where(sel, -jnp.inf, s)
    return jnp.concatenate(vals, axis=0), jnp.concatenate(idxs, axis=0)


def _post_kernel(o1_ref, o2_ref, x_ref, wo_ref, nw_ref, wq_ref, k1_ref, k2_ref,
                 x2_ref, hn_ref, j8_ref, sh_ref, gate_ref):
    mix = (_dot(o1_ref[...].astype(BF16), wo_ref[0:GLA_V, :])
           + _dot(o2_ref[...].astype(BF16), wo_ref[GLA_V:GLA_V + GDN_V, :]))
    x2 = x_ref[...] + mix
    x2_ref[...] = x2
    ms = jnp.mean(x2 * x2, axis=-1, keepdims=True)
    hn = x2 * lax.rsqrt(ms + NORM_EPS) * nw_ref[...]
    hn_ref[...] = hn
    qry = _dot(hn.astype(BF16), wq_ref[...])
    idx_rows, gate_rows = [], []
    for h in range(PEER_HEADS):
        q1 = qry[:, h * PEER_QDIM:h * PEER_QDIM + PEER_HALF].astype(BF16)
        q2 = qry[:, h * PEER_QDIM + PEER_HALF:(h + 1) * PEER_QDIM].astype(BF16)
        s1 = _dot_nt(k1_ref[h], q1)
        s2 = _dot_nt(k2_ref[h], q2)
        v1, i1 = _topk_cols(s1, PEER_TOPK)
        v2, i2 = _topk_cols(s2, PEER_TOPK)
        cand_rows, cidx_rows = [], []
        for a in range(PEER_TOPK // 2):
            n_valid = PEER_TOPK // (a + 1)
            n_rows = -(-n_valid // SUBLANES) * SUBLANES
            c = v1[a:a + 1, :] + v2[0:n_rows, :]
            if n_valid < n_rows:
                c = jnp.where(lax.broadcasted_iota(jnp.int32, c.shape, 0) < n_valid, c, -jnp.inf)
            cand_rows.append(c)
            cidx_rows.append(i1[a:a + 1, :] * float(N_KEYS) + i2[0:n_rows, :])
        cand_rows.append(v1[PEER_TOPK // 2:, :] + v2[0:1, :])
        cidx_rows.append(i1[PEER_TOPK // 2:, :] * float(N_KEYS) + i2[0:1, :])
        sc, eidx = _topk_cols(jnp.concatenate(cand_rows, axis=0), PEER_TOPK,
                              payload=jnp.concatenate(cidx_rows, axis=0))
        e = jnp.exp(sc - jnp.max(sc, axis=0, keepdims=True))
        gate_rows.append(e / jnp.sum(e, axis=0, keepdims=True))
        idx_rows.append(eidx)
    e = jnp.concatenate(idx_rows, axis=0).T.astype(jnp.int32)
    j8_ref[...] = (e & (HALF_EXPERTS - 1)) * SUBLANES
    sh_ref[...] = jnp.where(e >= HALF_EXPERTS, 0.0, 16.0)
    gate_ref[...] = jnp.concatenate(gate_rows, axis=0).T


def _post(o1, o2, x, wo, nw, wq, k1, k2, tm):
    t, d = x.shape
    row = lambda w: pl.BlockSpec((tm, w), lambda i: (i, 0))
    full = lambda a: pl.BlockSpec(a.shape, lambda i: (0,) * a.ndim)
    return pl.pallas_call(
        _post_kernel,
        out_shape=[jax.ShapeDtypeStruct((t, d), F32), jax.ShapeDtypeStruct((t, d), F32),
                   jax.ShapeDtypeStruct((t, N_SEL), jnp.int32), jax.ShapeDtypeStruct((t, N_SEL), F32),
                   jax.ShapeDtypeStruct((t, N_SEL), F32)],
        grid=(t // tm,),
        in_specs=[row(GLA_V), row(GDN_V), row(d), full(wo), full(nw), full(wq), full(k1), full(k2)],
        out_specs=[row(d), row(d), row(N_SEL), row(N_SEL), row(N_SEL)],
        compiler_params=_cparams(("parallel",), 48),
        name="post_topk",
    )(o1, o2, x, wo, nw, wq, k1, k2)


_HI_MASK = -65536
_IDX_GROUP = 4
_IDX_SLOTS = 8


def _expert_row(tab_ref, j8, shift_row):
    words = tab_ref[pl.ds(pl.multiple_of(j8, SUBLANES), SUBLANES), :]
    bits = lax.shift_left(words, jnp.broadcast_to(shift_row, (SUBLANES, LANES))) & _HI_MASK
    return lax.bitcast_convert_type(bits, F32)


def _diag_rows(vals, eye):
    n = vals.shape[0]
    return (vals[:, None, :] * eye[None, :, :]).reshape(n * N_SEL, N_SEL)


def _add_bf16_pairs(a, b):
    return pltpu.bitcast(pltpu.bitcast(a, BF16) + pltpu.bitcast(b, BF16), jnp.int32)


def _fold_sublanes(vs, sub, add):
    m4 = sub < 4
    cur = []
    for a, b in zip(vs[0::2], vs[1::2]):
        cur.append(add(jnp.where(m4, a, b), pltpu.roll(jnp.where(m4, b, a), 4, 0)))
    for half in (2, 1):
        m = (sub % (2 * half)) < half
        nxt = []
        for a, b in zip(cur[0::2], cur[1::2]):
            ta = add(a, pltpu.roll(a, SUBLANES - half, 0))
            tb = add(b, pltpu.roll(b, half, 0))
            nxt.append(jnp.where(m, ta, tb))
        cur = nxt
    return cur[0]


def _eye():
    return (_iota2(N_SEL, N_SEL, 0) == _iota2(N_SEL, N_SEL, 1)).astype(F32)


def _peer_u_kernel(j8_ref, sh_ref, hn_ref, gate_ref, tab_ref, w_ref, q0_ref, q1_ref, q2_ref, q3_ref,
                   idx_ref, sem, *, tb):
    sub = lax.broadcasted_iota(jnp.int32, (SUBLANES, LANES), 0)
    ones = jnp.ones((LANES, LANES), BF16)
    perm = (0, 4, 2, 6, 1, 5, 3, 7)
    q_refs = (q0_ref, q1_ref, q2_ref, q3_ref)
    half_of_row = _iota2(2 * N_SEL, N_SEL, 0) - 2 * _iota2(2 * N_SEL, N_SEL, 1)

    def finish(src_ref, out_tok):
        rep = _dot(pltpu.bitcast(src_ref[...], BF16), ones)
        high = jnp.where(sh_ref[pl.ds(out_tok, 1), :] == 0.0, 1, 0)
        act = jnp.sum(jnp.where(half_of_row == high, rep, 0.0), axis=0, keepdims=True)
        gelu = 0.5 * act * (1.0 + jnp.tanh(0.7978845608028654 * (act + 0.044715 * act * act * act)))
        w_ref[pl.ds(out_tok, 1), :] = gate_ref[pl.ds(out_tok, 1), :] * gelu

    n_groups = tb // _IDX_GROUP

    def idx_copy(g, buf):
        src = j8_ref.at[pl.ds(g * _IDX_GROUP, _IDX_GROUP)]
        return pltpu.make_async_copy(src, idx_ref.at[buf], sem.at[buf])

    lag = 2
    for q_ref in q_refs[len(q_refs) - lag:]:
        q_ref[...] = jnp.zeros((N_SEL, LANES), jnp.int32)
    idx_copy(0, 0).start()

    def body(i, carry):
        for buf in range(2):
            g = 2 * i + buf
            idx_copy(g, buf).wait()

            @pl.when(g + 1 < n_groups)
            def _():
                idx_copy(g + 1, 1 - buf).start()

            for s in range(_IDX_GROUP):
                t = g * _IDX_GROUP + s
                finish(q_refs[(s - lag) % _IDX_GROUP], jnp.maximum(t - lag, 0))
                xi = lax.bitcast_convert_type(hn_ref[t].astype(BF16).astype(F32), jnp.int32)
                xx = pltpu.bitcast(xi | lax.shift_right_logical(xi, 16), BF16)
                folded = []
                for grp in range(N_SEL // SUBLANES):
                    prods = []
                    for r in range(SUBLANES):
                        k = grp * SUBLANES + perm[r]
                        row0 = pl.multiple_of(idx_ref[buf, s, 0, k], SUBLANES)
                        words = tab_ref[pl.ds(row0, SUBLANES), :]
                        prods.append(pltpu.bitcast(pltpu.bitcast(words, BF16) * xx, jnp.int32))
                    folded.append(_fold_sublanes(prods, sub, _add_bf16_pairs))
                q_refs[s][...] = jnp.concatenate(folded, axis=0)
        return carry

    lax.fori_loop(0, n_groups // 2, body, 0)
    for t in range(tb - lag, tb):
        finish(q_refs[t % _IDX_GROUP], t)


def _peer_u(j8, sh, hn3, gate, tab, tb):
    t = j8.shape[0]
    vec = lambda: pl.BlockSpec((tb, N_SEL), lambda i: (i, 0))
    return pl.pallas_call(
        functools.partial(_peer_u_kernel, tb=tb),
        out_shape=jax.ShapeDtypeStruct((t, N_SEL), F32),
        grid=(t // tb,),
        in_specs=[pl.BlockSpec((tb, 1, N_SEL), lambda i: (i, 0, 0)), vec(),
                  pl.BlockSpec((tb, SUBLANES, LANES), lambda i: (i, 0, 0)), vec(),
                  pl.BlockSpec(memory_space=pltpu.VMEM)],
        out_specs=vec(),
        scratch_shapes=[pltpu.VMEM((N_SEL, LANES), jnp.int32) for _ in range(_IDX_GROUP)]
        + [pltpu.SMEM((2, _IDX_GROUP, 1, N_SEL), jnp.int32), pltpu.SemaphoreType.DMA((2,))],
        compiler_params=_cparams(("arbitrary",), 56),
        name="peer_u",
    )(j8.reshape(t, 1, N_SEL), sh, hn3, gate, tab)


def _peer_v_kernel(j8_ref, sh_ref, wgt_ref, x2_ref, nw_ref, tab_ref, y_ref, sd_ref, wd_ref, idx_ref, sem,
                   *, tb, n_acc):
    eye = _eye()
    blk = (_iota2(2 * N_SEL, 2 * LANES, 0) // N_SEL == _iota2(2 * N_SEL, 2 * LANES, 1) // LANES).astype(BF16)
    half_toks = _IDX_SLOTS // 2
    half_rows = half_toks * N_SEL

    def pre(c, carry):
        for h in range(2):
            tok0 = pl.multiple_of(c * _IDX_SLOTS, _IDX_SLOTS) + h * half_toks
            toks = pl.ds(tok0, half_toks)
            diag = jnp.concatenate([_diag_rows(sh_ref[toks, :], eye).astype(BF16),
                                    _diag_rows(wgt_ref[toks, :], eye).astype(BF16)], axis=1)
            rep = _dot(diag, blk)
            dst = pl.ds(pl.multiple_of(tok0 * N_SEL, half_rows), half_rows)
            sd_ref[dst, :] = rep[:, :LANES].astype(jnp.int32)
            wd_ref[dst, :] = rep[:, LANES:]
        return carry

    lax.fori_loop(0, tb // _IDX_SLOTS, pre, 0)

    def idx_copy(t, slot):
        return pltpu.make_async_copy(j8_ref.at[t], idx_ref.at[slot], sem.at[slot])

    for s in range(_IDX_SLOTS):
        idx_copy(s, s).start()

    def group(gi, carry):
        for s in range(_IDX_SLOTS):
            t = gi * _IDX_SLOTS + s
            idx_copy(t, s).wait()
            base = t * N_SEL
            accs = [jnp.zeros((SUBLANES, LANES), F32) for _ in range(n_acc)]
            for k in range(N_SEL):
                row = _expert_row(tab_ref, idx_ref[s, 0, k], sd_ref[pl.ds(base + k, 1), :])
                accs[k % n_acc] = accs[k % n_acc] + row * wd_ref[pl.ds(base + k, 1), :]
            tot = accs[0]
            for a in accs[1:]:
                tot = tot + a
            y_ref[t] = x2_ref[t] + tot

            @pl.when(t + _IDX_SLOTS < tb)
            def _():
                idx_copy(t + _IDX_SLOTS, s).start()
        return carry

    lax.fori_loop(0, tb // _IDX_SLOTS, group, 0)
    x3 = y_ref[...]
    ss = jnp.sum(jnp.sum(x3 * x3, axis=2, keepdims=True), axis=1, keepdims=True)
    y_ref[...] = x3 * lax.rsqrt(ss * (1.0 / (SUBLANES * LANES)) + NORM_EPS) * nw_ref[...][None]


def _peer_v(j8, sh, wgt, x23, nw3, tab, tb):
    t = j8.shape[0]
    vec = lambda: pl.BlockSpec((tb, N_SEL), lambda i: (i, 0))
    return pl.pallas_call(
        functools.partial(_peer_v_kernel, tb=tb, n_acc=4),
        out_shape=jax.ShapeDtypeStruct((t, SUBLANES, LANES), F32),
        grid=(t // tb,),
        in_specs=[pl.BlockSpec((tb, 1, N_SEL), lambda i: (i, 0, 0)), vec(), vec(),
                  pl.BlockSpec((tb, SUBLANES, LANES), lambda i: (i, 0, 0)),
                  pl.BlockSpec((SUBLANES, LANES), lambda i: (0, 0)),
                  pl.BlockSpec(memory_space=pltpu.VMEM)],
        out_specs=pl.BlockSpec((tb, SUBLANES, LANES), lambda i: (i, 0, 0)),
        scratch_shapes=[pltpu.VMEM((tb * N_SEL, LANES), jnp.int32), pltpu.VMEM((tb * N_SEL, LANES), F32),
                        pltpu.SMEM((_IDX_SLOTS, 1, N_SEL), jnp.int32),
                        pltpu.SemaphoreType.DMA((_IDX_SLOTS,))],
        compiler_params=_cparams(("arbitrary",), 56),
        name="peer_v",
    )(j8.reshape(t, 1, N_SEL), sh, wgt, x23, nw3, tab)


_PACK_ROWS = 256


def _pack_kernel(lo_ref, hi_ref, o_ref):
    lo = lax.bitcast_convert_type(lo_ref[...].astype(BF16).astype(F32), jnp.int32)
    hi = lax.bitcast_convert_type(hi_ref[...].astype(BF16).astype(F32), jnp.int32)
    packed = hi | lax.shift_right_logical(lo, 16)
    rows, d = packed.shape
    for s in range(d // LANES):
        o_ref[pl.ds(s, rows, stride=d // LANES), :] = packed[:, s * LANES:(s + 1) * LANES]


def _pack_table(tab):
    n, d = tab.shape
    sub = d // LANES
    steps = n // 2 // _PACK_ROWS
    return pl.pallas_call(
        _pack_kernel,
        out_shape=jax.ShapeDtypeStruct((n // 2 * sub, LANES), jnp.int32),
        grid=(steps,),
        in_specs=[pl.BlockSpec((_PACK_ROWS, d), lambda i: (i, 0)),
                  pl.BlockSpec((_PACK_ROWS, d), lambda i: (i + steps, 0))],
        out_specs=pl.BlockSpec((_PACK_ROWS * sub, LANES), lambda i: (i, 0)),
        compiler_params=_cparams(("arbitrary",), 32),
        name="pack_table",
    )(tab, tab)


def _prep_weights(norm_mix_w, w_in, gla_w_gk2, gla_b_gk, gla_norm_w, gdn_conv_w, gdn_a_log,
                  gdn_dt_bias, gdn_norm_w, w_out, norm_ffn_w, peer_wq, peer_k1, peer_k2, peer_u, peer_v):
    d = w_in.shape[0]
    o_glr = 2 * GLA_QK + GLA_V
    o_gg = o_glr + GLA_LR
    o_qkv = o_gg + GLA_V
    o_da = o_qkv + GDN_QKV
    o_dz = o_da + 2 * GDN_HEADS
    small = jnp.concatenate([w_in[:, o_glr:o_gg], w_in[:, o_da:o_dz],
                             jnp.zeros((d, LANES - GLA_LR - 2 * GDN_HEADS), w_in.dtype)], axis=1)
    wcat = jnp.concatenate([w_in[:, :o_glr], w_in[:, o_gg:o_qkv], w_in[:, o_qkv:o_da],
                            w_in[:, o_dz:], small], axis=1).astype(BF16)
    w2p = jnp.zeros((LANES, GLA_QK), F32).at[:GLA_LR].set(gla_w_gk2).astype(BF16)
    avec = jnp.zeros((1, LANES), F32).at[0, SM_A0:SM_B0].set(-jnp.exp(gdn_a_log))
    dtb = jnp.zeros((1, LANES), F32).at[0, SM_A0:SM_B0].set(gdn_dt_bias)
    cw = jnp.zeros((SUBLANES, GDN_QKV), F32).at[:CONV_W].set(gdn_conv_w)
    return dict(
        nmix=norm_mix_w.reshape(1, d), wcat=wcat, w2p=w2p, b2=gla_b_gk.reshape(1, GLA_QK), avec=avec, dtb=dtb,
        gla_nw=gla_norm_w.reshape(1, GLA_DV), cw=cw, gdn_nw=gdn_norm_w.reshape(1, GDN_DV),
        wo=w_out.astype(BF16), nffn=norm_ffn_w.reshape(1, d), wq=peer_wq.astype(BF16),
        k1=peer_k1.astype(BF16), k2=peer_k2.astype(BF16), tab_u=_pack_table(peer_u), tab_v=_pack_table(peer_v))


def _layer(x, s_gla, s_gdn, conv_buf, w, nfw3, chunk, tm, tb):
    n_seq, seq_len, d = x.shape
    t = n_seq * seq_len
    xf = x.reshape(t, d)
    gq, gk, gv, gg, dqkv, dz, la, sm = _inproj(xf, w["nmix"], w["wcat"], w["w2p"], w["b2"], w["avec"],
                                               w["dtb"], tm)
    o1, gla_t = _gla(gq, gk, gv, la, gg, jnp.swapaxes(s_gla, -1, -2), w["gla_nw"], n_seq, seq_len, chunk)
    tail0 = jnp.pad(conv_buf, ((0, 0), (SUBLANES - (CONV_W - 1), 0), (0, 0)))
    o2, gdn_new, tail = _gdn(dqkv, sm, dz, tail0, w["cw"], s_gdn, w["gdn_nw"], n_seq, seq_len, chunk)
    x2, hn, j8, sh, gate = _post(o1, o2, xf, w["wo"], w["nffn"], w["wq"], w["k1"], w["k2"], tm)
    wgt = _peer_u(j8, sh, hn.reshape(t, SUBLANES, LANES), gate, w["tab_u"], tb)
    y = _peer_v(j8, sh, wgt, x2.reshape(t, SUBLANES, LANES), nfw3, w["tab_v"], tb)
    return (y.reshape(n_seq, seq_len, d), jnp.swapaxes(gla_t, -1, -2), gdn_new,
            tail[:, SUBLANES - (CONV_W - 1):, :])


def kernel(x_prompt, x_sample, state_gla, state_gdn, state_gdn_conv, norm_mix_w, w_in, gla_w_gk2, gla_b_gk,
           gla_norm_w, gdn_conv_w, gdn_a_log, gdn_dt_bias, gdn_norm_w, w_out, norm_ffn_w, peer_wq, peer_k1,
           peer_k2, peer_u, peer_v, norm_final_w):
    depth = w_in.shape[0]
    assert depth == 1, "the final norm is fused into the last layer's PEER pass"
    n_p, l_p, d = x_prompt.shape
    n_s, l_s, _ = x_sample.shape
    nfw3 = norm_final_w.reshape(SUBLANES, LANES)
    w = _prep_weights(norm_mix_w[0], w_in[0], gla_w_gk2[0], gla_b_gk[0], gla_norm_w[0], gdn_conv_w[0],
                      gdn_a_log[0], gdn_dt_bias[0], gdn_norm_w[0], w_out[0], norm_ffn_w[0], peer_wq[0],
                      peer_k1[0], peer_k2[0], peer_u[0], peer_v[0])
    z_gla = jnp.zeros((n_p, GLA_HEADS, GLA_DK, GLA_DV), F32)
    z_gdn = jnp.zeros((n_p, GDN_HEADS, GDN_DK, GDN_DV), F32)
    z_conv = jnp.zeros((n_p, CONV_W - 1, GDN_QKV), F32)
    chunk_p = min(CHUNK, l_p)
    chunk_s = min(CHUNK, l_s)
    tm_p = min(256, n_p * l_p)
    tm_s = min(256, n_s * l_s)
    yp, gla_p, gdn_p, conv_p = _layer(x_prompt, z_gla, z_gdn, z_conv, w, nfw3, chunk_p, tm_p, min(64, tm_p))
    ys, gla_s, gdn_s, conv_s = _layer(x_sample, state_gla[0], state_gdn[0], state_gdn_conv[0], w, nfw3,
                                      chunk_s, tm_s, min(64, tm_s))
    return (yp, ys, gla_p[None], gdn_p[None], conv_p[None], gla_s[None], gdn_s[None], conv_s[None])
```

```python
import functools

import jax
import jax.numpy as jnp
from jax import lax
from jax.experimental import pallas as pl
from jax.experimental.pallas import tpu as pltpu

F32 = jnp.float32
BF16 = jnp.bfloat16
HI = lax.Precision.HIGHEST

NORM_EPS = 1e-6
CHUNK = 64
GLA_HEADS, GLA_DK, GLA_DV, GLA_LR, GLA_GATE_TAU = 4, 64, 128, 16, 16.0
GDN_HEADS, GDN_DK, GDN_DV, CONV_W = 4, 128, 128, 4
GLA_QK, GLA_V = GLA_HEADS * GLA_DK, GLA_HEADS * GLA_DV
GDN_QK, GDN_V = GDN_HEADS * GDN_DK, GDN_HEADS * GDN_DV
GDN_QKV = 2 * GDN_QK + GDN_V
PEER_HEADS, PEER_QDIM, N_KEYS, PEER_TOPK = 8, 256, 128, 16
PEER_HALF = PEER_QDIM // 2
N_SEL = PEER_HEADS * PEER_TOPK
HALF_EXPERTS = N_KEYS * N_KEYS // 2

LANES = 128
SUBLANES = 8
MIB = 1024 * 1024

SM_A0 = GLA_LR
SM_B0 = GLA_LR + GDN_HEADS


def _cparams(sem, vmem_mib):
    return pltpu.CompilerParams(dimension_semantics=sem, vmem_limit_bytes=vmem_mib * MIB)


def _softplus(x):
    return jnp.maximum(x, 0.0) + jnp.log(1.0 + jnp.exp(-jnp.abs(x)))


def _sigmoid(x):
    return 1.0 / (1.0 + jnp.exp(-x))


def _dot(a, b, prec=None):
    return jnp.dot(a, b, preferred_element_type=F32, precision=prec)


def _dot_nt(a, b, prec=None):
    return lax.dot_general(a, b, (((1,), (1,)), ((), ())), preferred_element_type=F32, precision=prec)


def _dot_tn(a, b, prec=None):
    return lax.dot_general(a, b, (((0,), (0,)), ((), ())), preferred_element_type=F32, precision=prec)


def _split_bf16(x):
    hi = x.astype(BF16)
    return hi, (x - hi.astype(F32)).astype(BF16)


def _split3_bf16(x):
    hi = x.astype(BF16)
    r = x - hi.astype(F32)
    mid = r.astype(BF16)
    return hi, mid, (r - mid.astype(F32)).astype(BF16)


def _dot3(a, b):
    ah, al = _split_bf16(a)
    bh, bl = _split_bf16(b)
    return _dot(jnp.concatenate([ah, ah, al], axis=1), jnp.concatenate([bh, bl, bh], axis=0))


def _dot3_nt(a, b):
    ah, al = _split_bf16(a)
    bh, bl = _split_bf16(b)
    return _dot_nt(jnp.concatenate([ah, ah, al], axis=1), jnp.concatenate([bh, bl, bh], axis=1))


def _dot3_tn(a, b):
    ah, al = _split_bf16(a)
    bh, bl = _split_bf16(b)
    return _dot_tn(jnp.concatenate([ah, ah, al], axis=0), jnp.concatenate([bh, bl, bh], axis=0))


def _dot_exact_lhs(a01, b):
    a16 = a01.astype(BF16)
    return _dot(jnp.concatenate([a16, a16, a16], axis=1), jnp.concatenate(_split3_bf16(b), axis=0))


def _dot_nt_exact_rhs(a, b01):
    b16 = b01.astype(BF16)
    return _dot_nt(jnp.concatenate(_split3_bf16(a), axis=1), jnp.concatenate([b16, b16, b16], axis=1))


_W_COLS = (("gq", GLA_QK), ("gk", GLA_QK), ("gv", GLA_V), ("gg", GLA_V), ("dqkv", GDN_QKV),
           ("dz", GDN_V), ("sm", LANES))


def _inproj_kernel(x_ref, nw_ref, w_ref, w2_ref, b2_ref, avec_ref, dtb_ref,
                   gq_ref, gk_ref, gv_ref, gg_ref, dqkv_ref, dz_ref, la_ref, sm_ref):
    x = x_ref[...]
    ms = jnp.mean(x * x, axis=-1, keepdims=True)
    h = (x * lax.rsqrt(ms + NORM_EPS) * nw_ref[...]).astype(BF16)
    outs = dict(gq=gq_ref, gk=gk_ref, gv=gv_ref, gg=gg_ref, dqkv=dqkv_ref, dz=dz_ref)
    off = 0
    ps = None
    for name, width in _W_COLS:
        p = _dot(h, w_ref[:, off:off + width])
        off += width
        if name == "sm":
            ps = p
        elif name == "gq":
            gq_ref[...] = p * (GLA_DK ** -0.5)
        else:
            outs[name][...] = p
    z = _dot(ps.astype(BF16), w2_ref[...]) + b2_ref[...]
    la_ref[...] = -_softplus(-z) * (1.0 / GLA_GATE_TAU)
    lane = lax.broadcasted_iota(jnp.int32, ps.shape, 1)
    log_a = avec_ref[...] * _softplus(ps + dtb_ref[...])
    beta = _sigmoid(ps)
    is_a = (lane >= SM_A0) & (lane < SM_B0)
    is_b = (lane >= SM_B0) & (lane < SM_B0 + GDN_HEADS)
    sm_ref[...] = jnp.where(is_a, log_a, jnp.where(is_b, beta, ps))


def _inproj(x, nw, wcat, w2p, b2, avec, dtb, tm):
    t, d = x.shape
    n_w = wcat.shape[1]
    widths = dict(_W_COLS)
    names = ("gq", "gk", "gv", "gg", "dqkv", "dz")
    out_shape = [jax.ShapeDtypeStruct((t, widths[n]), F32) for n in names]
    out_shape += [jax.ShapeDtypeStruct((t, GLA_QK), F32), jax.ShapeDtypeStruct((t, LANES), F32)]
    row = lambda w: pl.BlockSpec((tm, w), lambda i: (i, 0))
    full = lambda a: pl.BlockSpec(a.shape, lambda i: (0,) * a.ndim)
    return pl.pallas_call(
        _inproj_kernel,
        out_shape=out_shape,
        grid=(t // tm,),
        in_specs=[row(d), full(nw), full(wcat), full(w2p), full(b2), full(avec), full(dtb)],
        out_specs=[row(widths[n]) for n in names] + [row(GLA_QK), row(LANES)],
        compiler_params=_cparams(("parallel",), 48),
        name="inproj",
    )(x, nw, wcat, w2p, b2, avec, dtb)


def _iota2(n, m, axis):
    return lax.broadcasted_iota(jnp.int32, (n, m), axis)


def _gla_kernel(q_ref, k_ref, v_ref, la_ref, gg_ref, s0_ref, nw_ref, o_ref, sfin_ref, st_ref, *, chunk, n_sub):
    c = pl.program_id(1)
    nc = pl.num_programs(1)

    @pl.when(c == 0)
    def _():
        st_ref[...] = s0_ref[0]

    row = _iota2(chunk, chunk, 0)
    col = _iota2(chunk, chunk, 1)
    subs = range(n_sub)
    rows = [slice(j * chunk, (j + 1) * chunk) for j in subs]

    blocks = []
    b = chunk // 2
    while b >= 1:
        blocks.append(b)
        b //= 2
    sel = [row >= col]
    for b in blocks:
        sel.append(col <= (row // b) * b)
        sel.append(col <= jnp.minimum((row // b + 1) * b, chunk - 1))
    la_wide = jnp.concatenate([la_ref[r, :] for r in rows], axis=1)
    g_wide = _dot_exact_lhs(jnp.concatenate([m.astype(F32) for m in sel], axis=0), la_wide)

    heads = range(GLA_HEADS)
    dks = [slice(h * GLA_DK, (h + 1) * GLA_DK) for h in heads]
    dvs = [slice(h * GLA_DV, (h + 1) * GLA_DV) for h in heads]
    prep = []
    for j in subs:
        g_all = g_wide[:, j * GLA_QK:(j + 1) * GLA_QK]
        q = q_ref[rows[j], :]
        k = k_ref[rows[j], :]
        g = g_all[0:chunk]
        levels = []
        for li, b in enumerate(blocks):
            ref_r = g_all[(1 + 2 * li) * chunk:(2 + 2 * li) * chunk]
            ref_c = g_all[(2 + 2 * li) * chunk:(3 + 2 * li) * chunk]
            mask = (row // (2 * b) == col // (2 * b)) & (row // b == col // b + 1)
            levels.append((q * jnp.exp(g - ref_r), k * jnp.exp(ref_c - g), mask))
        g_last = g[chunk - 1:chunk, :]
        prep.append(dict(levels=levels, qk_diag=q * k, q_in=q * jnp.exp(g), k_out=k * jnp.exp(g_last - g),
                         decay_out=jnp.exp(g_last)))
    atts = [[jnp.zeros((chunk, chunk), F32) for _ in heads] for _ in subs]
    for li in range(len(blocks)):
        for j in subs:
            ql, kl, mask = prep[j]["levels"][li]
            atts[j] = [att + jnp.where(mask, _dot3_nt(ql[:, dk], kl[:, dk]), 0.0)
                       for att, dk in zip(atts[j], dks)]
    vhs = [[v_ref[rows[j], dv] for dv in dvs] for j in subs]
    intra = [[_dot3(atts[j][h], vhs[j][h])
              + jnp.sum(prep[j]["qk_diag"][:, dks[h]], axis=-1, keepdims=True) * vhs[j][h]
              for h in heads] for j in subs]
    kvs = [[_dot3_tn(vhs[j][h], prep[j]["k_out"][:, dks[h]]) for h in heads] for j in subs]
    sts = [st_ref[h] for h in heads]
    for j in subs:
        os_ = [_dot3_nt(prep[j]["q_in"][:, dks[h]], sts[h]) + intra[j][h] for h in heads]
        sts = [sts[h] * prep[j]["decay_out"][:, dks[h]] + kvs[j][h] for h in heads]
        for h in heads:
            o = os_[h]
            ms = jnp.mean(o * o, axis=-1, keepdims=True)
            on = o * lax.rsqrt(ms + NORM_EPS) * nw_ref[...]
            gate = gg_ref[rows[j], dvs[h]]
            o_ref[rows[j], dvs[h]] = on * (gate * _sigmoid(gate))
    for h in heads:
        st_ref[h] = sts[h]

    @pl.when(c == nc - 1)
    def _():
        sfin_ref[0] = st_ref[...]


def _chunks_per_step(seq_len, chunk):
    return 2 if seq_len % (2 * chunk) == 0 else 1


def _gla(gq, gk, gv, la, gg, s0t, nw, n_seq, seq_len, chunk):
    n_sub = _chunks_per_step(seq_len, chunk)
    nc = seq_len // (chunk * n_sub)
    t = n_seq * seq_len
    blk = lambda w: pl.BlockSpec((chunk * n_sub, w), lambda b, c: (b * nc + c, 0))
    st_spec = pl.BlockSpec((1, GLA_HEADS, GLA_DV, GLA_DK), lambda b, c: (b, 0, 0, 0))
    return pl.pallas_call(
        functools.partial(_gla_kernel, chunk=chunk, n_sub=n_sub),
        out_shape=[jax.ShapeDtypeStruct((t, GLA_V), F32),
                   jax.ShapeDtypeStruct((n_seq, GLA_HEADS, GLA_DV, GLA_DK), F32)],
        grid=(n_seq, nc),
        in_specs=[blk(GLA_QK), blk(GLA_QK), blk(GLA_V), blk(GLA_QK), blk(GLA_V), st_spec,
                  pl.BlockSpec((1, GLA_DV), lambda b, c: (0, 0))],
        out_specs=[blk(GLA_V), st_spec],
        scratch_shapes=[pltpu.VMEM((GLA_HEADS, GLA_DV, GLA_DK), F32)],
        compiler_params=_cparams(("parallel", "arbitrary"), 32),
        name="gla_scan",
    )(gq, gk, gv, la, gg, s0t, nw)


_INV_BASE = 16


def _unit_lower_inverses(lows, row, col, n):
    eye = (row == col).astype(F32)
    base = min(_INV_BASE, n)
    diag_blk = row // base == col // base
    pws = [jnp.where(diag_blk, low, 0.0) for low in lows]
    invs = [eye - ld for ld in pws]
    p = 2
    while p < base:
        pws = [_dot3(pw, pw) for pw in pws]
        invs = [inv + _dot3(inv, pw) for inv, pw in zip(invs, pws)]
        p *= 2
    b = base
    while b < n:
        sib = (row // (2 * b) == col // (2 * b)) & (row // b != col // b)
        tmps = [_dot3(jnp.where(sib, low, 0.0), inv) for low, inv in zip(lows, invs)]
        invs = [inv - _dot3(inv, t) for inv, t in zip(invs, tmps)]
        b *= 2
    return invs


def _gdn_kernel(x_ref, ab_ref, dz_ref, tail0_ref, cw_ref, s0_ref, nw_ref,
                o_ref, sfin_ref, tail_out_ref, st_ref, tail_ref, *, chunk, n_sub):
    c = pl.program_id(1)
    nc = pl.num_programs(1)

    @pl.when(c == 0)
    def _():
        st_ref[...] = s0_ref[0]
        tail_ref[...] = tail0_ref[0]

    n_rows = chunk * n_sub
    x = x_ref[...]
    ext = jnp.concatenate([tail_ref[...], x], axis=0)
    base = SUBLANES - (CONV_W - 1)
    conv = ext[base:base + n_rows] * cw_ref[0:1, :]
    for i in range(1, CONV_W):
        conv = conv + ext[base + i:base + i + n_rows] * cw_ref[i:i + 1, :]
    tail_ref[...] = x[n_rows - SUBLANES:n_rows]
    conv = conv * _sigmoid(conv)

    row = _iota2(chunk, chunk, 0)
    col = _iota2(chunk, chunk, 1)
    lower = row >= col
    tril = lower.astype(F32)

    heads = range(GDN_HEADS)
    subs = range(n_sub)
    qs, ks, vs, gcs, betas, decs = [], [], [], [], [], []
    for j in subs:
        rows = slice(j * chunk, (j + 1) * chunk)
        ab = ab_ref[rows, :]
        g_col = _dot_exact_lhs(tril, ab)
        g_row = _dot_nt_exact_rhs(ab.T, tril)
        for h in heads:
            cq = conv[rows, h * GDN_DK:(h + 1) * GDN_DK]
            ck = conv[rows, GDN_QK + h * GDN_DK:GDN_QK + (h + 1) * GDN_DK]
            vs.append(conv[rows, 2 * GDN_QK + h * GDN_DV:2 * GDN_QK + (h + 1) * GDN_DV])
            qs.append(cq * lax.rsqrt(jnp.sum(cq * cq, axis=-1, keepdims=True) + NORM_EPS) * (GDN_DK ** -0.5))
            ks.append(ck * lax.rsqrt(jnp.sum(ck * ck, axis=-1, keepdims=True) + NORM_EPS))
            gc = g_col[:, SM_A0 + h:SM_A0 + h + 1]
            gr = g_row[SM_A0 + h:SM_A0 + h + 1, :]
            gcs.append(gc)
            betas.append(ab[:, SM_B0 + h:SM_B0 + h + 1])
            decs.append(jnp.where(lower, jnp.exp(jnp.where(lower, gc - gr, 0.0)), 0.0))
    kks = [_dot3_nt(k, k) for k in ks]
    qks = [_dot3_nt(q, k) * dec for q, k, dec in zip(qs, ks, decs)]
    lows = [jnp.where(row > col, beta * kk * dec, 0.0) for beta, kk, dec in zip(betas, kks, decs)]
    tinvs = _unit_lower_inverses(lows, row, col, chunk)
    egs = [jnp.exp(gc) for gc in gcs]
    uws = [_dot3(tinv, jnp.concatenate([v * beta, k * (beta * eg)], axis=1))
           for tinv, v, k, beta, eg in zip(tinvs, vs, ks, betas, egs)]
    sts = [st_ref[h] for h in heads]
    for j in subs:
        rows = slice(j * chunk, (j + 1) * chunk)
        ids = [j * GDN_HEADS + h for h in heads]
        v_news = [uws[i][:, :GDN_DV] - _dot3(uws[i][:, GDN_DV:], st) for i, st in zip(ids, sts)]
        os_ = [_dot3(qs[i] * egs[i], st) + _dot3(qks[i], v_new) for i, st, v_new in zip(ids, sts, v_news)]
        new_sts = []
        for h, i in enumerate(ids):
            g_last = gcs[i][chunk - 1:chunk, :]
            new_sts.append(sts[h] * jnp.exp(g_last) + _dot3_tn(ks[i] * jnp.exp(g_last - gcs[i]), v_news[h]))
        sts = new_sts
        for h in heads:
            o = os_[h]
            ms = jnp.mean(o * o, axis=-1, keepdims=True)
            on = o * lax.rsqrt(ms + NORM_EPS) * nw_ref[...]
            gate = dz_ref[rows, h * GDN_DV:(h + 1) * GDN_DV]
            o_ref[rows, h * GDN_DV:(h + 1) * GDN_DV] = on * (gate * _sigmoid(gate))
    for h in heads:
        st_ref[h] = sts[h]

    @pl.when(c == nc - 1)
    def _():
        sfin_ref[0] = st_ref[...]
        tail_out_ref[0] = tail_ref[...]


def _gdn(dqkv, ab, dz, tail0, cw, s0, nw, n_seq, seq_len, chunk):
    n_sub = _chunks_per_step(seq_len, chunk)
    nc = seq_len // (chunk * n_sub)
    t = n_seq * seq_len
    blk = lambda w: pl.BlockSpec((chunk * n_sub, w), lambda b, c: (b * nc + c, 0))
    st_spec = pl.BlockSpec((1, GDN_HEADS, GDN_DK, GDN_DV), lambda b, c: (b, 0, 0, 0))
    tail_spec = pl.BlockSpec((1, SUBLANES, GDN_QKV), lambda b, c: (b, 0, 0))
    return pl.pallas_call(
        functools.partial(_gdn_kernel, chunk=chunk, n_sub=n_sub),
        out_shape=[jax.ShapeDtypeStruct((t, GDN_V), F32),
                   jax.ShapeDtypeStruct((n_seq, GDN_HEADS, GDN_DK, GDN_DV), F32),
                   jax.ShapeDtypeStruct((n_seq, SUBLANES, GDN_QKV), F32)],
        grid=(n_seq, nc),
        in_specs=[blk(GDN_QKV), blk(LANES), blk(GDN_V), tail_spec,
                  pl.BlockSpec((SUBLANES, GDN_QKV), lambda b, c: (0, 0)), st_spec,
                  pl.BlockSpec((1, GDN_DV), lambda b, c: (0, 0))],
        out_specs=[blk(GDN_V), st_spec, tail_spec],
        scratch_shapes=[pltpu.VMEM((GDN_HEADS, GDN_DK, GDN_DV), F32),
                        pltpu.VMEM((SUBLANES, GDN_QKV), F32)],
        compiler_params=_cparams(("parallel", "arbitrary"), 32),
        name="gdn_scan",
    )(dqkv, ab, dz, tail0, cw, s0, nw)


def _topk_cols(s, k, payload=None):
    n = s.shape[0]
    rid = lax.broadcasted_iota(jnp.int32, s.shape, 0).astype(F32)
    vals, idxs = [], []
    for _ in range(k):
        m = jnp.max(s, axis=0, keepdims=True)
        am = jnp.min(jnp.where(s == m, rid, float(n)), axis=0, keepdims=True)
        sel = rid == am
        vals.append(m)
        if payload is None:
            idxs.append(am)
        else:
            idxs.append(jnp.sum(jnp.where(sel, payload, 0.0), axis=0, keepdims=True))
        s = jnp.where(sel, -jnp.inf, s)
    return jnp.concatenate(vals, axis=0), jnp.concatenate(idxs, axis=0)


def _retrieve(s1, s2):
    v1, i1 = _topk_cols(s1, PEER_TOPK)
    v2, i2 = _topk_cols(s2, PEER_TOPK)
    cand_rows, cidx_rows = [], []
    for a in range(PEER_TOPK // 2):
        n_valid = PEER_TOPK // (a + 1)
        n_rows = -(-n_valid // SUBLANES) * SUBLANES
        c = v1[a:a + 1, :] + v2[0:n_rows, :]
        if n_valid < n_rows:
            c = jnp.where(lax.broadcasted_iota(jnp.int32, c.shape, 0) < n_valid, c, -jnp.inf)
        cand_rows.append(c)
        cidx_rows.append(i1[a:a + 1, :] * float(N_KEYS) + i2[0:n_rows, :])
    cand_rows.append(v1[PEER_TOPK // 2:, :] + v2[0:1, :])
    cidx_rows.append(i1[PEER_TOPK // 2:, :] * float(N_KEYS) + i2[0:1, :])
    sc, eidx = _topk_cols(jnp.concatenate(cand_rows, axis=0), PEER_TOPK,
                          payload=jnp.concatenate(cidx_rows, axis=0))
    e = jnp.exp(sc - jnp.max(sc, axis=0, keepdims=True))
    return eidx, e / jnp.sum(e, axis=0, keepdims=True)


def _post_kernel(o1_ref, o2_ref, x_ref, wo_ref, nw_ref, wq_ref, k1_ref, k2_ref,
                 x2_ref, hn_ref, j8_ref, sh_ref, gate_ref):
    mix = (_dot(o1_ref[...].astype(BF16), wo_ref[0:GLA_V, :])
           + _dot(o2_ref[...].astype(BF16), wo_ref[GLA_V:GLA_V + GDN_V, :]))
    x2 = x_ref[...] + mix
    x2_ref[...] = x2
    ms = jnp.mean(x2 * x2, axis=-1, keepdims=True)
    hn = x2 * lax.rsqrt(ms + NORM_EPS) * nw_ref[...]
    hn_ref[...] = hn
    qry = _dot(hn.astype(BF16), wq_ref[...])
    idx_rows, gate_rows = [], []
    for h in range(PEER_HEADS):
        q1 = qry[:, h * PEER_QDIM:h * PEER_QDIM + PEER_HALF].astype(BF16)
        q2 = qry[:, h * PEER_QDIM + PEER_HALF:(h + 1) * PEER_QDIM].astype(BF16)
        s1 = _dot_nt(k1_ref[h], q1)
        s2 = _dot_nt(k2_ref[h], q2)
        parts = [_retrieve(s1[:, c0:c0 + LANES], s2[:, c0:c0 + LANES]) for c0 in range(0, s1.shape[1], LANES)]
        idx_rows.append(jnp.concatenate([p[0] for p in parts], axis=1))
        gate_rows.append(jnp.concatenate([p[1] for p in parts], axis=1))
    e = jnp.concatenate(idx_rows, axis=0).T.astype(jnp.int32)
    j8_ref[...] = (e & (HALF_EXPERTS - 1)) * SUBLANES
    sh_ref[...] = jnp.where(e >= HALF_EXPERTS, 0.0, 16.0)
    gate_ref[...] = jnp.concatenate(gate_rows, axis=0).T


def _post(o1, o2, x, wo, nw, wq, k1, k2, tm):
    t, d = x.shape
    row = lambda w: pl.BlockSpec((tm, w), lambda i: (i, 0))
    full = lambda a: pl.BlockSpec(a.shape, lambda i: (0,) * a.ndim)
    return pl.pallas_call(
        _post_kernel,
        out_shape=[jax.ShapeDtypeStruct((t, d), F32), jax.ShapeDtypeStruct((t, d), F32),
                   jax.ShapeDtypeStruct((t, N_SEL), jnp.int32), jax.ShapeDtypeStruct((t, N_SEL), F32),
                   jax.ShapeDtypeStruct((t, N_SEL), F32)],
        grid=(t // tm,),
        in_specs=[row(GLA_V), row(GDN_V), row(d), full(wo), full(nw), full(wq), full(k1), full(k2)],
        out_specs=[row(d), row(d), row(N_SEL), row(N_SEL), row(N_SEL)],
        compiler_params=_cparams(("parallel",), 48),
        name="post_topk",
    )(o1, o2, x, wo, nw, wq, k1, k2)


_HI_MASK = -65536
_IDX_GROUP = 4


def _expert_row(tab_ref, j8, shift_row):
    words = tab_ref[pl.ds(pl.multiple_of(j8, SUBLANES), SUBLANES), :]
    bits = lax.shift_left(words, jnp.broadcast_to(shift_row, (SUBLANES, LANES))) & _HI_MASK
    return lax.bitcast_convert_type(bits, F32)


def _diag_rows(vals, eye):
    n = vals.shape[0]
    return (vals[:, None, :] * eye[None, :, :]).reshape(n * N_SEL, N_SEL)


def _add_bf16_pairs(a, b):
    return pltpu.bitcast(pltpu.bitcast(a, BF16) + pltpu.bitcast(b, BF16), jnp.int32)


def _fold_sublanes(vs, sub, add):
    m4 = sub < 4
    cur = []
    for a, b in zip(vs[0::2], vs[1::2]):
        cur.append(add(jnp.where(m4, a, b), pltpu.roll(jnp.where(m4, b, a), 4, 0)))
    for half in (2, 1):
        m = (sub % (2 * half)) < half
        nxt = []
        for a, b in zip(cur[0::2], cur[1::2]):
            ta = add(a, pltpu.roll(a, SUBLANES - half, 0))
            tb = add(b, pltpu.roll(b, half, 0))
            nxt.append(jnp.where(m, ta, tb))
        cur = nxt
    return cur[0]


def _eye():
    return (_iota2(N_SEL, N_SEL, 0) == _iota2(N_SEL, N_SEL, 1)).astype(F32)


def _peer_u_kernel(j8_ref, sh_ref, hn_ref, gate_ref, tab_ref, w_ref, q0_ref, q1_ref, q2_ref, q3_ref,
                   idx_ref, sem, *, tb):
    sub = lax.broadcasted_iota(jnp.int32, (SUBLANES, LANES), 0)
    ones = jnp.ones((LANES, LANES), BF16)
    perm = (0, 4, 2, 6, 1, 5, 3, 7)
    q_refs = (q0_ref, q1_ref, q2_ref, q3_ref)
    half_of_row = _iota2(2 * N_SEL, N_SEL, 0) - 2 * _iota2(2 * N_SEL, N_SEL, 1)

    def finish(src_ref, out_tok):
        rep = _dot(pltpu.bitcast(src_ref[...], BF16), ones)
        high = jnp.where(sh_ref[pl.ds(out_tok, 1), :] == 0.0, 1, 0)
        act = jnp.sum(jnp.where(half_of_row == high, rep, 0.0), axis=0, keepdims=True)
        gelu = 0.5 * act * (1.0 + jnp.tanh(0.7978845608028654 * (act + 0.044715 * act * act * act)))
        w_ref[pl.ds(out_tok, 1), :] = gate_ref[pl.ds(out_tok, 1), :] * gelu

    n_groups = tb // _IDX_GROUP

    def idx_copy(g, buf):
        src = j8_ref.at[pl.ds(g * _IDX_GROUP, _IDX_GROUP)]
        return pltpu.make_async_copy(src, idx_ref.at[buf], sem.at[buf])

    lag = 2
    for q_ref in q_refs[len(q_refs) - lag:]:
        q_ref[...] = jnp.zeros((N_SEL, LANES), jnp.int32)
    idx_copy(0, 0).start()

    def body(i, carry):
        for buf in range(2):
            g = 2 * i + buf
            idx_copy(g, buf).wait()

            @pl.when(g + 1 < n_groups)
            def _():
                idx_copy(g + 1, 1 - buf).start()

            for s in range(_IDX_GROUP):
                t = g * _IDX_GROUP + s
                finish(q_refs[(s - lag) % _IDX_GROUP], jnp.maximum(t - lag, 0))
                xi = lax.bitcast_convert_type(hn_ref[t].astype(BF16).astype(F32), jnp.int32)
                xx = pltpu.bitcast(xi | lax.shift_right_logical(xi, 16), BF16)
                folded = []
                for grp in range(N_SEL // SUBLANES):
                    prods = []
                    for r in range(SUBLANES):
                        k = grp * SUBLANES + perm[r]
                        row0 = pl.multiple_of(idx_ref[buf, s, 0, k], SUBLANES)
                        words = tab_ref[pl.ds(row0, SUBLANES), :]
                        prods.append(pltpu.bitcast(pltpu.bitcast(words, BF16) * xx, jnp.int32))
                    folded.append(_fold_sublanes(prods, sub, _add_bf16_pairs))
                q_refs[s][...] = jnp.concatenate(folded, axis=0)
        return carry

    lax.fori_loop(0, n_groups // 2, body, 0)
    for t in range(tb - lag, tb):
        finish(q_refs[t % _IDX_GROUP], t)


def _peer_u(j8, sh, hn3, gate, tab, tb):
    t = j8.shape[0]
    vec = lambda: pl.BlockSpec((tb, N_SEL), lambda i: (i, 0))
    return pl.pallas_call(
        functools.partial(_peer_u_kernel, tb=tb),
        out_shape=jax.ShapeDtypeStruct((t, N_SEL), F32),
        grid=(t // tb,),
        in_specs=[pl.BlockSpec((tb, 1, N_SEL), lambda i: (i, 0, 0)), vec(),
                  pl.BlockSpec((tb, SUBLANES, LANES), lambda i: (i, 0, 0)), vec(),
                  pl.BlockSpec(memory_space=pltpu.VMEM)],
        out_specs=vec(),
        scratch_shapes=[pltpu.VMEM((N_SEL, LANES), jnp.int32) for _ in range(_IDX_GROUP)]
        + [pltpu.SMEM((2, _IDX_GROUP, 1, N_SEL), jnp.int32), pltpu.SemaphoreType.DMA((2,))],
        compiler_params=_cparams(("arbitrary",), 56),
        name="peer_u",
    )(j8.reshape(t, 1, N_SEL), sh, hn3, gate, tab)


def _peer_v_kernel(j8_ref, sh_ref, wgt_ref, x2_ref, nw_ref, tab_ref, y_ref, wp_ref, idx_ref, sem,
                   *, tb, n_acc):
    eye = _eye()
    blk = (_iota2(2 * N_SEL, 2 * LANES, 0) // N_SEL == _iota2(2 * N_SEL, 2 * LANES, 1) // LANES).astype(BF16)

    def pre(c, carry):
        tok0 = pl.multiple_of(c * SUBLANES, SUBLANES)
        toks = pl.ds(tok0, SUBLANES)
        w = wgt_ref[toks, :]
        low = sh_ref[toks, :] > 0.0
        diag = jnp.concatenate([_diag_rows(jnp.where(low, w, 0.0), eye).astype(BF16),
                                _diag_rows(jnp.where(low, 0.0, w), eye).astype(BF16)], axis=1)
        rep = lax.bitcast_convert_type(_dot(diag, blk), jnp.int32)
        dst = pl.ds(pl.multiple_of(tok0 * N_SEL, SUBLANES * N_SEL), SUBLANES * N_SEL)
        wp_ref[dst, :] = rep[:, LANES:] | lax.shift_right_logical(rep[:, :LANES], 16)
        return carry

    lax.fori_loop(0, tb // SUBLANES, pre, 0)
    group4 = 4

    n_groups = tb // _IDX_GROUP

    def idx_copy(g, buf):
        src = j8_ref.at[pl.ds(g * _IDX_GROUP, _IDX_GROUP)]
        return pltpu.make_async_copy(src, idx_ref.at[buf], sem.at[buf])

    idx_copy(0, 0).start()

    def group_pair(i, carry):
        for buf in range(2):
            g = 2 * i + buf
            idx_copy(g, buf).wait()

            @pl.when(g + 1 < n_groups)
            def _():
                idx_copy(g + 1, 1 - buf).start()

            for s in range(_IDX_GROUP):
                t = g * _IDX_GROUP + s
                base = t * N_SEL
                accs = [jnp.zeros((SUBLANES, LANES), F32) for _ in range(n_acc)]
                for k0 in range(0, N_SEL, group4):
                    part = None
                    for k in range(k0, k0 + group4):
                        row0 = pl.multiple_of(idx_ref[buf, s, 0, k], SUBLANES)
                        words = pltpu.bitcast(tab_ref[pl.ds(row0, SUBLANES), :], BF16)
                        wpair = jnp.broadcast_to(wp_ref[pl.ds(base + k, 1), :], (SUBLANES, LANES))
                        prod = words * pltpu.bitcast(wpair, BF16)
                        part = prod if part is None else part + prod
                    bits = pltpu.bitcast(part, jnp.int32)
                    a = (k0 // group4) % (n_acc // 2)
                    accs[2 * a] = accs[2 * a] + lax.bitcast_convert_type(lax.shift_left(bits, 16), F32)
                    accs[2 * a + 1] = accs[2 * a + 1] + lax.bitcast_convert_type(bits & _HI_MASK, F32)
                tot = accs[0]
                for a in accs[1:]:
                    tot = tot + a
                y_ref[t] = x2_ref[t] + tot
        return carry

    lax.fori_loop(0, n_groups // 2, group_pair, 0)
    x3 = y_ref[...]
    ss = jnp.sum(jnp.sum(x3 * x3, axis=2, keepdims=True), axis=1, keepdims=True)
    y_ref[...] = x3 * lax.rsqrt(ss * (1.0 / (SUBLANES * LANES)) + NORM_EPS) * nw_ref[...][None]


def _peer_v(j8, sh, wgt, x23, nw3, tab, tb):
    t = j8.shape[0]
    vec = lambda: pl.BlockSpec((tb, N_SEL), lambda i: (i, 0))
    return pl.pallas_call(
        functools.partial(_peer_v_kernel, tb=tb, n_acc=4),
        out_shape=jax.ShapeDtypeStruct((t, SUBLANES, LANES), F32),
        grid=(t // tb,),
        in_specs=[pl.BlockSpec((tb, 1, N_SEL), lambda i: (i, 0, 0)), vec(), vec(),
                  pl.BlockSpec((tb, SUBLANES, LANES), lambda i: (i, 0, 0)),
                  pl.BlockSpec((SUBLANES, LANES), lambda i: (0, 0)),
                  pl.BlockSpec(memory_space=pltpu.VMEM)],
        out_specs=pl.BlockSpec((tb, SUBLANES, LANES), lambda i: (i, 0, 0)),
        scratch_shapes=[pltpu.VMEM((tb * N_SEL, LANES), jnp.int32),
                        pltpu.SMEM((2, _IDX_GROUP, 1, N_SEL), jnp.int32),
                        pltpu.SemaphoreType.DMA((2,))],
        compiler_params=_cparams(("arbitrary",), 56),
        name="peer_v",
    )(j8.reshape(t, 1, N_SEL), sh, wgt, x23, nw3, tab)


_PACK_ROWS = 256


def _pack_kernel(lo_ref, hi_ref, o_ref):
    lo = lax.bitcast_convert_type(lo_ref[...].astype(BF16).astype(F32), jnp.int32)
    hi = lax.bitcast_convert_type(hi_ref[...].astype(BF16).astype(F32), jnp.int32)
    packed = hi | lax.shift_right_logical(lo, 16)
    rows, d = packed.shape
    for s in range(d // LANES):
        o_ref[pl.ds(s, rows, stride=d // LANES), :] = packed[:, s * LANES:(s + 1) * LANES]


def _pack_table(tab):
    n, d = tab.shape
    sub = d // LANES
    steps = n // 2 // _PACK_ROWS
    return pl.pallas_call(
        _pack_kernel,
        out_shape=jax.ShapeDtypeStruct((n // 2 * sub, LANES), jnp.int32),
        grid=(steps,),
        in_specs=[pl.BlockSpec((_PACK_ROWS, d), lambda i: (i, 0)),
                  pl.BlockSpec((_PACK_ROWS, d), lambda i: (i + steps, 0))],
        out_specs=pl.BlockSpec((_PACK_ROWS * sub, LANES), lambda i: (i, 0)),
        compiler_params=_cparams(("arbitrary",), 32),
        name="pack_table",
    )(tab, tab)


def _prep_weights(norm_mix_w, w_in, gla_w_gk2, gla_b_gk, gla_norm_w, gdn_conv_w, gdn_a_log,
                  gdn_dt_bias, gdn_norm_w, w_out, norm_ffn_w, peer_wq, peer_k1, peer_k2, peer_u, peer_v):
    d = w_in.shape[0]
    o_glr = 2 * GLA_QK + GLA_V
    o_gg = o_glr + GLA_LR
    o_qkv = o_gg + GLA_V
    o_da = o_qkv + GDN_QKV
    o_dz = o_da + 2 * GDN_HEADS
    small = jnp.concatenate([w_in[:, o_glr:o_gg], w_in[:, o_da:o_dz],
                             jnp.zeros((d, LANES - GLA_LR - 2 * GDN_HEADS), w_in.dtype)], axis=1)
    wcat = jnp.concatenate([w_in[:, :o_glr], w_in[:, o_gg:o_qkv], w_in[:, o_qkv:o_da],
                            w_in[:, o_dz:], small], axis=1).astype(BF16)
    w2p = jnp.zeros((LANES, GLA_QK), F32).at[:GLA_LR].set(gla_w_gk2).astype(BF16)
    avec = jnp.zeros((1, LANES), F32).at[0, SM_A0:SM_B0].set(-jnp.exp(gdn_a_log))
    dtb = jnp.zeros((1, LANES), F32).at[0, SM_A0:SM_B0].set(gdn_dt_bias)
    cw = jnp.zeros((SUBLANES, GDN_QKV), F32).at[:CONV_W].set(gdn_conv_w)
    return dict(
        nmix=norm_mix_w.reshape(1, d), wcat=wcat, w2p=w2p, b2=gla_b_gk.reshape(1, GLA_QK), avec=avec, dtb=dtb,
        gla_nw=gla_norm_w.reshape(1, GLA_DV), cw=cw, gdn_nw=gdn_norm_w.reshape(1, GDN_DV),
        wo=w_out.astype(BF16), nffn=norm_ffn_w.reshape(1, d), wq=peer_wq.astype(BF16),
        k1=peer_k1.astype(BF16), k2=peer_k2.astype(BF16), tab_u=_pack_table(peer_u), tab_v=_pack_table(peer_v))


def _layer(x, s_gla, s_gdn, conv_buf, w, nfw3, chunk, tm, tb):
    n_seq, seq_len, d = x.shape
    t = n_seq * seq_len
    xf = x.reshape(t, d)
    gq, gk, gv, gg, dqkv, dz, la, sm = _inproj(xf, w["nmix"], w["wcat"], w["w2p"], w["b2"], w["avec"],
                                               w["dtb"], tm)
    o1, gla_t = _gla(gq, gk, gv, la, gg, jnp.swapaxes(s_gla, -1, -2), w["gla_nw"], n_seq, seq_len, chunk)
    tail0 = jnp.pad(conv_buf, ((0, 0), (SUBLANES - (CONV_W - 1), 0), (0, 0)))
    o2, gdn_new, tail = _gdn(dqkv, sm, dz, tail0, w["cw"], s_gdn, w["gdn_nw"], n_seq, seq_len, chunk)
    x2, hn, j8, sh, gate = _post(o1, o2, xf, w["wo"], w["nffn"], w["wq"], w["k1"], w["k2"], tm)
    wgt = _peer_u(j8, sh, hn.reshape(t, SUBLANES, LANES), gate, w["tab_u"], tb)
    y = _peer_v(j8, sh, wgt, x2.reshape(t, SUBLANES, LANES), nfw3, w["tab_v"], tb)
    return (y.reshape(n_seq, seq_len, d), jnp.swapaxes(gla_t, -1, -2), gdn_new,
            tail[:, SUBLANES - (CONV_W - 1):, :])


def kernel(x_prompt, x_sample, state_gla, state_gdn, state_gdn_conv, norm_mix_w, w_in, gla_w_gk2, gla_b_gk,
           gla_norm_w, gdn_conv_w, gdn_a_log, gdn_dt_bias, gdn_norm_w, w_out, norm_ffn_w, peer_wq, peer_k1,
           peer_k2, peer_u, peer_v, norm_final_w):
    depth = w_in.shape[0]
    assert depth == 1, "the final norm is fused into the last layer's PEER pass"
    n_p, l_p, d = x_prompt.shape
    n_s, l_s, _ = x_sample.shape
    nfw3 = norm_final_w.reshape(SUBLANES, LANES)
    w = _prep_weights(norm_mix_w[0], w_in[0], gla_w_gk2[0], gla_b_gk[0], gla_norm_w[0], gdn_conv_w[0],
                      gdn_a_log[0], gdn_dt_bias[0], gdn_norm_w[0], w_out[0], norm_ffn_w[0], peer_wq[0],
                      peer_k1[0], peer_k2[0], peer_u[0], peer_v[0])
    z_gla = jnp.zeros((n_p, GLA_HEADS, GLA_DK, GLA_DV), F32)
    z_gdn = jnp.zeros((n_p, GDN_HEADS, GDN_DK, GDN_DV), F32)
    z_conv = jnp.zeros((n_p, CONV_W - 1, GDN_QKV), F32)
    chunk_p = min(CHUNK, l_p)
    chunk_s = min(CHUNK, l_s)
    tm_p = min(256, n_p * l_p)
    tm_s = min(256, n_s * l_s)
    yp, gla_p, gdn_p, conv_p = _layer(x_prompt, z_gla, z_gdn, z_conv, w, nfw3, chunk_p, tm_p, min(64, tm_p))
    ys, gla_s, gdn_s, conv_s = _layer(x_sample, state_gla[0], state_gdn[0], state_gdn_conv[0], w, nfw3,
                                      chunk_s, tm_s, min(64, tm_s))
    return (yp, ys, gla_p[None], gdn_p[None], conv_p[None], gla_s[None], gdn_s[None], conv_s[None])
```

```python
import functools

import jax
import jax.numpy as jnp
from jax import lax
from jax.experimental import pallas as pl
from jax.experimental.pallas import tpu as pltpu

F32 = jnp.float32
BF16 = jnp.bfloat16
HI = lax.Precision.HIGHEST

NORM_EPS = 1e-6
CHUNK = 64
GLA_HEADS, GLA_DK, GLA_DV, GLA_LR, GLA_GATE_TAU = 4, 64, 128, 16, 16.0
GDN_HEADS, GDN_DK, GDN_DV, CONV_W = 4, 128, 128, 4
GLA_QK, GLA_V = GLA_HEADS * GLA_DK, GLA_HEADS * GLA_DV
GDN_QK, GDN_V = GDN_HEADS * GDN_DK, GDN_HEADS * GDN_DV
GDN_QKV = 2 * GDN_QK + GDN_V
PEER_HEADS, PEER_QDIM, N_KEYS, PEER_TOPK = 8, 256, 128, 16
PEER_HALF = PEER_QDIM // 2
N_SEL = PEER_HEADS * PEER_TOPK
HALF_EXPERTS = N_KEYS * N_KEYS // 2

LANES = 128
SUBLANES = 8
MIB = 1024 * 1024

SM_A0 = GLA_LR
SM_B0 = GLA_LR + GDN_HEADS


def _cparams(sem, vmem_mib):
    return pltpu.CompilerParams(dimension_semantics=sem, vmem_limit_bytes=vmem_mib * MIB)


def _softplus(x):
    return jnp.maximum(x, 0.0) + jnp.log(1.0 + jnp.exp(-jnp.abs(x)))


def _sigmoid(x):
    return 1.0 / (1.0 + jnp.exp(-x))


def _dot(a, b, prec=None):
    return jnp.dot(a, b, preferred_element_type=F32, precision=prec)


def _dot_nt(a, b, prec=None):
    return lax.dot_general(a, b, (((1,), (1,)), ((), ())), preferred_element_type=F32, precision=prec)


def _dot_tn(a, b, prec=None):
    return lax.dot_general(a, b, (((0,), (0,)), ((), ())), preferred_element_type=F32, precision=prec)


def _split_bf16(x):
    hi = x.astype(BF16)
    return hi, (x - hi.astype(F32)).astype(BF16)


def _split3_bf16(x):
    hi = x.astype(BF16)
    r = x - hi.astype(F32)
    mid = r.astype(BF16)
    return hi, mid, (r - mid.astype(F32)).astype(BF16)


def _dot3(a, b):
    ah, al = _split_bf16(a)
    bh, bl = _split_bf16(b)
    return _dot(jnp.concatenate([ah, ah, al], axis=1), jnp.concatenate([bh, bl, bh], axis=0))


def _dot3_nt(a, b):
    ah, al = _split_bf16(a)
    bh, bl = _split_bf16(b)
    return _dot_nt(jnp.concatenate([ah, ah, al], axis=1), jnp.concatenate([bh, bl, bh], axis=1))


def _dot3_tn(a, b):
    ah, al = _split_bf16(a)
    bh, bl = _split_bf16(b)
    return _dot_tn(jnp.concatenate([ah, ah, al], axis=0), jnp.concatenate([bh, bl, bh], axis=0))


def _dot_exact_lhs(a01, b):
    a16 = a01.astype(BF16)
    return _dot(jnp.concatenate([a16, a16, a16], axis=1), jnp.concatenate(_split3_bf16(b), axis=0))


def _dot_nt_exact_rhs(a, b01):
    b16 = b01.astype(BF16)
    return _dot_nt(jnp.concatenate(_split3_bf16(a), axis=1), jnp.concatenate([b16, b16, b16], axis=1))


_W_COLS = (("gq", GLA_QK), ("gk", GLA_QK), ("gv", GLA_V), ("gg", GLA_V), ("dqkv", GDN_QKV),
           ("dz", GDN_V), ("sm", LANES))


def _inproj_kernel(x_ref, nw_ref, w_ref, w2_ref, b2_ref, avec_ref, dtb_ref,
                   gq_ref, gk_ref, gv_ref, gg_ref, dqkv_ref, dz_ref, la_ref, sm_ref):
    x = x_ref[...]
    ms = jnp.mean(x * x, axis=-1, keepdims=True)
    h = (x * lax.rsqrt(ms + NORM_EPS) * nw_ref[...]).astype(BF16)
    outs = dict(gq=gq_ref, gk=gk_ref, gv=gv_ref, gg=gg_ref, dqkv=dqkv_ref, dz=dz_ref)
    off = 0
    ps = None
    for name, width in _W_COLS:
        p = _dot(h, w_ref[:, off:off + width])
        off += width
        if name == "sm":
            ps = p
        elif name == "gq":
            gq_ref[...] = p * (GLA_DK ** -0.5)
        else:
            outs[name][...] = p
    z = _dot(ps.astype(BF16), w2_ref[...]) + b2_ref[...]
    la_ref[...] = -_softplus(-z) * (1.0 / GLA_GATE_TAU)
    lane = lax.broadcasted_iota(jnp.int32, ps.shape, 1)
    log_a = avec_ref[...] * _softplus(ps + dtb_ref[...])
    beta = _sigmoid(ps)
    is_a = (lane >= SM_A0) & (lane < SM_B0)
    is_b = (lane >= SM_B0) & (lane < SM_B0 + GDN_HEADS)
    sm_ref[...] = jnp.where(is_a, log_a, jnp.where(is_b, beta, ps))


def _inproj(x, nw, wcat, w2p, b2, avec, dtb, tm):
    t, d = x.shape
    n_w = wcat.shape[1]
    widths = dict(_W_COLS)
    names = ("gq", "gk", "gv", "gg", "dqkv", "dz")
    out_shape = [jax.ShapeDtypeStruct((t, widths[n]), F32) for n in names]
    out_shape += [jax.ShapeDtypeStruct((t, GLA_QK), F32), jax.ShapeDtypeStruct((t, LANES), F32)]
    row = lambda w: pl.BlockSpec((tm, w), lambda i: (i, 0))
    full = lambda a: pl.BlockSpec(a.shape, lambda i: (0,) * a.ndim)
    return pl.pallas_call(
        _inproj_kernel,
        out_shape=out_shape,
        grid=(t // tm,),
        in_specs=[row(d), full(nw), full(wcat), full(w2p), full(b2), full(avec), full(dtb)],
        out_specs=[row(widths[n]) for n in names] + [row(GLA_QK), row(LANES)],
        compiler_params=_cparams(("parallel",), 48),
        name="inproj",
    )(x, nw, wcat, w2p, b2, avec, dtb)


def _iota2(n, m, axis):
    return lax.broadcasted_iota(jnp.int32, (n, m), axis)


def _gla_kernel(q_ref, k_ref, v_ref, la_ref, gg_ref, s0_ref, nw_ref, o_ref, sfin_ref, st_ref, *, chunk, n_sub):
    c = pl.program_id(1)
    nc = pl.num_programs(1)

    @pl.when(c == 0)
    def _():
        st_ref[...] = s0_ref[0]

    row = _iota2(chunk, chunk, 0)
    col = _iota2(chunk, chunk, 1)
    subs = range(n_sub)
    rows = [slice(j * chunk, (j + 1) * chunk) for j in subs]

    blocks = []
    b = chunk // 2
    while b >= 1:
        blocks.append(b)
        b //= 2
    sel = [row >= col]
    for b in blocks:
        sel.append(col <= (row // b) * b)
        sel.append(col <= jnp.minimum((row // b + 1) * b, chunk - 1))
    la_wide = jnp.concatenate([la_ref[r, :] for r in rows], axis=1)
    g_wide = _dot_exact_lhs(jnp.concatenate([m.astype(F32) for m in sel], axis=0), la_wide)

    heads = range(GLA_HEADS)
    dks = [slice(h * GLA_DK, (h + 1) * GLA_DK) for h in heads]
    dvs = [slice(h * GLA_DV, (h + 1) * GLA_DV) for h in heads]
    prep = []
    for j in subs:
        g_all = g_wide[:, j * GLA_QK:(j + 1) * GLA_QK]
        q = q_ref[rows[j], :]
        k = k_ref[rows[j], :]
        g = g_all[0:chunk]
        levels = []
        for li, b in enumerate(blocks):
            ref_r = g_all[(1 + 2 * li) * chunk:(2 + 2 * li) * chunk]
            ref_c = g_all[(2 + 2 * li) * chunk:(3 + 2 * li) * chunk]
            mask = (row // (2 * b) == col // (2 * b)) & (row // b == col // b + 1)
            levels.append((q * jnp.exp(g - ref_r), k * jnp.exp(ref_c - g), mask))
        g_last = g[chunk - 1:chunk, :]
        prep.append(dict(levels=levels, qk_diag=q * k, q_in=q * jnp.exp(g), k_out=k * jnp.exp(g_last - g),
                         decay_out=jnp.exp(g_last)))
    atts = [[jnp.zeros((chunk, chunk), F32) for _ in heads] for _ in subs]
    for li in range(len(blocks)):
        for j in subs:
            ql, kl, mask = prep[j]["levels"][li]
            atts[j] = [att + jnp.where(mask, _dot3_nt(ql[:, dk], kl[:, dk]), 0.0)
                       for att, dk in zip(atts[j], dks)]
    vhs = [[v_ref[rows[j], dv] for dv in dvs] for j in subs]
    intra = [[_dot3(atts[j][h], vhs[j][h])
              + jnp.sum(prep[j]["qk_diag"][:, dks[h]], axis=-1, keepdims=True) * vhs[j][h]
              for h in heads] for j in subs]
    kvs = [[_dot3_tn(vhs[j][h], prep[j]["k_out"][:, dks[h]]) for h in heads] for j in subs]
    sts = [st_ref[h] for h in heads]
    for j in subs:
        os_ = [_dot3_nt(prep[j]["q_in"][:, dks[h]], sts[h]) + intra[j][h] for h in heads]
        sts = [sts[h] * prep[j]["decay_out"][:, dks[h]] + kvs[j][h] for h in heads]
        for h in heads:
            o = os_[h]
            ms = jnp.mean(o * o, axis=-1, keepdims=True)
            on = o * lax.rsqrt(ms + NORM_EPS) * nw_ref[...]
            gate = gg_ref[rows[j], dvs[h]]
            o_ref[rows[j], dvs[h]] = on * (gate * _sigmoid(gate))
    for h in heads:
        st_ref[h] = sts[h]

    @pl.when(c == nc - 1)
    def _():
        sfin_ref[0] = st_ref[...]


def _chunks_per_step(seq_len, chunk):
    return 2 if seq_len % (2 * chunk) == 0 else 1


def _gla(gq, gk, gv, la, gg, s0t, nw, n_seq, seq_len, chunk):
    n_sub = _chunks_per_step(seq_len, chunk)
    nc = seq_len // (chunk * n_sub)
    t = n_seq * seq_len
    blk = lambda w: pl.BlockSpec((chunk * n_sub, w), lambda b, c: (b * nc + c, 0))
    st_spec = pl.BlockSpec((1, GLA_HEADS, GLA_DV, GLA_DK), lambda b, c: (b, 0, 0, 0))
    return pl.pallas_call(
        functools.partial(_gla_kernel, chunk=chunk, n_sub=n_sub),
        out_shape=[jax.ShapeDtypeStruct((t, GLA_V), F32),
                   jax.ShapeDtypeStruct((n_seq, GLA_HEADS, GLA_DV, GLA_DK), F32)],
        grid=(n_seq, nc),
        in_specs=[blk(GLA_QK), blk(GLA_QK), blk(GLA_V), blk(GLA_QK), blk(GLA_V), st_spec,
                  pl.BlockSpec((1, GLA_DV), lambda b, c: (0, 0))],
        out_specs=[blk(GLA_V), st_spec],
        scratch_shapes=[pltpu.VMEM((GLA_HEADS, GLA_DV, GLA_DK), F32)],
        compiler_params=_cparams(("parallel", "arbitrary"), 32),
        name="gla_scan",
    )(gq, gk, gv, la, gg, s0t, nw)


_INV_BASE = 16


def _unit_lower_inverses(lows, row, col, n):
    eye = (row == col).astype(F32)
    base = min(_INV_BASE, n)
    diag_blk = row // base == col // base
    pws = [jnp.where(diag_blk, low, 0.0) for low in lows]
    invs = [eye - ld for ld in pws]
    p = 2
    while p < base:
        pws = [_dot3(pw, pw) for pw in pws]
        invs = [inv + _dot3(inv, pw) for inv, pw in zip(invs, pws)]
        p *= 2
    b = base
    while b < n:
        sib = (row // (2 * b) == col // (2 * b)) & (row // b != col // b)
        tmps = [_dot3(jnp.where(sib, low, 0.0), inv) for low, inv in zip(lows, invs)]
        invs = [inv - _dot3(inv, t) for inv, t in zip(invs, tmps)]
        b *= 2
    return invs


def _gdn_kernel(x_ref, ab_ref, dz_ref, tail0_ref, cw_ref, s0_ref, nw_ref,
                o_ref, sfin_ref, tail_out_ref, st_ref, tail_ref, *, chunk, n_sub):
    c = pl.program_id(1)
    nc = pl.num_programs(1)

    @pl.when(c == 0)
    def _():
        st_ref[...] = s0_ref[0]
        tail_ref[...] = tail0_ref[0]

    n_rows = chunk * n_sub
    x = x_ref[...]
    ext = jnp.concatenate([tail_ref[...], x], axis=0)
    base = SUBLANES - (CONV_W - 1)
    conv = ext[base:base + n_rows] * cw_ref[0:1, :]
    for i in range(1, CONV_W):
        conv = conv + ext[base + i:base + i + n_rows] * cw_ref[i:i + 1, :]
    tail_ref[...] = x[n_rows - SUBLANES:n_rows]
    conv = conv * _sigmoid(conv)

    row = _iota2(chunk, chunk, 0)
    col = _iota2(chunk, chunk, 1)
    lower = row >= col
    tril = lower.astype(F32)

    heads = range(GDN_HEADS)
    subs = range(n_sub)
    qs, ks, vs, gcs, betas, decs = [], [], [], [], [], []
    for j in subs:
        rows = slice(j * chunk, (j + 1) * chunk)
        ab = ab_ref[rows, :]
        g_col = _dot_exact_lhs(tril, ab)
        g_row = _dot_nt_exact_rhs(ab.T, tril)
        for h in heads:
            cq = conv[rows, h * GDN_DK:(h + 1) * GDN_DK]
            ck = conv[rows, GDN_QK + h * GDN_DK:GDN_QK + (h + 1) * GDN_DK]
            vs.append(conv[rows, 2 * GDN_QK + h * GDN_DV:2 * GDN_QK + (h + 1) * GDN_DV])
            qs.append(cq * lax.rsqrt(jnp.sum(cq * cq, axis=-1, keepdims=True) + NORM_EPS) * (GDN_DK ** -0.5))
            ks.append(ck * lax.rsqrt(jnp.sum(ck * ck, axis=-1, keepdims=True) + NORM_EPS))
            gc = g_col[:, SM_A0 + h:SM_A0 + h + 1]
            gr = g_row[SM_A0 + h:SM_A0 + h + 1, :]
            gcs.append(gc)
            betas.append(ab[:, SM_B0 + h:SM_B0 + h + 1])
            decs.append(jnp.where(lower, jnp.exp(jnp.where(lower, gc - gr, 0.0)), 0.0))
    kks = [_dot3_nt(k, k) for k in ks]
    qks = [_dot3_nt(q, k) * dec for q, k, dec in zip(qs, ks, decs)]
    lows = [jnp.where(row > col, beta * kk * dec, 0.0) for beta, kk, dec in zip(betas, kks, decs)]
    tinvs = _unit_lower_inverses(lows, row, col, chunk)
    egs = [jnp.exp(gc) for gc in gcs]
    uws = [_dot3(tinv, jnp.concatenate([v * beta, k * (beta * eg)], axis=1))
           for tinv, v, k, beta, eg in zip(tinvs, vs, ks, betas, egs)]
    sts = [st_ref[h] for h in heads]
    for j in subs:
        rows = slice(j * chunk, (j + 1) * chunk)
        ids = [j * GDN_HEADS + h for h in heads]
        v_news = [uws[i][:, :GDN_DV] - _dot3(uws[i][:, GDN_DV:], st) for i, st in zip(ids, sts)]
        os_ = [_dot3(qs[i] * egs[i], st) + _dot3(qks[i], v_new) for i, st, v_new in zip(ids, sts, v_news)]
        new_sts = []
        for h, i in enumerate(ids):
            g_last = gcs[i][chunk - 1:chunk, :]
            new_sts.append(sts[h] * jnp.exp(g_last) + _dot3_tn(ks[i] * jnp.exp(g_last - gcs[i]), v_news[h]))
        sts = new_sts
        for h in heads:
            o = os_[h]
            ms = jnp.mean(o * o, axis=-1, keepdims=True)
            on = o * lax.rsqrt(ms + NORM_EPS) * nw_ref[...]
            gate = dz_ref[rows, h * GDN_DV:(h + 1) * GDN_DV]
            o_ref[rows, h * GDN_DV:(h + 1) * GDN_DV] = on * (gate * _sigmoid(gate))
    for h in heads:
        st_ref[h] = sts[h]

    @pl.when(c == nc - 1)
    def _():
        sfin_ref[0] = st_ref[...]
        tail_out_ref[0] = tail_ref[...]


def _gdn(dqkv, ab, dz, tail0, cw, s0, nw, n_seq, seq_len, chunk):
    n_sub = _chunks_per_step(seq_len, chunk)
    nc = seq_len // (chunk * n_sub)
    t = n_seq * seq_len
    blk = lambda w: pl.BlockSpec((chunk * n_sub, w), lambda b, c: (b * nc + c, 0))
    st_spec = pl.BlockSpec((1, GDN_HEADS, GDN_DK, GDN_DV), lambda b, c: (b, 0, 0, 0))
    tail_spec = pl.BlockSpec((1, SUBLANES, GDN_QKV), lambda b, c: (b, 0, 0))
    return pl.pallas_call(
        functools.partial(_gdn_kernel, chunk=chunk, n_sub=n_sub),
        out_shape=[jax.ShapeDtypeStruct((t, GDN_V), F32),
                   jax.ShapeDtypeStruct((n_seq, GDN_HEADS, GDN_DK, GDN_DV), F32),
                   jax.ShapeDtypeStruct((n_seq, SUBLANES, GDN_QKV), F32)],
        grid=(n_seq, nc),
        in_specs=[blk(GDN_QKV), blk(LANES), blk(GDN_V), tail_spec,
                  pl.BlockSpec((SUBLANES, GDN_QKV), lambda b, c: (0, 0)), st_spec,
                  pl.BlockSpec((1, GDN_DV), lambda b, c: (0, 0))],
        out_specs=[blk(GDN_V), st_spec, tail_spec],
        scratch_shapes=[pltpu.VMEM((GDN_HEADS, GDN_DK, GDN_DV), F32),
                        pltpu.VMEM((SUBLANES, GDN_QKV), F32)],
        compiler_params=_cparams(("parallel", "arbitrary"), 32),
        name="gdn_scan",
    )(dqkv, ab, dz, tail0, cw, s0, nw)


def _topk_cols(s, k, payload=None):
    n = s.shape[0]
    rid = lax.broadcasted_iota(jnp.int32, s.shape, 0).astype(F32)
    vals, idxs = [], []
    for _ in range(k):
        m = jnp.max(s, axis=0, keepdims=True)
        am = jnp.min(jnp.where(s == m, rid, float(n)), axis=0, keepdims=True)
        sel = rid == am
        vals.append(m)
        if payload is None:
            idxs.append(am)
        else:
            idxs.append(jnp.sum(jnp.where(sel, payload, 0.0), axis=0, keepdims=True))
        s = jnp.where(sel, -jnp.inf, s)
    return jnp.concatenate(vals, axis=0), jnp.concatenate(idxs, axis=0)


def _retrieve(s1, s2):
    v1, i1 = _topk_cols(s1, PEER_TOPK)
    v2, i2 = _topk_cols(s2, PEER_TOPK)
    cand_rows, cidx_rows = [], []
    for a in range(PEER_TOPK // 2):
        n_valid = PEER_TOPK // (a + 1)
        n_rows = -(-n_valid // SUBLANES) * SUBLANES
        c = v1[a:a + 1, :] + v2[0:n_rows, :]
        if n_valid < n_rows:
            c = jnp.where(lax.broadcasted_iota(jnp.int32, c.shape, 0) < n_valid, c, -jnp.inf)
        cand_rows.append(c)
        cidx_rows.append(i1[a:a + 1, :] * float(N_KEYS) + i2[0:n_rows, :])
    cand_rows.append(v1[PEER_TOPK // 2:, :] + v2[0:1, :])
    cidx_rows.append(i1[PEER_TOPK // 2:, :] * float(N_KEYS) + i2[0:1, :])
    sc, eidx = _topk_cols(jnp.concatenate(cand_rows, axis=0), PEER_TOPK,
                          payload=jnp.concatenate(cidx_rows, axis=0))
    e = jnp.exp(sc - jnp.max(sc, axis=0, keepdims=True))
    return eidx, e / jnp.sum(e, axis=0, keepdims=True)


def _post_kernel(o1_ref, o2_ref, x_ref, wo_ref, nw_ref, wq_ref, k1_ref, k2_ref,
                 x2_ref, hn_ref, j8_ref, sh_ref, gate_ref):
    mix = (_dot(o1_ref[...].astype(BF16), wo_ref[0:GLA_V, :])
           + _dot(o2_ref[...].astype(BF16), wo_ref[GLA_V:GLA_V + GDN_V, :]))
    x2 = x_ref[...] + mix
    x2_ref[...] = x2
    ms = jnp.mean(x2 * x2, axis=-1, keepdims=True)
    hn = x2 * lax.rsqrt(ms + NORM_EPS) * nw_ref[...]
    hn_ref[...] = hn
    qry = _dot(hn.astype(BF16), wq_ref[...])
    idx_rows, gate_rows = [], []
    for h in range(PEER_HEADS):
        q1 = qry[:, h * PEER_QDIM:h * PEER_QDIM + PEER_HALF].astype(BF16)
        q2 = qry[:, h * PEER_QDIM + PEER_HALF:(h + 1) * PEER_QDIM].astype(BF16)
        s1 = _dot_nt(k1_ref[h], q1)
        s2 = _dot_nt(k2_ref[h], q2)
        parts = [_retrieve(s1[:, c0:c0 + LANES], s2[:, c0:c0 + LANES]) for c0 in range(0, s1.shape[1], LANES)]
        idx_rows.append(jnp.concatenate([p[0] for p in parts], axis=1))
        gate_rows.append(jnp.concatenate([p[1] for p in parts], axis=1))
    e = jnp.concatenate(idx_rows, axis=0).T.astype(jnp.int32)
    j8_ref[...] = (e & (HALF_EXPERTS - 1)) * SUBLANES
    sh_ref[...] = jnp.where(e >= HALF_EXPERTS, 0.0, 16.0)
    gate_ref[...] = jnp.concatenate(gate_rows, axis=0).T


def _post(o1, o2, x, wo, nw, wq, k1, k2, tm):
    t, d = x.shape
    row = lambda w: pl.BlockSpec((tm, w), lambda i: (i, 0))
    full = lambda a: pl.BlockSpec(a.shape, lambda i: (0,) * a.ndim)
    return pl.pallas_call(
        _post_kernel,
        out_shape=[jax.ShapeDtypeStruct((t, d), F32), jax.ShapeDtypeStruct((t, d), F32),
                   jax.ShapeDtypeStruct((t, N_SEL), jnp.int32), jax.ShapeDtypeStruct((t, N_SEL), F32),
                   jax.ShapeDtypeStruct((t, N_SEL), F32)],
        grid=(t // tm,),
        in_specs=[row(GLA_V), row(GDN_V), row(d), full(wo), full(nw), full(wq), full(k1), full(k2)],
        out_specs=[row(d), row(d), row(N_SEL), row(N_SEL), row(N_SEL)],
        compiler_params=_cparams(("parallel",), 48),
        name="post_topk",
    )(o1, o2, x, wo, nw, wq, k1, k2)


_HI_MASK = -65536
_IDX_GROUP = 16


def _expert_row(tab_ref, j8, shift_row):
    words = tab_ref[pl.ds(pl.multiple_of(j8, SUBLANES), SUBLANES), :]
    bits = lax.shift_left(words, jnp.broadcast_to(shift_row, (SUBLANES, LANES))) & _HI_MASK
    return lax.bitcast_convert_type(bits, F32)


def _diag_rows(vals, eye):
    n = vals.shape[0]
    return (vals[:, None, :] * eye[None, :, :]).reshape(n * N_SEL, N_SEL)


def _add_bf16_pairs(a, b):
    return pltpu.bitcast(pltpu.bitcast(a, BF16) + pltpu.bitcast(b, BF16), jnp.int32)


def _fold_sublanes(vs, sub, add):
    m4 = sub < 4
    cur = []
    for a, b in zip(vs[0::2], vs[1::2]):
        cur.append(add(jnp.where(m4, a, b), pltpu.roll(jnp.where(m4, b, a), 4, 0)))
    for half in (2, 1):
        m = (sub % (2 * half)) < half
        nxt = []
        for a, b in zip(cur[0::2], cur[1::2]):
            ta = add(a, pltpu.roll(a, SUBLANES - half, 0))
            tb = add(b, pltpu.roll(b, half, 0))
            nxt.append(jnp.where(m, ta, tb))
        cur = nxt
    return cur[0]


def _eye():
    return (_iota2(N_SEL, N_SEL, 0) == _iota2(N_SEL, N_SEL, 1)).astype(F32)


def _peer_u_kernel(j8_ref, sh_ref, hn_ref, gate_ref, tab_ref, w_ref, *scratch, tb):
    q_refs = scratch[:_IDX_GROUP]
    idx_ref, sem = scratch[_IDX_GROUP:]
    sub = lax.broadcasted_iota(jnp.int32, (SUBLANES, LANES), 0)
    ones = jnp.ones((LANES, LANES), BF16)
    perm = (0, 4, 2, 6, 1, 5, 3, 7)
    half_of_row = _iota2(2 * N_SEL, N_SEL, 0) - 2 * _iota2(2 * N_SEL, N_SEL, 1)

    def finish(src_ref, out_tok):
        rep = _dot(pltpu.bitcast(src_ref[...], BF16), ones)
        high = jnp.where(sh_ref[pl.ds(out_tok, 1), :] == 0.0, 1, 0)
        act = jnp.sum(jnp.where(half_of_row == high, rep, 0.0), axis=0, keepdims=True)
        gelu = 0.5 * act * (1.0 + jnp.tanh(0.7978845608028654 * (act + 0.044715 * act * act * act)))
        w_ref[pl.ds(out_tok, 1), :] = gate_ref[pl.ds(out_tok, 1), :] * gelu

    n_groups = tb // _IDX_GROUP

    def idx_copy(g, buf):
        src = j8_ref.at[pl.ds(g * _IDX_GROUP, _IDX_GROUP)]
        return pltpu.make_async_copy(src, idx_ref.at[buf], sem.at[buf])

    lag = 2
    for q_ref in q_refs[len(q_refs) - lag:]:
        q_ref[...] = jnp.zeros((N_SEL, LANES), jnp.int32)
    idx_copy(0, 0).start()

    def body(i, carry):
        for buf in range(2):
            g = 2 * i + buf
            idx_copy(g, buf).wait()

            @pl.when(g + 1 < n_groups)
            def _():
                idx_copy(g + 1, 1 - buf).start()

            for s in range(_IDX_GROUP):
                t = g * _IDX_GROUP + s
                finish(q_refs[(s - lag) % _IDX_GROUP], jnp.maximum(t - lag, 0))
                xi = lax.bitcast_convert_type(hn_ref[t].astype(BF16).astype(F32), jnp.int32)
                xx = pltpu.bitcast(xi | lax.shift_right_logical(xi, 16), BF16)
                folded = []
                for grp in range(N_SEL // SUBLANES):
                    prods = []
                    for r in range(SUBLANES):
                        k = grp * SUBLANES + perm[r]
                        row0 = pl.multiple_of(idx_ref[buf, s, 0, k], SUBLANES)
                        words = tab_ref[pl.ds(row0, SUBLANES), :]
                        prods.append(pltpu.bitcast(pltpu.bitcast(words, BF16) * xx, jnp.int32))
                    folded.append(_fold_sublanes(prods, sub, _add_bf16_pairs))
                q_refs[s][...] = jnp.concatenate(folded, axis=0)
        return carry

    lax.fori_loop(0, n_groups // 2, body, 0)
    for t in range(tb - lag, tb):
        finish(q_refs[t % _IDX_GROUP], t)


def _peer_u(j8, sh, hn3, gate, tab, tb):
    t = j8.shape[0]
    vec = lambda: pl.BlockSpec((tb, N_SEL), lambda i: (i, 0))
    return pl.pallas_call(
        functools.partial(_peer_u_kernel, tb=tb),
        out_shape=jax.ShapeDtypeStruct((t, N_SEL), F32),
        grid=(t // tb,),
        in_specs=[pl.BlockSpec((tb, 1, N_SEL), lambda i: (i, 0, 0)), vec(),
                  pl.BlockSpec((tb, SUBLANES, LANES), lambda i: (i, 0, 0)), vec(),
                  pl.BlockSpec(memory_space=pltpu.VMEM)],
        out_specs=vec(),
        scratch_shapes=[pltpu.VMEM((N_SEL, LANES), jnp.int32) for _ in range(_IDX_GROUP)]
        + [pltpu.SMEM((2, _IDX_GROUP, 1, N_SEL), jnp.int32), pltpu.SemaphoreType.DMA((2,))],
        compiler_params=_cparams(("arbitrary",), 56),
        name="peer_u",
    )(j8.reshape(t, 1, N_SEL), sh, hn3, gate, tab)


def _peer_v_kernel(j8_ref, sh_ref, wgt_ref, x2_ref, nw_ref, tab_ref, y_ref, wp_ref, idx_ref, sem,
                   *, tb, n_acc):
    eye = _eye()
    blk = (_iota2(2 * N_SEL, 2 * LANES, 0) // N_SEL == _iota2(2 * N_SEL, 2 * LANES, 1) // LANES).astype(BF16)

    def pre(c, carry):
        tok0 = pl.multiple_of(c * SUBLANES, SUBLANES)
        toks = pl.ds(tok0, SUBLANES)
        w = wgt_ref[toks, :]
        low = sh_ref[toks, :] > 0.0
        diag = jnp.concatenate([_diag_rows(jnp.where(low, w, 0.0), eye).astype(BF16),
                                _diag_rows(jnp.where(low, 0.0, w), eye).astype(BF16)], axis=1)
        rep = lax.bitcast_convert_type(_dot(diag, blk), jnp.int32)
        dst = pl.ds(pl.multiple_of(tok0 * N_SEL, SUBLANES * N_SEL), SUBLANES * N_SEL)
        wp_ref[dst, :] = rep[:, LANES:] | lax.shift_right_logical(rep[:, :LANES], 16)
        return carry

    lax.fori_loop(0, tb // SUBLANES, pre, 0)
    group4 = 4

    n_groups = tb // _IDX_GROUP

    def idx_copy(g, buf):
        src = j8_ref.at[pl.ds(g * _IDX_GROUP, _IDX_GROUP)]
        return pltpu.make_async_copy(src, idx_ref.at[buf], sem.at[buf])

    idx_copy(0, 0).start()

    def group_pair(i, carry):
        for buf in range(2):
            g = 2 * i + buf
            idx_copy(g, buf).wait()

            @pl.when(g + 1 < n_groups)
            def _():
                idx_copy(g + 1, 1 - buf).start()

            for s in range(_IDX_GROUP):
                t = g * _IDX_GROUP + s
                base = t * N_SEL
                accs = [jnp.zeros((SUBLANES, LANES), F32) for _ in range(n_acc)]
                for k0 in range(0, N_SEL, group4):
                    part = None
                    for k in range(k0, k0 + group4):
                        row0 = pl.multiple_of(idx_ref[buf, s, 0, k], SUBLANES)
                        words = pltpu.bitcast(tab_ref[pl.ds(row0, SUBLANES), :], BF16)
                        wpair = jnp.broadcast_to(wp_ref[pl.ds(base + k, 1), :], (SUBLANES, LANES))
                        prod = words * pltpu.bitcast(wpair, BF16)
                        part = prod if part is None else part + prod
                    bits = pltpu.bitcast(part, jnp.int32)
                    a = (k0 // group4) % (n_acc // 2)
                    accs[2 * a] = accs[2 * a] + lax.bitcast_convert_type(lax.shift_left(bits, 16), F32)
                    accs[2 * a + 1] = accs[2 * a + 1] + lax.bitcast_convert_type(bits & _HI_MASK, F32)
                tot = accs[0]
                for a in accs[1:]:
                    tot = tot + a
                y_ref[t] = x2_ref[t] + tot
        return carry

    lax.fori_loop(0, n_groups // 2, group_pair, 0)
    x3 = y_ref[...]
    ss = jnp.sum(jnp.sum(x3 * x3, axis=2, keepdims=True), axis=1, keepdims=True)
    y_ref[...] = x3 * lax.rsqrt(ss * (1.0 / (SUBLANES * LANES)) + NORM_EPS) * nw_ref[...][None]


def _peer_v(j8, sh, wgt, x23, nw3, tab, tb):
    t = j8.shape[0]
    vec = lambda: pl.BlockSpec((tb, N_SEL), lambda i: (i, 0))
    return pl.pallas_call(
        functools.partial(_peer_v_kernel, tb=tb, n_acc=4),
        out_shape=jax.ShapeDtypeStruct((t, SUBLANES, LANES), F32),
        grid=(t // tb,),
        in_specs=[pl.BlockSpec((tb, 1, N_SEL), lambda i: (i, 0, 0)), vec(), vec(),
                  pl.BlockSpec((tb, SUBLANES, LANES), lambda i: (i, 0, 0)),
                  pl.BlockSpec((SUBLANES, LANES), lambda i: (0, 0)),
                  pl.BlockSpec(memory_space=pltpu.VMEM)],
        out_specs=pl.BlockSpec((tb, SUBLANES, LANES), lambda i: (i, 0, 0)),
        scratch_shapes=[pltpu.VMEM((tb * N_SEL, LANES), jnp.int32),
                        pltpu.SMEM((2, _IDX_GROUP, 1, N_SEL), jnp.int32),
                        pltpu.SemaphoreType.DMA((2,))],
        compiler_params=_cparams(("arbitrary",), 56),
        name="peer_v",
    )(j8.reshape(t, 1, N_SEL), sh, wgt, x23, nw3, tab)


_PACK_ROWS = 256


def _pack_kernel(lo_ref, hi_ref, o_ref):
    lo = lax.bitcast_convert_type(lo_ref[...].astype(BF16).astype(F32), jnp.int32)
    hi = lax.bitcast_convert_type(hi_ref[...].astype(BF16).astype(F32), jnp.int32)
    packed = hi | lax.shift_right_logical(lo, 16)
    rows, d = packed.shape
    for s in range(d // LANES):
        o_ref[pl.ds(s, rows, stride=d // LANES), :] = packed[:, s * LANES:(s + 1) * LANES]


def _pack_table(tab):
    n, d = tab.shape
    sub = d // LANES
    steps = n // 2 // _PACK_ROWS
    return pl.pallas_call(
        _pack_kernel,
        out_shape=jax.ShapeDtypeStruct((n // 2 * sub, LANES), jnp.int32),
        grid=(steps,),
        in_specs=[pl.BlockSpec((_PACK_ROWS, d), lambda i: (i, 0)),
                  pl.BlockSpec((_PACK_ROWS, d), lambda i: (i + steps, 0))],
        out_specs=pl.BlockSpec((_PACK_ROWS * sub, LANES), lambda i: (i, 0)),
        compiler_params=_cparams(("arbitrary",), 32),
        name="pack_table",
    )(tab, tab)


def _prep_weights(norm_mix_w, w_in, gla_w_gk2, gla_b_gk, gla_norm_w, gdn_conv_w, gdn_a_log,
                  gdn_dt_bias, gdn_norm_w, w_out, norm_ffn_w, peer_wq, peer_k1, peer_k2, peer_u, peer_v):
    d = w_in.shape[0]
    o_glr = 2 * GLA_QK + GLA_V
    o_gg = o_glr + GLA_LR
    o_qkv = o_gg + GLA_V
    o_da = o_qkv + GDN_QKV
    o_dz = o_da + 2 * GDN_HEADS
    small = jnp.concatenate([w_in[:, o_glr:o_gg], w_in[:, o_da:o_dz],
                             jnp.zeros((d, LANES - GLA_LR - 2 * GDN_HEADS), w_in.dtype)], axis=1)
    wcat = jnp.concatenate([w_in[:, :o_glr], w_in[:, o_gg:o_qkv], w_in[:, o_qkv:o_da],
                            w_in[:, o_dz:], small], axis=1).astype(BF16)
    w2p = jnp.zeros((LANES, GLA_QK), F32).at[:GLA_LR].set(gla_w_gk2).astype(BF16)
    avec = jnp.zeros((1, LANES), F32).at[0, SM_A0:SM_B0].set(-jnp.exp(gdn_a_log))
    dtb = jnp.zeros((1, LANES), F32).at[0, SM_A0:SM_B0].set(gdn_dt_bias)
    cw = jnp.zeros((SUBLANES, GDN_QKV), F32).at[:CONV_W].set(gdn_conv_w)
    return dict(
        nmix=norm_mix_w.reshape(1, d), wcat=wcat, w2p=w2p, b2=gla_b_gk.reshape(1, GLA_QK), avec=avec, dtb=dtb,
        gla_nw=gla_norm_w.reshape(1, GLA_DV), cw=cw, gdn_nw=gdn_norm_w.reshape(1, GDN_DV),
        wo=w_out.astype(BF16), nffn=norm_ffn_w.reshape(1, d), wq=peer_wq.astype(BF16),
        k1=peer_k1.astype(BF16), k2=peer_k2.astype(BF16), tab_u=_pack_table(peer_u), tab_v=_pack_table(peer_v))


def _layer(x, s_gla, s_gdn, conv_buf, w, nfw3, chunk, tm, tb):
    n_seq, seq_len, d = x.shape
    t = n_seq * seq_len
    xf = x.reshape(t, d)
    gq, gk, gv, gg, dqkv, dz, la, sm = _inproj(xf, w["nmix"], w["wcat"], w["w2p"], w["b2"], w["avec"],
                                               w["dtb"], tm)
    o1, gla_t = _gla(gq, gk, gv, la, gg, jnp.swapaxes(s_gla, -1, -2), w["gla_nw"], n_seq, seq_len, chunk)
    tail0 = jnp.pad(conv_buf, ((0, 0), (SUBLANES - (CONV_W - 1), 0), (0, 0)))
    o2, gdn_new, tail = _gdn(dqkv, sm, dz, tail0, w["cw"], s_gdn, w["gdn_nw"], n_seq, seq_len, chunk)
    x2, hn, j8, sh, gate = _post(o1, o2, xf, w["wo"], w["nffn"], w["wq"], w["k1"], w["k2"], tm)
    wgt = _peer_u(j8, sh, hn.reshape(t, SUBLANES, LANES), gate, w["tab_u"], 2 * tb)
    y = _peer_v(j8, sh, wgt, x2.reshape(t, SUBLANES, LANES), nfw3, w["tab_v"], tb)
    return (y.reshape(n_seq, seq_len, d), jnp.swapaxes(gla_t, -1, -2), gdn_new,
            tail[:, SUBLANES - (CONV_W - 1):, :])


def kernel(x_prompt, x_sample, state_gla, state_gdn, state_gdn_conv, norm_mix_w, w_in, gla_w_gk2, gla_b_gk,
           gla_norm_w, gdn_conv_w, gdn_a_log, gdn_dt_bias, gdn_norm_w, w_out, norm_ffn_w, peer_wq, peer_k1,
           peer_k2, peer_u, peer_v, norm_final_w):
    depth = w_in.shape[0]
    assert depth == 1, "the final norm is fused into the last layer's PEER pass"
    n_p, l_p, d = x_prompt.shape
    n_s, l_s, _ = x_sample.shape
    nfw3 = norm_final_w.reshape(SUBLANES, LANES)
    w = _prep_weights(norm_mix_w[0], w_in[0], gla_w_gk2[0], gla_b_gk[0], gla_norm_w[0], gdn_conv_w[0],
                      gdn_a_log[0], gdn_dt_bias[0], gdn_norm_w[0], w_out[0], norm_ffn_w[0], peer_wq[0],
                      peer_k1[0], peer_k2[0], peer_u[0], peer_v[0])
    z_gla = jnp.zeros((n_p, GLA_HEADS, GLA_DK, GLA_DV), F32)
    z_gdn = jnp.zeros((n_p, GDN_HEADS, GDN_DK, GDN_DV), F32)
    z_conv = jnp.zeros((n_p, CONV_W - 1, GDN_QKV), F32)
    chunk_p = min(CHUNK, l_p)
    chunk_s = min(CHUNK, l_s)
    tm_p = min(256, n_p * l_p)
    tm_s = min(256, n_s * l_s)
    yp, gla_p, gdn_p, conv_p = _layer(x_prompt, z_gla, z_gdn, z_conv, w, nfw3, chunk_p, tm_p, min(64, tm_p))
    ys, gla_s, gdn_s, conv_s = _layer(x_sample, state_gla[0], state_gdn[0], state_gdn_conv[0], w, nfw3,
                                      chunk_s, tm_s, min(64, tm_s))
    return (yp, ys, gla_p[None], gdn_p[None], conv_p[None], gla_s[None], gdn_s[None], conv_s[None])
```

```python
import functools

import jax
import jax.numpy as jnp
from jax import lax
from jax.experimental import pallas as pl
from jax.experimental.pallas import tpu as pltpu

F32 = jnp.float32
BF16 = jnp.bfloat16
HI = lax.Precision.HIGHEST

NORM_EPS = 1e-6
CHUNK = 64
GLA_HEADS, GLA_DK, GLA_DV, GLA_LR, GLA_GATE_TAU = 4, 64, 128, 16, 16.0
GDN_HEADS, GDN_DK, GDN_DV, CONV_W = 4, 128, 128, 4
GLA_QK, GLA_V = GLA_HEADS * GLA_DK, GLA_HEADS * GLA_DV
GDN_QK, GDN_V = GDN_HEADS * GDN_DK, GDN_HEADS * GDN_DV
GDN_QKV = 2 * GDN_QK + GDN_V
PEER_HEADS, PEER_QDIM, N_KEYS, PEER_TOPK = 8, 256, 128, 16
PEER_HALF = PEER_QDIM // 2
N_SEL = PEER_HEADS * PEER_TOPK
HALF_EXPERTS = N_KEYS * N_KEYS // 2

LANES = 128
SUBLANES = 8
MIB = 1024 * 1024

SM_A0 = GLA_LR
SM_B0 = GLA_LR + GDN_HEADS


def _cparams(sem, vmem_mib):
    return pltpu.CompilerParams(dimension_semantics=sem, vmem_limit_bytes=vmem_mib * MIB)


def _softplus(x):
    return jnp.maximum(x, 0.0) + jnp.log(1.0 + jnp.exp(-jnp.abs(x)))


def _sigmoid(x):
    return 1.0 / (1.0 + jnp.exp(-x))


def _dot(a, b, prec=None):
    return jnp.dot(a, b, preferred_element_type=F32, precision=prec)


def _dot_nt(a, b, prec=None):
    return lax.dot_general(a, b, (((1,), (1,)), ((), ())), preferred_element_type=F32, precision=prec)


def _dot_tn(a, b, prec=None):
    return lax.dot_general(a, b, (((0,), (0,)), ((), ())), preferred_element_type=F32, precision=prec)


def _split_bf16(x):
    hi = x.astype(BF16)
    return hi, (x - hi.astype(F32)).astype(BF16)


def _split3_bf16(x):
    hi = x.astype(BF16)
    r = x - hi.astype(F32)
    mid = r.astype(BF16)
    return hi, mid, (r - mid.astype(F32)).astype(BF16)


def _dot3(a, b):
    ah, al = _split_bf16(a)
    bh, bl = _split_bf16(b)
    return _dot(jnp.concatenate([ah, ah, al], axis=1), jnp.concatenate([bh, bl, bh], axis=0))


def _dot3_nt(a, b):
    ah, al = _split_bf16(a)
    bh, bl = _split_bf16(b)
    return _dot_nt(jnp.concatenate([ah, ah, al], axis=1), jnp.concatenate([bh, bl, bh], axis=1))


def _dot3_tn(a, b):
    ah, al = _split_bf16(a)
    bh, bl = _split_bf16(b)
    return _dot_tn(jnp.concatenate([ah, ah, al], axis=0), jnp.concatenate([bh, bl, bh], axis=0))


def _dot_exact_lhs(a01, b):
    a16 = a01.astype(BF16)
    return _dot(jnp.concatenate([a16, a16, a16], axis=1), jnp.concatenate(_split3_bf16(b), axis=0))


def _dot_nt_exact_rhs(a, b01):
    b16 = b01.astype(BF16)
    return _dot_nt(jnp.concatenate(_split3_bf16(a), axis=1), jnp.concatenate([b16, b16, b16], axis=1))


_W_COLS = (("gq", GLA_QK), ("gk", GLA_QK), ("gv", GLA_V), ("gg", GLA_V), ("dqkv", GDN_QKV),
           ("dz", GDN_V), ("sm", LANES))


def _inproj_kernel(x_ref, nw_ref, w_ref, w2_ref, b2_ref, avec_ref, dtb_ref,
                   gq_ref, gk_ref, gv_ref, gg_ref, dqkv_ref, dz_ref, la_ref, sm_ref):
    x = x_ref[...]
    ms = jnp.mean(x * x, axis=-1, keepdims=True)
    h = (x * lax.rsqrt(ms + NORM_EPS) * nw_ref[...]).astype(BF16)
    outs = dict(gq=gq_ref, gk=gk_ref, gv=gv_ref, gg=gg_ref, dqkv=dqkv_ref, dz=dz_ref)
    off = 0
    ps = None
    for name, width in _W_COLS:
        p = _dot(h, w_ref[:, off:off + width])
        off += width
        if name == "sm":
            ps = p
        elif name == "gq":
            gq_ref[...] = p * (GLA_DK ** -0.5)
        else:
            outs[name][...] = p
    z = _dot(ps.astype(BF16), w2_ref[...]) + b2_ref[...]
    la_ref[...] = -_softplus(-z) * (1.0 / GLA_GATE_TAU)
    lane = lax.broadcasted_iota(jnp.int32, ps.shape, 1)
    log_a = avec_ref[...] * _softplus(ps + dtb_ref[...])
    beta = _sigmoid(ps)
    is_a = (lane >= SM_A0) & (lane < SM_B0)
    is_b = (lane >= SM_B0) & (lane < SM_B0 + GDN_HEADS)
    sm_ref[...] = jnp.where(is_a, log_a, jnp.where(is_b, beta, ps))


def _inproj(x, nw, wcat, w2p, b2, avec, dtb, tm):
    t, d = x.shape
    n_w = wcat.shape[1]
    widths = dict(_W_COLS)
    names = ("gq", "gk", "gv", "gg", "dqkv", "dz")
    out_shape = [jax.ShapeDtypeStruct((t, widths[n]), F32) for n in names]
    out_shape += [jax.ShapeDtypeStruct((t, GLA_QK), F32), jax.ShapeDtypeStruct((t, LANES), F32)]
    row = lambda w: pl.BlockSpec((tm, w), lambda i: (i, 0))
    full = lambda a: pl.BlockSpec(a.shape, lambda i: (0,) * a.ndim)
    return pl.pallas_call(
        _inproj_kernel,
        out_shape=out_shape,
        grid=(t // tm,),
        in_specs=[row(d), full(nw), full(wcat), full(w2p), full(b2), full(avec), full(dtb)],
        out_specs=[row(widths[n]) for n in names] + [row(GLA_QK), row(LANES)],
        compiler_params=_cparams(("parallel",), 48),
        name="inproj",
    )(x, nw, wcat, w2p, b2, avec, dtb)


def _iota2(n, m, axis):
    return lax.broadcasted_iota(jnp.int32, (n, m), axis)


def _gla_kernel(q_ref, k_ref, v_ref, la_ref, gg_ref, s0_ref, nw_ref, o_ref, sfin_ref, st_ref, *, chunk, n_sub):
    c = pl.program_id(1)
    nc = pl.num_programs(1)

    @pl.when(c == 0)
    def _():
        st_ref[...] = s0_ref[0]

    row = _iota2(chunk, chunk, 0)
    col = _iota2(chunk, chunk, 1)
    subs = range(n_sub)
    rows = [slice(j * chunk, (j + 1) * chunk) for j in subs]

    blocks = []
    b = chunk // 2
    while b >= 1:
        blocks.append(b)
        b //= 2
    sel = [row >= col]
    for b in blocks:
        sel.append(col <= (row // b) * b)
        sel.append(col <= jnp.minimum((row // b + 1) * b, chunk - 1))
    la_wide = jnp.concatenate([la_ref[r, :] for r in rows], axis=1)
    g_wide = _dot_exact_lhs(jnp.concatenate([m.astype(F32) for m in sel], axis=0), la_wide)

    heads = range(GLA_HEADS)
    dks = [slice(h * GLA_DK, (h + 1) * GLA_DK) for h in heads]
    dvs = [slice(h * GLA_DV, (h + 1) * GLA_DV) for h in heads]
    prep = []
    for j in subs:
        g_all = g_wide[:, j * GLA_QK:(j + 1) * GLA_QK]
        q = q_ref[rows[j], :]
        k = k_ref[rows[j], :]
        g = g_all[0:chunk]
        levels = []
        for li, b in enumerate(blocks):
            ref_r = g_all[(1 + 2 * li) * chunk:(2 + 2 * li) * chunk]
            ref_c = g_all[(2 + 2 * li) * chunk:(3 + 2 * li) * chunk]
            mask = (row // (2 * b) == col // (2 * b)) & (row // b == col // b + 1)
            levels.append((q * jnp.exp(g - ref_r), k * jnp.exp(ref_c - g), mask))
        g_last = g[chunk - 1:chunk, :]
        prep.append(dict(levels=levels, qk_diag=q * k, q_in=q * jnp.exp(g), k_out=k * jnp.exp(g_last - g),
                         decay_out=jnp.exp(g_last)))
    atts = [[jnp.zeros((chunk, chunk), F32) for _ in heads] for _ in subs]
    for li in range(len(blocks)):
        for j in subs:
            ql, kl, mask = prep[j]["levels"][li]
            atts[j] = [att + jnp.where(mask, _dot3_nt(ql[:, dk], kl[:, dk]), 0.0)
                       for att, dk in zip(atts[j], dks)]
    vhs = [[v_ref[rows[j], dv] for dv in dvs] for j in subs]
    intra = [[_dot3(atts[j][h], vhs[j][h])
              + jnp.sum(prep[j]["qk_diag"][:, dks[h]], axis=-1, keepdims=True) * vhs[j][h]
              for h in heads] for j in subs]
    kvs = [[_dot3_tn(vhs[j][h], prep[j]["k_out"][:, dks[h]]) for h in heads] for j in subs]
    sts = [st_ref[h] for h in heads]
    for j in subs:
        os_ = [_dot3_nt(prep[j]["q_in"][:, dks[h]], sts[h]) + intra[j][h] for h in heads]
        sts = [sts[h] * prep[j]["decay_out"][:, dks[h]] + kvs[j][h] for h in heads]
        for h in heads:
            o = os_[h]
            ms = jnp.mean(o * o, axis=-1, keepdims=True)
            on = o * lax.rsqrt(ms + NORM_EPS) * nw_ref[...]
            gate = gg_ref[rows[j], dvs[h]]
            o_ref[rows[j], dvs[h]] = on * (gate * _sigmoid(gate))
    for h in heads:
        st_ref[h] = sts[h]

    @pl.when(c == nc - 1)
    def _():
        sfin_ref[0] = st_ref[...]


def _chunks_per_step(seq_len, chunk):
    return 2 if seq_len % (2 * chunk) == 0 else 1


def _gla(gq, gk, gv, la, gg, s0t, nw, n_seq, seq_len, chunk):
    n_sub = _chunks_per_step(seq_len, chunk)
    nc = seq_len // (chunk * n_sub)
    t = n_seq * seq_len
    blk = lambda w: pl.BlockSpec((chunk * n_sub, w), lambda b, c: (b * nc + c, 0))
    st_spec = pl.BlockSpec((1, GLA_HEADS, GLA_DV, GLA_DK), lambda b, c: (b, 0, 0, 0))
    return pl.pallas_call(
        functools.partial(_gla_kernel, chunk=chunk, n_sub=n_sub),
        out_shape=[jax.ShapeDtypeStruct((t, GLA_V), F32),
                   jax.ShapeDtypeStruct((n_seq, GLA_HEADS, GLA_DV, GLA_DK), F32)],
        grid=(n_seq, nc),
        in_specs=[blk(GLA_QK), blk(GLA_QK), blk(GLA_V), blk(GLA_QK), blk(GLA_V), st_spec,
                  pl.BlockSpec((1, GLA_DV), lambda b, c: (0, 0))],
        out_specs=[blk(GLA_V), st_spec],
        scratch_shapes=[pltpu.VMEM((GLA_HEADS, GLA_DV, GLA_DK), F32)],
        compiler_params=_cparams(("parallel", "arbitrary"), 32),
        name="gla_scan",
    )(gq, gk, gv, la, gg, s0t, nw)


_INV_BASE = 16


def _unit_lower_inverses(lows, row, col, n):
    eye = (row == col).astype(F32)
    base = min(_INV_BASE, n)
    diag_blk = row // base == col // base
    pws = [jnp.where(diag_blk, low, 0.0) for low in lows]
    invs = [eye - ld for ld in pws]
    p = 2
    while p < base:
        pws = [_dot3(pw, pw) for pw in pws]
        invs = [inv + _dot3(inv, pw) for inv, pw in zip(invs, pws)]
        p *= 2
    b = base
    while b < n:
        sib = (row // (2 * b) == col // (2 * b)) & (row // b != col // b)
        tmps = [_dot3(jnp.where(sib, low, 0.0), inv) for low, inv in zip(lows, invs)]
        invs = [inv - _dot3(inv, t) for inv, t in zip(invs, tmps)]
        b *= 2
    return invs


def _gdn_kernel(x_ref, ab_ref, dz_ref, tail0_ref, cw_ref, s0_ref, nw_ref,
                o_ref, sfin_ref, tail_out_ref, st_ref, tail_ref, *, chunk, n_sub):
    c = pl.program_id(1)
    nc = pl.num_programs(1)

    @pl.when(c == 0)
    def _():
        st_ref[...] = s0_ref[0]
        tail_ref[...] = tail0_ref[0]

    n_rows = chunk * n_sub
    x = x_ref[...]
    ext = jnp.concatenate([tail_ref[...], x], axis=0)
    base = SUBLANES - (CONV_W - 1)
    conv = ext[base:base + n_rows] * cw_ref[0:1, :]
    for i in range(1, CONV_W):
        conv = conv + ext[base + i:base + i + n_rows] * cw_ref[i:i + 1, :]
    tail_ref[...] = x[n_rows - SUBLANES:n_rows]
    conv = conv * _sigmoid(conv)

    row = _iota2(chunk, chunk, 0)
    col = _iota2(chunk, chunk, 1)
    lower = row >= col
    tril = lower.astype(F32)

    heads = range(GDN_HEADS)
    subs = range(n_sub)
    qs, ks, vs, gcs, betas, decs = [], [], [], [], [], []
    for j in subs:
        rows = slice(j * chunk, (j + 1) * chunk)
        ab = ab_ref[rows, :]
        g_col = _dot_exact_lhs(tril, ab)
        g_row = _dot_nt_exact_rhs(ab.T, tril)
        for h in heads:
            cq = conv[rows, h * GDN_DK:(h + 1) * GDN_DK]
            ck = conv[rows, GDN_QK + h * GDN_DK:GDN_QK + (h + 1) * GDN_DK]
            vs.append(conv[rows, 2 * GDN_QK + h * GDN_DV:2 * GDN_QK + (h + 1) * GDN_DV])
            qs.append(cq * lax.rsqrt(jnp.sum(cq * cq, axis=-1, keepdims=True) + NORM_EPS) * (GDN_DK ** -0.5))
            ks.append(ck * lax.rsqrt(jnp.sum(ck * ck, axis=-1, keepdims=True) + NORM_EPS))
            gc = g_col[:, SM_A0 + h:SM_A0 + h + 1]
            gr = g_row[SM_A0 + h:SM_A0 + h + 1, :]
            gcs.append(gc)
            betas.append(ab[:, SM_B0 + h:SM_B0 + h + 1])
            decs.append(jnp.where(lower, jnp.exp(jnp.where(lower, gc - gr, 0.0)), 0.0))
    kks = [_dot3_nt(k, k) for k in ks]
    qks = [_dot3_nt(q, k) * dec for q, k, dec in zip(qs, ks, decs)]
    lows = [jnp.where(row > col, beta * kk * dec, 0.0) for beta, kk, dec in zip(betas, kks, decs)]
    tinvs = _unit_lower_inverses(lows, row, col, chunk)
    egs = [jnp.exp(gc) for gc in gcs]
    uws = [_dot3(tinv, jnp.concatenate([v * beta, k * (beta * eg)], axis=1))
           for tinv, v, k, beta, eg in zip(tinvs, vs, ks, betas, egs)]
    sts = [st_ref[h] for h in heads]
    for j in subs:
        rows = slice(j * chunk, (j + 1) * chunk)
        ids = [j * GDN_HEADS + h for h in heads]
        v_news = [uws[i][:, :GDN_DV] - _dot3(uws[i][:, GDN_DV:], st) for i, st in zip(ids, sts)]
        os_ = [_dot3(qs[i] * egs[i], st) + _dot3(qks[i], v_new) for i, st, v_new in zip(ids, sts, v_news)]
        new_sts = []
        for h, i in enumerate(ids):
            g_last = gcs[i][chunk - 1:chunk, :]
            new_sts.append(sts[h] * jnp.exp(g_last) + _dot3_tn(ks[i] * jnp.exp(g_last - gcs[i]), v_news[h]))
        sts = new_sts
        for h in heads:
            o = os_[h]
            ms = jnp.mean(o * o, axis=-1, keepdims=True)
            on = o * lax.rsqrt(ms + NORM_EPS) * nw_ref[...]
            gate = dz_ref[rows, h * GDN_DV:(h + 1) * GDN_DV]
            o_ref[rows, h * GDN_DV:(h + 1) * GDN_DV] = on * (gate * _sigmoid(gate))
    for h in heads:
        st_ref[h] = sts[h]

    @pl.when(c == nc - 1)
    def _():
        sfin_ref[0] = st_ref[...]
        tail_out_ref[0] = tail_ref[...]


def _gdn(dqkv, ab, dz, tail0, cw, s0, nw, n_seq, seq_len, chunk):
    n_sub = _chunks_per_step(seq_len, chunk)
    nc = seq_len // (chunk * n_sub)
    t = n_seq * seq_len
    blk = lambda w: pl.BlockSpec((chunk * n_sub, w), lambda b, c: (b * nc + c, 0))
    st_spec = pl.BlockSpec((1, GDN_HEADS, GDN_DK, GDN_DV), lambda b, c: (b, 0, 0, 0))
    tail_spec = pl.BlockSpec((1, SUBLANES, GDN_QKV), lambda b, c: (b, 0, 0))
    return pl.pallas_call(
        functools.partial(_gdn_kernel, chunk=chunk, n_sub=n_sub),
        out_shape=[jax.ShapeDtypeStruct((t, GDN_V), F32),
                   jax.ShapeDtypeStruct((n_seq, GDN_HEADS, GDN_DK, GDN_DV), F32),
                   jax.ShapeDtypeStruct((n_seq, SUBLANES, GDN_QKV), F32)],
        grid=(n_seq, nc),
        in_specs=[blk(GDN_QKV), blk(LANES), blk(GDN_V), tail_spec,
                  pl.BlockSpec((SUBLANES, GDN_QKV), lambda b, c: (0, 0)), st_spec,
                  pl.BlockSpec((1, GDN_DV), lambda b, c: (0, 0))],
        out_specs=[blk(GDN_V), st_spec, tail_spec],
        scratch_shapes=[pltpu.VMEM((GDN_HEADS, GDN_DK, GDN_DV), F32),
                        pltpu.VMEM((SUBLANES, GDN_QKV), F32)],
        compiler_params=_cparams(("parallel", "arbitrary"), 32),
        name="gdn_scan",
    )(dqkv, ab, dz, tail0, cw, s0, nw)


def _topk_cols(s, k, payload=None):
    n = s.shape[0]
    rid = lax.broadcasted_iota(jnp.int32, s.shape, 0).astype(F32)
    vals, idxs = [], []
    for _ in range(k):
        m = jnp.max(s, axis=0, keepdims=True)
        am = jnp.min(jnp.where(s == m, rid, float(n)), axis=0, keepdims=True)
        sel = rid == am
        vals.append(m)
        if payload is None:
            idxs.append(am)
        else:
            idxs.append(jnp.sum(jnp.where(sel, payload, 0.0), axis=0, keepdims=True))
        s = jnp.where(sel, -jnp.inf, s)
    return jnp.concatenate(vals, axis=0), jnp.concatenate(idxs, axis=0)


def _retrieve(s1, s2):
    v1, i1 = _topk_cols(s1, PEER_TOPK)
    v2, i2 = _topk_cols(s2, PEER_TOPK)
    cand_rows, cidx_rows = [], []
    for a in range(PEER_TOPK // 2):
        n_valid = PEER_TOPK // (a + 1)
        n_rows = -(-n_valid // SUBLANES) * SUBLANES
        c = v1[a:a + 1, :] + v2[0:n_rows, :]
        if n_valid < n_rows:
            c = jnp.where(lax.broadcasted_iota(jnp.int32, c.shape, 0) < n_valid, c, -jnp.inf)
        cand_rows.append(c)
        cidx_rows.append(i1[a:a + 1, :] * float(N_KEYS) + i2[0:n_rows, :])
    cand_rows.append(v1[PEER_TOPK // 2:, :] + v2[0:1, :])
    cidx_rows.append(i1[PEER_TOPK // 2:, :] * float(N_KEYS) + i2[0:1, :])
    sc, eidx = _topk_cols(jnp.concatenate(cand_rows, axis=0), PEER_TOPK,
                          payload=jnp.concatenate(cidx_rows, axis=0))
    e = jnp.exp(sc - jnp.max(sc, axis=0, keepdims=True))
    return eidx, e / jnp.sum(e, axis=0, keepdims=True)


def _post_kernel(o1_ref, o2_ref, x_ref, wo_ref, nw_ref, wq_ref, k1_ref, k2_ref,
                 x2_ref, hn_ref, j8_ref, sh_ref, gate_ref):
    mix = (_dot(o1_ref[...].astype(BF16), wo_ref[0:GLA_V, :])
           + _dot(o2_ref[...].astype(BF16), wo_ref[GLA_V:GLA_V + GDN_V, :]))
    x2 = x_ref[...] + mix
    x2_ref[...] = x2
    ms = jnp.mean(x2 * x2, axis=-1, keepdims=True)
    hn = x2 * lax.rsqrt(ms + NORM_EPS) * nw_ref[...]
    hn_ref[...] = hn
    qry = _dot(hn.astype(BF16), wq_ref[...])
    idx_rows, gate_rows = [], []
    for h in range(PEER_HEADS):
        q1 = qry[:, h * PEER_QDIM:h * PEER_QDIM + PEER_HALF].astype(BF16)
        q2 = qry[:, h * PEER_QDIM + PEER_HALF:(h + 1) * PEER_QDIM].astype(BF16)
        s1 = _dot_nt(k1_ref[h], q1)
        s2 = _dot_nt(k2_ref[h], q2)
        parts = [_retrieve(s1[:, c0:c0 + LANES], s2[:, c0:c0 + LANES]) for c0 in range(0, s1.shape[1], LANES)]
        idx_rows.append(jnp.concatenate([p[0] for p in parts], axis=1))
        gate_rows.append(jnp.concatenate([p[1] for p in parts], axis=1))
    e = jnp.concatenate(idx_rows, axis=0).T.astype(jnp.int32)
    j8_ref[...] = (e & (HALF_EXPERTS - 1)) * SUBLANES
    sh_ref[...] = jnp.where(e >= HALF_EXPERTS, 0.0, 16.0)
    gate_ref[...] = jnp.concatenate(gate_rows, axis=0).T


def _post(o1, o2, x, wo, nw, wq, k1, k2, tm):
    t, d = x.shape
    row = lambda w: pl.BlockSpec((tm, w), lambda i: (i, 0))
    full = lambda a: pl.BlockSpec(a.shape, lambda i: (0,) * a.ndim)
    return pl.pallas_call(
        _post_kernel,
        out_shape=[jax.ShapeDtypeStruct((t, d), F32), jax.ShapeDtypeStruct((t, d), F32),
                   jax.ShapeDtypeStruct((t, N_SEL), jnp.int32), jax.ShapeDtypeStruct((t, N_SEL), F32),
                   jax.ShapeDtypeStruct((t, N_SEL), F32)],
        grid=(t // tm,),
        in_specs=[row(GLA_V), row(GDN_V), row(d), full(wo), full(nw), full(wq), full(k1), full(k2)],
        out_specs=[row(d), row(d), row(N_SEL), row(N_SEL), row(N_SEL)],
        compiler_params=_cparams(("parallel",), 48),
        name="post_topk",
    )(o1, o2, x, wo, nw, wq, k1, k2)


_HI_MASK = -65536
_IDX_GROUP = 16


def _expert_row(tab_ref, j8, shift_row):
    words = tab_ref[pl.ds(pl.multiple_of(j8, SUBLANES), SUBLANES), :]
    bits = lax.shift_left(words, jnp.broadcast_to(shift_row, (SUBLANES, LANES))) & _HI_MASK
    return lax.bitcast_convert_type(bits, F32)


def _diag_rows(vals, eye):
    n = vals.shape[0]
    return (vals[:, None, :] * eye[None, :, :]).reshape(n * N_SEL, N_SEL)


def _add_bf16_pairs(a, b):
    return pltpu.bitcast(pltpu.bitcast(a, BF16) + pltpu.bitcast(b, BF16), jnp.int32)


def _fold_sublanes(vs, sub, add):
    m4 = sub < 4
    cur = []
    for a, b in zip(vs[0::2], vs[1::2]):
        cur.append(add(jnp.where(m4, a, b), pltpu.roll(jnp.where(m4, b, a), 4, 0)))
    for half in (2, 1):
        m = (sub % (2 * half)) < half
        nxt = []
        for a, b in zip(cur[0::2], cur[1::2]):
            ta = add(a, pltpu.roll(a, SUBLANES - half, 0))
            tb = add(b, pltpu.roll(b, half, 0))
            nxt.append(jnp.where(m, ta, tb))
        cur = nxt
    return cur[0]


def _eye():
    return (_iota2(N_SEL, N_SEL, 0) == _iota2(N_SEL, N_SEL, 1)).astype(F32)


def _peer_u_kernel(j8_ref, sh_ref, hn_ref, gate_ref, tab_ref, w_ref, *scratch, tb):
    q_refs = scratch[:_IDX_GROUP]
    idx_ref, sem = scratch[_IDX_GROUP:]
    sub = lax.broadcasted_iota(jnp.int32, (SUBLANES, LANES), 0)
    ones = jnp.ones((LANES, LANES), BF16)
    perm = (0, 4, 2, 6, 1, 5, 3, 7)
    half_of_row = _iota2(2 * N_SEL, N_SEL, 0) - 2 * _iota2(2 * N_SEL, N_SEL, 1)

    def finish(src_ref, out_tok):
        rep = _dot(pltpu.bitcast(src_ref[...], BF16), ones)
        high = jnp.where(sh_ref[pl.ds(out_tok, 1), :] == 0.0, 1, 0)
        act = jnp.sum(jnp.where(half_of_row == high, rep, 0.0), axis=0, keepdims=True)
        gelu = 0.5 * act * (1.0 + jnp.tanh(0.7978845608028654 * (act + 0.044715 * act * act * act)))
        w_ref[pl.ds(out_tok, 1), :] = gate_ref[pl.ds(out_tok, 1), :] * gelu

    n_groups = tb // _IDX_GROUP

    def idx_copy(g, buf):
        src = j8_ref.at[pl.ds(g * _IDX_GROUP, _IDX_GROUP)]
        return pltpu.make_async_copy(src, idx_ref.at[buf], sem.at[buf])

    lag = 2
    for q_ref in q_refs[len(q_refs) - lag:]:
        q_ref[...] = jnp.zeros((N_SEL, LANES), jnp.int32)
    idx_copy(0, 0).start()

    def body(i, carry):
        for buf in range(2):
            g = 2 * i + buf
            idx_copy(g, buf).wait()

            @pl.when(g + 1 < n_groups)
            def _():
                idx_copy(g + 1, 1 - buf).start()

            for s in range(_IDX_GROUP):
                t = g * _IDX_GROUP + s
                finish(q_refs[(s - lag) % _IDX_GROUP], jnp.maximum(t - lag, 0))
                xi = lax.bitcast_convert_type(hn_ref[t].astype(BF16).astype(F32), jnp.int32)
                xx = pltpu.bitcast(xi | lax.shift_right_logical(xi, 16), BF16)
                folded = []
                for grp in range(N_SEL // SUBLANES):
                    prods = []
                    for r in range(SUBLANES):
                        k = grp * SUBLANES + perm[r]
                        row0 = pl.multiple_of(idx_ref[buf, s, 0, k], SUBLANES)
                        words = tab_ref[pl.ds(row0, SUBLANES), :]
                        prods.append(pltpu.bitcast(pltpu.bitcast(words, BF16) * xx, jnp.int32))
                    folded.append(_fold_sublanes(prods, sub, _add_bf16_pairs))
                q_refs[s][...] = jnp.concatenate(folded, axis=0)
        return carry

    lax.fori_loop(0, n_groups // 2, body, 0)
    for t in range(tb - lag, tb):
        finish(q_refs[t % _IDX_GROUP], t)


def _peer_u(j8, sh, hn3, gate, tab, tb):
    t = j8.shape[0]
    vec = lambda: pl.BlockSpec((tb, N_SEL), lambda i: (i, 0))
    return pl.pallas_call(
        functools.partial(_peer_u_kernel, tb=tb),
        out_shape=jax.ShapeDtypeStruct((t, N_SEL), F32),
        grid=(t // tb,),
        in_specs=[pl.BlockSpec((tb, 1, N_SEL), lambda i: (i, 0, 0)), vec(),
                  pl.BlockSpec((tb, SUBLANES, LANES), lambda i: (i, 0, 0)), vec(),
                  pl.BlockSpec(memory_space=pltpu.VMEM)],
        out_specs=vec(),
        scratch_shapes=[pltpu.VMEM((N_SEL, LANES), jnp.int32) for _ in range(_IDX_GROUP)]
        + [pltpu.SMEM((2, _IDX_GROUP, 1, N_SEL), jnp.int32), pltpu.SemaphoreType.DMA((2,))],
        compiler_params=_cparams(("arbitrary",), 56),
        name="peer_u",
    )(j8.reshape(t, 1, N_SEL), sh, hn3, gate, tab)


def _peer_v_kernel(j8_ref, sh_ref, wgt_ref, x2_ref, nw_ref, tab_ref, y_ref, wp_ref, idx_ref, sem,
                   *, tb, n_acc):
    eye = _eye()
    blk = (_iota2(2 * N_SEL, 2 * LANES, 0) // N_SEL == _iota2(2 * N_SEL, 2 * LANES, 1) // LANES).astype(BF16)

    def pre(c, carry):
        tok0 = pl.multiple_of(c * SUBLANES, SUBLANES)
        toks = pl.ds(tok0, SUBLANES)
        w = wgt_ref[toks, :]
        low = sh_ref[toks, :] > 0.0
        diag = jnp.concatenate([_diag_rows(jnp.where(low, w, 0.0), eye).astype(BF16),
                                _diag_rows(jnp.where(low, 0.0, w), eye).astype(BF16)], axis=1)
        rep = lax.bitcast_convert_type(_dot(diag, blk), jnp.int32)
        dst = pl.ds(pl.multiple_of(tok0 * N_SEL, SUBLANES * N_SEL), SUBLANES * N_SEL)
        wp_ref[dst, :] = rep[:, LANES:] | lax.shift_right_logical(rep[:, :LANES], 16)
        return carry

    n_groups = tb // _IDX_GROUP

    def idx_copy(g, buf):
        src = j8_ref.at[pl.ds(g * _IDX_GROUP, _IDX_GROUP)]
        return pltpu.make_async_copy(src, idx_ref.at[buf], sem.at[buf])

    idx_copy(0, 0).start()
    lax.fori_loop(0, tb // SUBLANES, pre, 0)
    group4 = 4

    def group_pair(i, carry):
        for buf in range(2):
            g = 2 * i + buf
            idx_copy(g, buf).wait()

            @pl.when(g + 1 < n_groups)
            def _():
                idx_copy(g + 1, 1 - buf).start()

            for s in range(_IDX_GROUP):
                t = g * _IDX_GROUP + s
                base = t * N_SEL
                accs = [jnp.zeros((SUBLANES, LANES), F32) for _ in range(n_acc)]
                for k0 in range(0, N_SEL, group4):
                    part = None
                    for k in range(k0, k0 + group4):
                        row0 = pl.multiple_of(idx_ref[buf, s, 0, k], SUBLANES)
                        words = pltpu.bitcast(tab_ref[pl.ds(row0, SUBLANES), :], BF16)
                        wpair = jnp.broadcast_to(wp_ref[pl.ds(base + k, 1), :], (SUBLANES, LANES))
                        prod = words * pltpu.bitcast(wpair, BF16)
                        part = prod if part is None else part + prod
                    bits = pltpu.bitcast(part, jnp.int32)
                    a = (k0 // group4) % (n_acc // 2)
                    accs[2 * a] = accs[2 * a] + lax.bitcast_convert_type(lax.shift_left(bits, 16), F32)
                    accs[2 * a + 1] = accs[2 * a + 1] + lax.bitcast_convert_type(bits & _HI_MASK, F32)
                tot = accs[0]
                for a in accs[1:]:
                    tot = tot + a
                y_ref[t] = x2_ref[t] + tot
        return carry

    lax.fori_loop(0, n_groups // 2, group_pair, 0)
    x3 = y_ref[...]
    ss = jnp.sum(jnp.sum(x3 * x3, axis=2, keepdims=True), axis=1, keepdims=True)
    y_ref[...] = x3 * lax.rsqrt(ss * (1.0 / (SUBLANES * LANES)) + NORM_EPS) * nw_ref[...][None]


def _peer_v(j8, sh, wgt, x23, nw3, tab, tb):
    t = j8.shape[0]
    vec = lambda: pl.BlockSpec((tb, N_SEL), lambda i: (i, 0))
    return pl.pallas_call(
        functools.partial(_peer_v_kernel, tb=tb, n_acc=4),
        out_shape=jax.ShapeDtypeStruct((t, SUBLANES, LANES), F32),
        grid=(t // tb,),
        in_specs=[pl.BlockSpec((tb, 1, N_SEL), lambda i: (i, 0, 0)), vec(), vec(),
                  pl.BlockSpec((tb, SUBLANES, LANES), lambda i: (i, 0, 0)),
                  pl.BlockSpec((SUBLANES, LANES), lambda i: (0, 0)),
                  pl.BlockSpec(memory_space=pltpu.VMEM)],
        out_specs=pl.BlockSpec((tb, SUBLANES, LANES), lambda i: (i, 0, 0)),
        scratch_shapes=[pltpu.VMEM((tb * N_SEL, LANES), jnp.int32),
                        pltpu.SMEM((2, _IDX_GROUP, 1, N_SEL), jnp.int32),
                        pltpu.SemaphoreType.DMA((2,))],
        compiler_params=_cparams(("arbitrary",), 56),
        name="peer_v",
    )(j8.reshape(t, 1, N_SEL), sh, wgt, x23, nw3, tab)


_PACK_ROWS = 256


def _pack_kernel(lo_ref, hi_ref, o_ref):
    lo = lax.bitcast_convert_type(lo_ref[...].astype(BF16).astype(F32), jnp.int32)
    hi = lax.bitcast_convert_type(hi_ref[...].astype(BF16).astype(F32), jnp.int32)
    packed = hi | lax.shift_right_logical(lo, 16)
    rows, d = packed.shape
    for s in range(d // LANES):
        o_ref[pl.ds(s, rows, stride=d // LANES), :] = packed[:, s * LANES:(s + 1) * LANES]


def _pack_table(tab):
    n, d = tab.shape
    sub = d // LANES
    steps = n // 2 // _PACK_ROWS
    return pl.pallas_call(
        _pack_kernel,
        out_shape=jax.ShapeDtypeStruct((n // 2 * sub, LANES), jnp.int32),
        grid=(steps,),
        in_specs=[pl.BlockSpec((_PACK_ROWS, d), lambda i: (i, 0)),
                  pl.BlockSpec((_PACK_ROWS, d), lambda i: (i + steps, 0))],
        out_specs=pl.BlockSpec((_PACK_ROWS * sub, LANES), lambda i: (i, 0)),
        compiler_params=_cparams(("arbitrary",), 32),
        name="pack_table",
    )(tab, tab)


def _prep_weights(norm_mix_w, w_in, gla_w_gk2, gla_b_gk, gla_norm_w, gdn_conv_w, gdn_a_log,
                  gdn_dt_bias, gdn_norm_w, w_out, norm_ffn_w, peer_wq, peer_k1, peer_k2, peer_u, peer_v):
    d = w_in.shape[0]
    o_glr = 2 * GLA_QK + GLA_V
    o_gg = o_glr + GLA_LR
    o_qkv = o_gg + GLA_V
    o_da = o_qkv + GDN_QKV
    o_dz = o_da + 2 * GDN_HEADS
    small = jnp.concatenate([w_in[:, o_glr:o_gg], w_in[:, o_da:o_dz],
                             jnp.zeros((d, LANES - GLA_LR - 2 * GDN_HEADS), w_in.dtype)], axis=1)
    wcat = jnp.concatenate([w_in[:, :o_glr], w_in[:, o_gg:o_qkv], w_in[:, o_qkv:o_da],
                            w_in[:, o_dz:], small], axis=1).astype(BF16)
    w2p = jnp.zeros((LANES, GLA_QK), F32).at[:GLA_LR].set(gla_w_gk2).astype(BF16)
    avec = jnp.zeros((1, LANES), F32).at[0, SM_A0:SM_B0].set(-jnp.exp(gdn_a_log))
    dtb = jnp.zeros((1, LANES), F32).at[0, SM_A0:SM_B0].set(gdn_dt_bias)
    cw = jnp.zeros((SUBLANES, GDN_QKV), F32).at[:CONV_W].set(gdn_conv_w)
    return dict(
        nmix=norm_mix_w.reshape(1, d), wcat=wcat, w2p=w2p, b2=gla_b_gk.reshape(1, GLA_QK), avec=avec, dtb=dtb,
        gla_nw=gla_norm_w.reshape(1, GLA_DV), cw=cw, gdn_nw=gdn_norm_w.reshape(1, GDN_DV),
        wo=w_out.astype(BF16), nffn=norm_ffn_w.reshape(1, d), wq=peer_wq.astype(BF16),
        k1=peer_k1.astype(BF16), k2=peer_k2.astype(BF16), tab_u=_pack_table(peer_u), tab_v=_pack_table(peer_v))


def _layer(x, s_gla, s_gdn, conv_buf, w, nfw3, chunk, tm, tb):
    n_seq, seq_len, d = x.shape
    t = n_seq * seq_len
    xf = x.reshape(t, d)
    gq, gk, gv, gg, dqkv, dz, la, sm = _inproj(xf, w["nmix"], w["wcat"], w["w2p"], w["b2"], w["avec"],
                                               w["dtb"], tm)
    o1, gla_t = _gla(gq, gk, gv, la, gg, jnp.swapaxes(s_gla, -1, -2), w["gla_nw"], n_seq, seq_len, chunk)
    tail0 = jnp.pad(conv_buf, ((0, 0), (SUBLANES - (CONV_W - 1), 0), (0, 0)))
    o2, gdn_new, tail = _gdn(dqkv, sm, dz, tail0, w["cw"], s_gdn, w["gdn_nw"], n_seq, seq_len, chunk)
    x2, hn, j8, sh, gate = _post(o1, o2, xf, w["wo"], w["nffn"], w["wq"], w["k1"], w["k2"], tm)
    wgt = _peer_u(j8, sh, hn.reshape(t, SUBLANES, LANES), gate, w["tab_u"], tb)
    y = _peer_v(j8, sh, wgt, x2.reshape(t, SUBLANES, LANES), nfw3, w["tab_v"], tb)
    return (y.reshape(n_seq, seq_len, d), jnp.swapaxes(gla_t, -1, -2), gdn_new,
            tail[:, SUBLANES - (CONV_W - 1):, :])


def kernel(x_prompt, x_sample, state_gla, state_gdn, state_gdn_conv, norm_mix_w, w_in, gla_w_gk2, gla_b_gk,
           gla_norm_w, gdn_conv_w, gdn_a_log, gdn_dt_bias, gdn_norm_w, w_out, norm_ffn_w, peer_wq, peer_k1,
           peer_k2, peer_u, peer_v, norm_final_w):
    depth = w_in.shape[0]
    assert depth == 1, "the final norm is fused into the last layer's PEER pass"
    n_p, l_p, d = x_prompt.shape
    n_s, l_s, _ = x_sample.shape
    nfw3 = norm_final_w.reshape(SUBLANES, LANES)
    w = _prep_weights(norm_mix_w[0], w_in[0], gla_w_gk2[0], gla_b_gk[0], gla_norm_w[0], gdn_conv_w[0],
                      gdn_a_log[0], gdn_dt_bias[0], gdn_norm_w[0], w_out[0], norm_ffn_w[0], peer_wq[0],
                      peer_k1[0], peer_k2[0], peer_u[0], peer_v[0])
    z_gla = jnp.zeros((n_p, GLA_HEADS, GLA_DK, GLA_DV), F32)
    z_gdn = jnp.zeros((n_p, GDN_HEADS, GDN_DK, GDN_DV), F32)
    z_conv = jnp.zeros((n_p, CONV_W - 1, GDN_QKV), F32)
    chunk_p = min(CHUNK, l_p)
    chunk_s = min(CHUNK, l_s)
    tm_p = min(256, n_p * l_p)
    tm_s = min(256, n_s * l_s)
    yp, gla_p, gdn_p, conv_p = _layer(x_prompt, z_gla, z_gdn, z_conv, w, nfw3, chunk_p, tm_p, min(128, tm_p))
    ys, gla_s, gdn_s, conv_s = _layer(x_sample, state_gla[0], state_gdn[0], state_gdn_conv[0], w, nfw3,
                                      chunk_s, tm_s, min(128, tm_s))
    return (yp, ys, gla_p[None], gdn_p[None], conv_p[None], gla_s[None], gdn_s[None], conv_s[None])
```

```python
import functools

import jax
import jax.numpy as jnp
from jax import lax
from jax.experimental import pallas as pl
from jax.experimental.pallas import tpu as pltpu

F32 = jnp.float32
BF16 = jnp.bfloat16

NORM_EPS = 1e-6
CHUNK = 64
GLA_HEADS, GLA_DK, GLA_DV, GLA_LR, GLA_GATE_TAU = 4, 64, 128, 16, 16.0
GDN_HEADS, GDN_DK, GDN_DV, CONV_W = 4, 128, 128, 4
GLA_QK, GLA_V = GLA_HEADS * GLA_DK, GLA_HEADS * GLA_DV
GDN_QK, GDN_V = GDN_HEADS * GDN_DK, GDN_HEADS * GDN_DV
GDN_QKV = 2 * GDN_QK + GDN_V
PEER_HEADS, PEER_QDIM, N_KEYS, PEER_TOPK = 8, 256, 128, 16
PEER_HALF = PEER_QDIM // 2
N_SEL = PEER_HEADS * PEER_TOPK
HALF_EXPERTS = N_KEYS * N_KEYS // 2

LANES = 128
SUBLANES = 8
MIB = 1024 * 1024

SM_A0 = GLA_LR
SM_B0 = GLA_LR + GDN_HEADS


def _cparams(sem, vmem_mib):
    return pltpu.CompilerParams(dimension_semantics=sem, vmem_limit_bytes=vmem_mib * MIB)


def _softplus(x):
    return jnp.maximum(x, 0.0) + jnp.log(1.0 + jnp.exp(-jnp.abs(x)))


def _sigmoid(x):
    return 1.0 / (1.0 + jnp.exp(-x))


def _dot(a, b, prec=None):
    return jnp.dot(a, b, preferred_element_type=F32, precision=prec)


def _dot_nt(a, b, prec=None):
    return lax.dot_general(a, b, (((1,), (1,)), ((), ())), preferred_element_type=F32, precision=prec)


def _dot_tn(a, b, prec=None):
    return lax.dot_general(a, b, (((0,), (0,)), ((), ())), preferred_element_type=F32, precision=prec)


def _split_bf16(x):
    hi = x.astype(BF16)
    return hi, (x - hi.astype(F32)).astype(BF16)


def _split3_bf16(x):
    hi = x.astype(BF16)
    r = x - hi.astype(F32)
    mid = r.astype(BF16)
    return hi, mid, (r - mid.astype(F32)).astype(BF16)


def _dot3(a, b):
    ah, al = _split_bf16(a)
    bh, bl = _split_bf16(b)
    return _dot(jnp.concatenate([ah, ah, al], axis=1), jnp.concatenate([bh, bl, bh], axis=0))


def _dot3_nt(a, b):
    ah, al = _split_bf16(a)
    bh, bl = _split_bf16(b)
    return _dot_nt(jnp.concatenate([ah, ah, al], axis=1), jnp.concatenate([bh, bl, bh], axis=1))


def _dot3_tn(a, b):
    ah, al = _split_bf16(a)
    bh, bl = _split_bf16(b)
    return _dot_tn(jnp.concatenate([ah, ah, al], axis=0), jnp.concatenate([bh, bl, bh], axis=0))


def _dot_exact_lhs(a01, b):
    a16 = a01.astype(BF16)
    return _dot(jnp.concatenate([a16, a16, a16], axis=1), jnp.concatenate(_split3_bf16(b), axis=0))


def _dot_nt_exact_rhs(a, b01):
    b16 = b01.astype(BF16)
    return _dot_nt(jnp.concatenate(_split3_bf16(a), axis=1), jnp.concatenate([b16, b16, b16], axis=1))


_W_COLS = (("gq", GLA_QK), ("gk", GLA_QK), ("gv", GLA_V), ("gg", GLA_V), ("dqkv", GDN_QKV),
           ("dz", GDN_V), ("sm", LANES))


def _inproj_kernel(x_ref, nw_ref, w_ref, w2_ref, b2_ref, avec_ref, dtb_ref,
                   gq_ref, gk_ref, gv_ref, gg_ref, dqkv_ref, dz_ref, la_ref, sm_ref):
    x = x_ref[...]
    ms = jnp.mean(x * x, axis=-1, keepdims=True)
    h = (x * lax.rsqrt(ms + NORM_EPS) * nw_ref[...]).astype(BF16)
    outs = dict(gq=gq_ref, gk=gk_ref, gv=gv_ref, gg=gg_ref, dqkv=dqkv_ref, dz=dz_ref)
    off = 0
    ps = None
    for name, width in _W_COLS:
        p = _dot(h, w_ref[:, off:off + width])
        off += width
        if name == "sm":
            ps = p
        elif name == "gq":
            gq_ref[...] = p * (GLA_DK ** -0.5)
        else:
            outs[name][...] = p
    z = _dot(ps.astype(BF16), w2_ref[...]) + b2_ref[...]
    la_ref[...] = -_softplus(-z) * (1.0 / GLA_GATE_TAU)
    lane = lax.broadcasted_iota(jnp.int32, ps.shape, 1)
    log_a = avec_ref[...] * _softplus(ps + dtb_ref[...])
    beta = _sigmoid(ps)
    is_a = (lane >= SM_A0) & (lane < SM_B0)
    is_b = (lane >= SM_B0) & (lane < SM_B0 + GDN_HEADS)
    sm_ref[...] = jnp.where(is_a, log_a, jnp.where(is_b, beta, ps))


def _inproj(x, nw, wcat, w2p, b2, avec, dtb, tm):
    t, d = x.shape
    n_w = wcat.shape[1]
    widths = dict(_W_COLS)
    names = ("gq", "gk", "gv", "gg", "dqkv", "dz")
    out_shape = [jax.ShapeDtypeStruct((t, widths[n]), F32) for n in names]
    out_shape += [jax.ShapeDtypeStruct((t, GLA_QK), F32), jax.ShapeDtypeStruct((t, LANES), F32)]
    row = lambda w: pl.BlockSpec((tm, w), lambda i: (i, 0))
    full = lambda a: pl.BlockSpec(a.shape, lambda i: (0,) * a.ndim)
    return pl.pallas_call(
        _inproj_kernel,
        out_shape=out_shape,
        grid=(t // tm,),
        in_specs=[row(d), full(nw), full(wcat), full(w2p), full(b2), full(avec), full(dtb)],
        out_specs=[row(widths[n]) for n in names] + [row(GLA_QK), row(LANES)],
        compiler_params=_cparams(("parallel",), 48),
        name="inproj",
    )(x, nw, wcat, w2p, b2, avec, dtb)


def _iota2(n, m, axis):
    return lax.broadcasted_iota(jnp.int32, (n, m), axis)


def _gla_kernel(q_ref, k_ref, v_ref, la_ref, gg_ref, s0_ref, nw_ref, o_ref, sfin_ref, st_ref, *, chunk, n_sub):
    c = pl.program_id(1)
    nc = pl.num_programs(1)

    @pl.when(c == 0)
    def _():
        st_ref[...] = s0_ref[0]

    row = _iota2(chunk, chunk, 0)
    col = _iota2(chunk, chunk, 1)
    subs = range(n_sub)
    rows = [slice(j * chunk, (j + 1) * chunk) for j in subs]

    blocks = []
    b = chunk // 2
    while b >= 1:
        blocks.append(b)
        b //= 2
    sel = [row >= col]
    for b in blocks:
        sel.append(col <= (row // b) * b)
        sel.append(col <= jnp.minimum((row // b + 1) * b, chunk - 1))
    la_wide = jnp.concatenate([la_ref[r, :] for r in rows], axis=1)
    g_wide = _dot_exact_lhs(jnp.concatenate([m.astype(F32) for m in sel], axis=0), la_wide)

    heads = range(GLA_HEADS)
    dks = [slice(h * GLA_DK, (h + 1) * GLA_DK) for h in heads]
    dvs = [slice(h * GLA_DV, (h + 1) * GLA_DV) for h in heads]
    prep = []
    for j in subs:
        g_all = g_wide[:, j * GLA_QK:(j + 1) * GLA_QK]
        q = q_ref[rows[j], :]
        k = k_ref[rows[j], :]
        g = g_all[0:chunk]
        levels = []
        for li, b in enumerate(blocks):
            ref_r = g_all[(1 + 2 * li) * chunk:(2 + 2 * li) * chunk]
            ref_c = g_all[(2 + 2 * li) * chunk:(3 + 2 * li) * chunk]
            mask = (row // (2 * b) == col // (2 * b)) & (row // b == col // b + 1)
            levels.append((q * jnp.exp(g - ref_r), k * jnp.exp(ref_c - g), mask))
        g_last = g[chunk - 1:chunk, :]
        prep.append(dict(levels=levels, qk_diag=q * k, q_in=q * jnp.exp(g), k_out=k * jnp.exp(g_last - g),
                         decay_out=jnp.exp(g_last)))
    atts = [[jnp.zeros((chunk, chunk), F32) for _ in heads] for _ in subs]
    for li in range(len(blocks)):
        for j in subs:
            ql, kl, mask = prep[j]["levels"][li]
            atts[j] = [att + jnp.where(mask, _dot3_nt(ql[:, dk], kl[:, dk]), 0.0)
                       for att, dk in zip(atts[j], dks)]
    vhs = [[v_ref[rows[j], dv] for dv in dvs] for j in subs]
    intra = [[_dot3(atts[j][h], vhs[j][h])
              + jnp.sum(prep[j]["qk_diag"][:, dks[h]], axis=-1, keepdims=True) * vhs[j][h]
              for h in heads] for j in subs]
    kvs = [[_dot3_tn(vhs[j][h], prep[j]["k_out"][:, dks[h]]) for h in heads] for j in subs]
    sts = [st_ref[h] for h in heads]
    for j in subs:
        os_ = [_dot3_nt(prep[j]["q_in"][:, dks[h]], sts[h]) + intra[j][h] for h in heads]
        sts = [sts[h] * prep[j]["decay_out"][:, dks[h]] + kvs[j][h] for h in heads]
        for h in heads:
            o = os_[h]
            ms = jnp.mean(o * o, axis=-1, keepdims=True)
            on = o * lax.rsqrt(ms + NORM_EPS) * nw_ref[...]
            gate = gg_ref[rows[j], dvs[h]]
            o_ref[rows[j], dvs[h]] = on * (gate * _sigmoid(gate))
    for h in heads:
        st_ref[h] = sts[h]

    @pl.when(c == nc - 1)
    def _():
        sfin_ref[0] = st_ref[...]


def _chunks_per_step(seq_len, chunk):
    return 2 if seq_len % (2 * chunk) == 0 else 1


def _gla(gq, gk, gv, la, gg, s0t, nw, n_seq, seq_len, chunk):
    n_sub = _chunks_per_step(seq_len, chunk)
    nc = seq_len // (chunk * n_sub)
    t = n_seq * seq_len
    blk = lambda w: pl.BlockSpec((chunk * n_sub, w), lambda b, c: (b * nc + c, 0))
    st_spec = pl.BlockSpec((1, GLA_HEADS, GLA_DV, GLA_DK), lambda b, c: (b, 0, 0, 0))
    return pl.pallas_call(
        functools.partial(_gla_kernel, chunk=chunk, n_sub=n_sub),
        out_shape=[jax.ShapeDtypeStruct((t, GLA_V), F32),
                   jax.ShapeDtypeStruct((n_seq, GLA_HEADS, GLA_DV, GLA_DK), F32)],
        grid=(n_seq, nc),
        in_specs=[blk(GLA_QK), blk(GLA_QK), blk(GLA_V), blk(GLA_QK), blk(GLA_V), st_spec,
                  pl.BlockSpec((1, GLA_DV), lambda b, c: (0, 0))],
        out_specs=[blk(GLA_V), st_spec],
        scratch_shapes=[pltpu.VMEM((GLA_HEADS, GLA_DV, GLA_DK), F32)],
        compiler_params=_cparams(("parallel", "arbitrary"), 32),
        name="gla_scan",
    )(gq, gk, gv, la, gg, s0t, nw)


_INV_BASE = 16


def _unit_lower_inverses(lows, row, col, n):
    eye = (row == col).astype(F32)
    base = min(_INV_BASE, n)
    diag_blk = row // base == col // base
    pws = [jnp.where(diag_blk, low, 0.0) for low in lows]
    invs = [eye - ld for ld in pws]
    p = 2
    while p < base:
        pws = [_dot3(pw, pw) for pw in pws]
        invs = [inv + _dot3(inv, pw) for inv, pw in zip(invs, pws)]
        p *= 2
    b = base
    while b < n:
        sib = (row // (2 * b) == col // (2 * b)) & (row // b != col // b)
        tmps = [_dot3(jnp.where(sib, low, 0.0), inv) for low, inv in zip(lows, invs)]
        invs = [inv - _dot3(inv, t) for inv, t in zip(invs, tmps)]
        b *= 2
    return invs


def _gdn_kernel(x_ref, ab_ref, dz_ref, tail0_ref, cw_ref, s0_ref, nw_ref,
                o_ref, sfin_ref, tail_out_ref, st_ref, tail_ref, *, chunk, n_sub):
    c = pl.program_id(1)
    nc = pl.num_programs(1)

    @pl.when(c == 0)
    def _():
        st_ref[...] = s0_ref[0]
        tail_ref[...] = tail0_ref[0]

    n_rows = chunk * n_sub
    x = x_ref[...]
    ext = jnp.concatenate([tail_ref[...], x], axis=0)
    base = SUBLANES - (CONV_W - 1)
    conv = ext[base:base + n_rows] * cw_ref[0:1, :]
    for i in range(1, CONV_W):
        conv = conv + ext[base + i:base + i + n_rows] * cw_ref[i:i + 1, :]
    tail_ref[...] = x[n_rows - SUBLANES:n_rows]
    conv = conv * _sigmoid(conv)

    row = _iota2(chunk, chunk, 0)
    col = _iota2(chunk, chunk, 1)
    lower = row >= col
    tril = lower.astype(F32)

    heads = range(GDN_HEADS)
    subs = range(n_sub)
    qs, ks, vs, gcs, betas, decs = [], [], [], [], [], []
    for j in subs:
        rows = slice(j * chunk, (j + 1) * chunk)
        ab = ab_ref[rows, :]
        g_col = _dot_exact_lhs(tril, ab)
        g_row = _dot_nt_exact_rhs(ab.T, tril)
        for h in heads:
            cq = conv[rows, h * GDN_DK:(h + 1) * GDN_DK]
            ck = conv[rows, GDN_QK + h * GDN_DK:GDN_QK + (h + 1) * GDN_DK]
            vs.append(conv[rows, 2 * GDN_QK + h * GDN_DV:2 * GDN_QK + (h + 1) * GDN_DV])
            qs.append(cq * lax.rsqrt(jnp.sum(cq * cq, axis=-1, keepdims=True) + NORM_EPS) * (GDN_DK ** -0.5))
            ks.append(ck * lax.rsqrt(jnp.sum(ck * ck, axis=-1, keepdims=True) + NORM_EPS))
            gc = g_col[:, SM_A0 + h:SM_A0 + h + 1]
            gr = g_row[SM_A0 + h:SM_A0 + h + 1, :]
            gcs.append(gc)
            betas.append(ab[:, SM_B0 + h:SM_B0 + h + 1])
            decs.append(jnp.where(lower, jnp.exp(jnp.where(lower, gc - gr, 0.0)), 0.0))
    kks = [_dot3_nt(k, k) for k in ks]
    qks = [_dot3_nt(q, k) * dec for q, k, dec in zip(qs, ks, decs)]
    lows = [jnp.where(row > col, beta * kk * dec, 0.0) for beta, kk, dec in zip(betas, kks, decs)]
    tinvs = _unit_lower_inverses(lows, row, col, chunk)
    egs = [jnp.exp(gc) for gc in gcs]
    uws = [_dot3(tinv, jnp.concatenate([v * beta, k * (beta * eg)], axis=1))
           for tinv, v, k, beta, eg in zip(tinvs, vs, ks, betas, egs)]
    sts = [st_ref[h] for h in heads]
    for j in subs:
        rows = slice(j * chunk, (j + 1) * chunk)
        ids = [j * GDN_HEADS + h for h in heads]
        v_news = [uws[i][:, :GDN_DV] - _dot3(uws[i][:, GDN_DV:], st) for i, st in zip(ids, sts)]
        os_ = [_dot3(qs[i] * egs[i], st) + _dot3(qks[i], v_new) for i, st, v_new in zip(ids, sts, v_news)]
        new_sts = []
        for h, i in enumerate(ids):
            g_last = gcs[i][chunk - 1:chunk, :]
            new_sts.append(sts[h] * jnp.exp(g_last) + _dot3_tn(ks[i] * jnp.exp(g_last - gcs[i]), v_news[h]))
        sts = new_sts
        for h in heads:
            o = os_[h]
            ms = jnp.mean(o * o, axis=-1, keepdims=True)
            on = o * lax.rsqrt(ms + NORM_EPS) * nw_ref[...]
            gate = dz_ref[rows, h * GDN_DV:(h + 1) * GDN_DV]
            o_ref[rows, h * GDN_DV:(h + 1) * GDN_DV] = on * (gate * _sigmoid(gate))
    for h in heads:
        st_ref[h] = sts[h]

    @pl.when(c == nc - 1)
    def _():
        sfin_ref[0] = st_ref[...]
        tail_out_ref[0] = tail_ref[...]


def _gdn(dqkv, ab, dz, tail0, cw, s0, nw, n_seq, seq_len, chunk):
    n_sub = _chunks_per_step(seq_len, chunk)
    nc = seq_len // (chunk * n_sub)
    t = n_seq * seq_len
    blk = lambda w: pl.BlockSpec((chunk * n_sub, w), lambda b, c: (b * nc + c, 0))
    st_spec = pl.BlockSpec((1, GDN_HEADS, GDN_DK, GDN_DV), lambda b, c: (b, 0, 0, 0))
    tail_spec = pl.BlockSpec((1, SUBLANES, GDN_QKV), lambda b, c: (b, 0, 0))
    return pl.pallas_call(
        functools.partial(_gdn_kernel, chunk=chunk, n_sub=n_sub),
        out_shape=[jax.ShapeDtypeStruct((t, GDN_V), F32),
                   jax.ShapeDtypeStruct((n_seq, GDN_HEADS, GDN_DK, GDN_DV), F32),
                   jax.ShapeDtypeStruct((n_seq, SUBLANES, GDN_QKV), F32)],
        grid=(n_seq, nc),
        in_specs=[blk(GDN_QKV), blk(LANES), blk(GDN_V), tail_spec,
                  pl.BlockSpec((SUBLANES, GDN_QKV), lambda b, c: (0, 0)), st_spec,
                  pl.BlockSpec((1, GDN_DV), lambda b, c: (0, 0))],
        out_specs=[blk(GDN_V), st_spec, tail_spec],
        scratch_shapes=[pltpu.VMEM((GDN_HEADS, GDN_DK, GDN_DV), F32),
                        pltpu.VMEM((SUBLANES, GDN_QKV), F32)],
        compiler_params=_cparams(("parallel", "arbitrary"), 32),
        name="gdn_scan",
    )(dqkv, ab, dz, tail0, cw, s0, nw)


def _topk_cols(s, k, payload=None):
    n = s.shape[0]
    rid = lax.broadcasted_iota(jnp.int32, s.shape, 0).astype(F32)
    vals, idxs = [], []
    for _ in range(k):
        m = jnp.max(s, axis=0, keepdims=True)
        am = jnp.min(jnp.where(s == m, rid, float(n)), axis=0, keepdims=True)
        sel = rid == am
        vals.append(m)
        if payload is None:
            idxs.append(am)
        else:
            idxs.append(jnp.sum(jnp.where(sel, payload, 0.0), axis=0, keepdims=True))
        s = jnp.where(sel, -jnp.inf, s)
    return jnp.concatenate(vals, axis=0), jnp.concatenate(idxs, axis=0)


def _retrieve(s1, s2):
    v1, i1 = _topk_cols(s1, PEER_TOPK)
    v2, i2 = _topk_cols(s2, PEER_TOPK)
    cand_rows, cidx_rows = [], []
    for a in range(PEER_TOPK // 2):
        n_valid = PEER_TOPK // (a + 1)
        n_rows = -(-n_valid // SUBLANES) * SUBLANES
        c = v1[a:a + 1, :] + v2[0:n_rows, :]
        if n_valid < n_rows:
            c = jnp.where(lax.broadcasted_iota(jnp.int32, c.shape, 0) < n_valid, c, -jnp.inf)
        cand_rows.append(c)
        cidx_rows.append(i1[a:a + 1, :] * float(N_KEYS) + i2[0:n_rows, :])
    cand_rows.append(v1[PEER_TOPK // 2:, :] + v2[0:1, :])
    cidx_rows.append(i1[PEER_TOPK // 2:, :] * float(N_KEYS) + i2[0:1, :])
    sc, eidx = _topk_cols(jnp.concatenate(cand_rows, axis=0), PEER_TOPK,
                          payload=jnp.concatenate(cidx_rows, axis=0))
    e = jnp.exp(sc - jnp.max(sc, axis=0, keepdims=True))
    return eidx, e / jnp.sum(e, axis=0, keepdims=True)


def _post_kernel(o1_ref, o2_ref, x_ref, wo_ref, nw_ref, wq_ref, k1_ref, k2_ref,
                 x2_ref, hn_ref, j8_ref, sh_ref, gate_ref):
    mix = (_dot(o1_ref[...].astype(BF16), wo_ref[0:GLA_V, :])
           + _dot(o2_ref[...].astype(BF16), wo_ref[GLA_V:GLA_V + GDN_V, :]))
    x2 = x_ref[...] + mix
    x2_ref[...] = x2
    ms = jnp.mean(x2 * x2, axis=-1, keepdims=True)
    hn = x2 * lax.rsqrt(ms + NORM_EPS) * nw_ref[...]
    n_chunk = hn.shape[1] // LANES
    for c in range(n_chunk):
        hn_ref[pl.ds(c, hn.shape[0], stride=n_chunk), :] = hn[:, c * LANES:(c + 1) * LANES]
    qry = _dot(hn.astype(BF16), wq_ref[...])
    idx_rows, gate_rows = [], []
    for h in range(PEER_HEADS):
        q1 = qry[:, h * PEER_QDIM:h * PEER_QDIM + PEER_HALF].astype(BF16)
        q2 = qry[:, h * PEER_QDIM + PEER_HALF:(h + 1) * PEER_QDIM].astype(BF16)
        s1 = _dot_nt(k1_ref[h], q1)
        s2 = _dot_nt(k2_ref[h], q2)
        parts = [_retrieve(s1[:, c0:c0 + LANES], s2[:, c0:c0 + LANES]) for c0 in range(0, s1.shape[1], LANES)]
        idx_rows.append(jnp.concatenate([p[0] for p in parts], axis=1))
        gate_rows.append(jnp.concatenate([p[1] for p in parts], axis=1))
    e = jnp.concatenate(idx_rows, axis=0).T.astype(jnp.int32)
    j8_ref[...] = (e & (HALF_EXPERTS - 1)) * SUBLANES
    sh_ref[...] = jnp.where(e >= HALF_EXPERTS, 0.0, 16.0)
    gate_ref[...] = jnp.concatenate(gate_rows, axis=0).T


def _post(o1, o2, x, wo, nw, wq, k1, k2, tm):
    t, d = x.shape
    row = lambda w: pl.BlockSpec((tm, w), lambda i: (i, 0))
    full = lambda a: pl.BlockSpec(a.shape, lambda i: (0,) * a.ndim)
    return pl.pallas_call(
        _post_kernel,
        out_shape=[jax.ShapeDtypeStruct((t, d), F32), jax.ShapeDtypeStruct((t * (d // LANES), LANES), F32),
                   jax.ShapeDtypeStruct((t, N_SEL), jnp.int32), jax.ShapeDtypeStruct((t, N_SEL), F32),
                   jax.ShapeDtypeStruct((t, N_SEL), F32)],
        grid=(t // tm,),
        in_specs=[row(GLA_V), row(GDN_V), row(d), full(wo), full(nw), full(wq), full(k1), full(k2)],
        out_specs=[row(d), pl.BlockSpec((tm * (d // LANES), LANES), lambda i: (i, 0)),
                   row(N_SEL), row(N_SEL), row(N_SEL)],
        compiler_params=_cparams(("parallel",), 48),
        name="post_topk",
    )(o1, o2, x, wo, nw, wq, k1, k2)


_HI_MASK = -65536
_IDX_GROUP = 16


def _diag_rows(vals, eye):
    n = vals.shape[0]
    return (vals[:, None, :] * eye[None, :, :]).reshape(n * N_SEL, N_SEL)


def _add_bf16_pairs(a, b):
    return pltpu.bitcast(pltpu.bitcast(a, BF16) + pltpu.bitcast(b, BF16), jnp.int32)


def _fold_sublanes(vs, sub, add):
    m4 = sub < 4
    cur = []
    for a, b in zip(vs[0::2], vs[1::2]):
        cur.append(add(jnp.where(m4, a, b), pltpu.roll(jnp.where(m4, b, a), 4, 0)))
    for half in (2, 1):
        m = (sub % (2 * half)) < half
        nxt = []
        for a, b in zip(cur[0::2], cur[1::2]):
            ta = add(a, pltpu.roll(a, SUBLANES - half, 0))
            tb = add(b, pltpu.roll(b, half, 0))
            nxt.append(jnp.where(m, ta, tb))
        cur = nxt
    return cur[0]


def _eye():
    return (_iota2(N_SEL, N_SEL, 0) == _iota2(N_SEL, N_SEL, 1)).astype(F32)


def _peer_u_kernel(j8_ref, sh_ref, hn_ref, gate_ref, tab_ref, w_ref, *scratch, tb):
    q_refs = scratch[:_IDX_GROUP]
    idx_ref, sem = scratch[_IDX_GROUP:]
    sub = lax.broadcasted_iota(jnp.int32, (SUBLANES, LANES), 0)
    ones = jnp.ones((LANES, LANES), BF16)
    perm = (0, 4, 2, 6, 1, 5, 3, 7)
    half_of_row = _iota2(2 * N_SEL, N_SEL, 0) - 2 * _iota2(2 * N_SEL, N_SEL, 1)

    def finish(src_ref, out_tok):
        rep = _dot(pltpu.bitcast(src_ref[...], BF16), ones)
        high = jnp.where(sh_ref[pl.ds(out_tok, 1), :] == 0.0, 1, 0)
        act = jnp.sum(jnp.where(half_of_row == high, rep, 0.0), axis=0, keepdims=True)
        gelu = 0.5 * act * (1.0 + jnp.tanh(0.7978845608028654 * (act + 0.044715 * act * act * act)))
        w_ref[pl.ds(out_tok, 1), :] = gate_ref[pl.ds(out_tok, 1), :] * gelu

    n_groups = tb // _IDX_GROUP

    def idx_copy(g, buf):
        src = j8_ref.at[pl.ds(g * _IDX_GROUP, _IDX_GROUP)]
        return pltpu.make_async_copy(src, idx_ref.at[buf], sem.at[buf])

    lag = 2
    for q_ref in q_refs[len(q_refs) - lag:]:
        q_ref[...] = jnp.zeros((N_SEL, LANES), jnp.int32)
    idx_copy(0, 0).start()

    def body(i, carry):
        for buf in range(2):
            g = 2 * i + buf
            idx_copy(g, buf).wait()

            @pl.when(g + 1 < n_groups)
            def _():
                idx_copy(g + 1, 1 - buf).start()

            for s in range(_IDX_GROUP):
                t = g * _IDX_GROUP + s
                finish(q_refs[(s - lag) % _IDX_GROUP], jnp.maximum(t - lag, 0))
                xi = lax.bitcast_convert_type(hn_ref[t].astype(BF16).astype(F32), jnp.int32)
                xx = pltpu.bitcast(xi | lax.shift_right_logical(xi, 16), BF16)
                folded = []
                for grp in range(N_SEL // SUBLANES):
                    prods = []
                    for r in range(SUBLANES):
                        k = grp * SUBLANES + perm[r]
                        row0 = pl.multiple_of(idx_ref[buf, s, 0, k], SUBLANES)
                        words = tab_ref[pl.ds(row0, SUBLANES), :]
                        prods.append(pltpu.bitcast(pltpu.bitcast(words, BF16) * xx, jnp.int32))
                    folded.append(_fold_sublanes(prods, sub, _add_bf16_pairs))
                q_refs[s][...] = jnp.concatenate(folded, axis=0)
        return carry

    lax.fori_loop(0, n_groups // 2, body, 0)
    for t in range(tb - lag, tb):
        finish(q_refs[t % _IDX_GROUP], t)


def _peer_u(j8, sh, hn3, gate, tab, tb):
    t = j8.shape[0]
    vec = lambda: pl.BlockSpec((tb, N_SEL), lambda i: (i, 0))
    return pl.pallas_call(
        functools.partial(_peer_u_kernel, tb=tb),
        out_shape=jax.ShapeDtypeStruct((t, N_SEL), F32),
        grid=(t // tb,),
        in_specs=[pl.BlockSpec((tb, 1, N_SEL), lambda i: (i, 0, 0)), vec(),
                  pl.BlockSpec((tb, SUBLANES, LANES), lambda i: (i, 0, 0)), vec(),
                  pl.BlockSpec(memory_space=pltpu.VMEM)],
        out_specs=vec(),
        scratch_shapes=[pltpu.VMEM((N_SEL, LANES), jnp.int32) for _ in range(_IDX_GROUP)]
        + [pltpu.SMEM((2, _IDX_GROUP, 1, N_SEL), jnp.int32), pltpu.SemaphoreType.DMA((2,))],
        compiler_params=_cparams(("arbitrary",), 56),
        name="peer_u",
    )(j8.reshape(t, 1, N_SEL), sh, hn3, gate, tab)


def _peer_v_kernel(j8_ref, sh_ref, wgt_ref, x2_ref, nw_ref, tab_ref, y_ref, wp_ref, acc_ref, idx_ref, sem,
                   *, tb, n_acc):
    eye = _eye()
    blk = (_iota2(2 * N_SEL, 2 * LANES, 0) // N_SEL == _iota2(2 * N_SEL, 2 * LANES, 1) // LANES).astype(BF16)

    def pre(c, carry):
        tok0 = pl.multiple_of(c * SUBLANES, SUBLANES)
        toks = pl.ds(tok0, SUBLANES)
        w = wgt_ref[toks, :]
        low = sh_ref[toks, :] > 0.0
        diag = jnp.concatenate([_diag_rows(jnp.where(low, w, 0.0), eye).astype(BF16),
                                _diag_rows(jnp.where(low, 0.0, w), eye).astype(BF16)], axis=1)
        rep = lax.bitcast_convert_type(_dot(diag, blk), jnp.int32)
        dst = pl.ds(pl.multiple_of(tok0 * N_SEL, SUBLANES * N_SEL), SUBLANES * N_SEL)
        wp_ref[dst, :] = rep[:, LANES:] | lax.shift_right_logical(rep[:, :LANES], 16)
        return carry

    n_groups = tb // _IDX_GROUP

    def idx_copy(g, buf):
        src = j8_ref.at[pl.ds(g * _IDX_GROUP, _IDX_GROUP)]
        return pltpu.make_async_copy(src, idx_ref.at[buf], sem.at[buf])

    idx_copy(0, 0).start()
    lax.fori_loop(0, tb // SUBLANES, pre, 0)
    group4 = 4

    def group_pair(i, carry):
        for buf in range(2):
            g = 2 * i + buf
            idx_copy(g, buf).wait()

            @pl.when(g + 1 < n_groups)
            def _():
                idx_copy(g + 1, 1 - buf).start()

            for s in range(_IDX_GROUP):
                t = g * _IDX_GROUP + s
                base = t * N_SEL
                accs = [jnp.zeros((SUBLANES, LANES), F32) for _ in range(n_acc)]
                for k0 in range(0, N_SEL, group4):
                    part = None
                    for k in range(k0, k0 + group4):
                        row0 = pl.multiple_of(idx_ref[buf, s, 0, k], SUBLANES)
                        words = pltpu.bitcast(tab_ref[pl.ds(row0, SUBLANES), :], BF16)
                        wpair = jnp.broadcast_to(wp_ref[pl.ds(base + k, 1), :], (SUBLANES, LANES))
                        prod = words * pltpu.bitcast(wpair, BF16)
                        part = prod if part is None else part + prod
                    bits = pltpu.bitcast(part, jnp.int32)
                    a = (k0 // group4) % (n_acc // 2)
                    accs[2 * a] = accs[2 * a] + lax.bitcast_convert_type(lax.shift_left(bits, 16), F32)
                    accs[2 * a + 1] = accs[2 * a + 1] + lax.bitcast_convert_type(bits & _HI_MASK, F32)
                tot = accs[0]
                for a in accs[1:]:
                    tot = tot + a
                acc_ref[pl.ds(pl.multiple_of(t * SUBLANES, SUBLANES), SUBLANES), :] = tot
        return carry

    lax.fori_loop(0, n_groups // 2, group_pair, 0)
    d = x2_ref.shape[1]
    x3 = [x2_ref[:, c * LANES:(c + 1) * LANES] + acc_ref[pl.ds(c, tb, stride=SUBLANES), :]
          for c in range(SUBLANES)]
    ss = x3[0] * x3[0]
    for xc in x3[1:]:
        ss = ss + xc * xc
    scale = lax.rsqrt(jnp.sum(ss, axis=-1, keepdims=True) * (1.0 / d) + NORM_EPS)
    for c, xc in enumerate(x3):
        y_ref[:, c * LANES:(c + 1) * LANES] = xc * scale * nw_ref[:, c * LANES:(c + 1) * LANES]


def _peer_v(j8, sh, wgt, x2, nw, tab, tb):
    t, d = x2.shape
    vec = lambda: pl.BlockSpec((tb, N_SEL), lambda i: (i, 0))
    return pl.pallas_call(
        functools.partial(_peer_v_kernel, tb=tb, n_acc=4),
        out_shape=jax.ShapeDtypeStruct((t, d), F32),
        grid=(t // tb,),
        in_specs=[pl.BlockSpec((tb, 1, N_SEL), lambda i: (i, 0, 0)), vec(), vec(),
                  pl.BlockSpec((tb, d), lambda i: (i, 0)),
                  pl.BlockSpec((1, d), lambda i: (0, 0)),
                  pl.BlockSpec(memory_space=pltpu.VMEM)],
        out_specs=pl.BlockSpec((tb, d), lambda i: (i, 0)),
        scratch_shapes=[pltpu.VMEM((tb * N_SEL, LANES), jnp.int32), pltpu.VMEM((tb * SUBLANES, LANES), F32),
                        pltpu.SMEM((2, _IDX_GROUP, 1, N_SEL), jnp.int32),
                        pltpu.SemaphoreType.DMA((2,))],
        compiler_params=_cparams(("arbitrary",), 56),
        name="peer_v",
    )(j8.reshape(t, 1, N_SEL), sh, wgt, x2, nw, tab)


_PACK_ROWS = 256


def _pack_kernel(lo_ref, hi_ref, o_ref):
    lo = lax.bitcast_convert_type(lo_ref[...].astype(BF16).astype(F32), jnp.int32)
    hi = lax.bitcast_convert_type(hi_ref[...].astype(BF16).astype(F32), jnp.int32)
    packed = hi | lax.shift_right_logical(lo, 16)
    rows, d = packed.shape
    for s in range(d // LANES):
        o_ref[pl.ds(s, rows, stride=d // LANES), :] = packed[:, s * LANES:(s + 1) * LANES]


def _pack_table(tab):
    n, d = tab.shape
    sub = d // LANES
    steps = n // 2 // _PACK_ROWS
    return pl.pallas_call(
        _pack_kernel,
        out_shape=jax.ShapeDtypeStruct((n // 2 * sub, LANES), jnp.int32),
        grid=(steps,),
        in_specs=[pl.BlockSpec((_PACK_ROWS, d), lambda i: (i, 0)),
                  pl.BlockSpec((_PACK_ROWS, d), lambda i: (i + steps, 0))],
        out_specs=pl.BlockSpec((_PACK_ROWS * sub, LANES), lambda i: (i, 0)),
        compiler_params=_cparams(("arbitrary",), 32),
        name="pack_table",
    )(tab, tab)


def _prep_weights(norm_mix_w, w_in, gla_w_gk2, gla_b_gk, gla_norm_w, gdn_conv_w, gdn_a_log,
                  gdn_dt_bias, gdn_norm_w, w_out, norm_ffn_w, peer_wq, peer_k1, peer_k2, peer_u, peer_v):
    d = w_in.shape[0]
    o_glr = 2 * GLA_QK + GLA_V
    o_gg = o_glr + GLA_LR
    o_qkv = o_gg + GLA_V
    o_da = o_qkv + GDN_QKV
    o_dz = o_da + 2 * GDN_HEADS
    small = jnp.concatenate([w_in[:, o_glr:o_gg], w_in[:, o_da:o_dz],
                             jnp.zeros((d, LANES - GLA_LR - 2 * GDN_HEADS), w_in.dtype)], axis=1)
    wcat = jnp.concatenate([w_in[:, :o_glr], w_in[:, o_gg:o_qkv], w_in[:, o_qkv:o_da],
                            w_in[:, o_dz:], small], axis=1).astype(BF16)
    w2p = jnp.zeros((LANES, GLA_QK), F32).at[:GLA_LR].set(gla_w_gk2).astype(BF16)
    avec = jnp.zeros((1, LANES), F32).at[0, SM_A0:SM_B0].set(-jnp.exp(gdn_a_log))
    dtb = jnp.zeros((1, LANES), F32).at[0, SM_A0:SM_B0].set(gdn_dt_bias)
    cw = jnp.zeros((SUBLANES, GDN_QKV), F32).at[:CONV_W].set(gdn_conv_w)
    return dict(
        nmix=norm_mix_w.reshape(1, d), wcat=wcat, w2p=w2p, b2=gla_b_gk.reshape(1, GLA_QK), avec=avec, dtb=dtb,
        gla_nw=gla_norm_w.reshape(1, GLA_DV), cw=cw, gdn_nw=gdn_norm_w.reshape(1, GDN_DV),
        wo=w_out.astype(BF16), nffn=norm_ffn_w.reshape(1, d), wq=peer_wq.astype(BF16),
        k1=peer_k1.astype(BF16), k2=peer_k2.astype(BF16), tab_u=_pack_table(peer_u), tab_v=_pack_table(peer_v))


def _layer(x, s_gla, s_gdn, conv_buf, w, nfw, chunk, tm, tb):
    n_seq, seq_len, d = x.shape
    t = n_seq * seq_len
    xf = x.reshape(t, d)
    gq, gk, gv, gg, dqkv, dz, la, sm = _inproj(xf, w["nmix"], w["wcat"], w["w2p"], w["b2"], w["avec"],
                                               w["dtb"], tm)
    o1, gla_t = _gla(gq, gk, gv, la, gg, jnp.swapaxes(s_gla, -1, -2), w["gla_nw"], n_seq, seq_len, chunk)
    tail0 = jnp.pad(conv_buf, ((0, 0), (SUBLANES - (CONV_W - 1), 0), (0, 0)))
    o2, gdn_new, tail = _gdn(dqkv, sm, dz, tail0, w["cw"], s_gdn, w["gdn_nw"], n_seq, seq_len, chunk)
    x2, hn, j8, sh, gate = _post(o1, o2, xf, w["wo"], w["nffn"], w["wq"], w["k1"], w["k2"], tm)
    wgt = _peer_u(j8, sh, hn.reshape(t, SUBLANES, LANES), gate, w["tab_u"], tb)
    y = _peer_v(j8, sh, wgt, x2, nfw, w["tab_v"], tb)
    return (y.reshape(n_seq, seq_len, d), jnp.swapaxes(gla_t, -1, -2), gdn_new,
            tail[:, SUBLANES - (CONV_W - 1):, :])


def kernel(x_prompt, x_sample, state_gla, state_gdn, state_gdn_conv, norm_mix_w, w_in, gla_w_gk2, gla_b_gk,
           gla_norm_w, gdn_conv_w, gdn_a_log, gdn_dt_bias, gdn_norm_w, w_out, norm_ffn_w, peer_wq, peer_k1,
           peer_k2, peer_u, peer_v, norm_final_w):
    depth = w_in.shape[0]
    assert depth == 1, "the final norm is fused into the last layer's PEER pass"
    n_p, l_p, d = x_prompt.shape
    n_s, l_s, _ = x_sample.shape
    nfw = norm_final_w.reshape(1, d)
    w = _prep_weights(norm_mix_w[0], w_in[0], gla_w_gk2[0], gla_b_gk[0], gla_norm_w[0], gdn_conv_w[0],
                      gdn_a_log[0], gdn_dt_bias[0], gdn_norm_w[0], w_out[0], norm_ffn_w[0], peer_wq[0],
                      peer_k1[0], peer_k2[0], peer_u[0], peer_v[0])
    z_gla = jnp.zeros((n_p, GLA_HEADS, GLA_DK, GLA_DV), F32)
    z_gdn = jnp.zeros((n_p, GDN_HEADS, GDN_DK, GDN_DV), F32)
    z_conv = jnp.zeros((n_p, CONV_W - 1, GDN_QKV), F32)
    chunk_p = min(CHUNK, l_p)
    chunk_s = min(CHUNK, l_s)
    tm_p = min(256, n_p * l_p)
    tm_s = min(256, n_s * l_s)
    yp, gla_p, gdn_p, conv_p = _layer(x_prompt, z_gla, z_gdn, z_conv, w, nfw, chunk_p, tm_p, min(128, tm_p))
    ys, gla_s, gdn_s, conv_s = _layer(x_sample, state_gla[0], state_gdn[0], state_gdn_conv[0], w, nfw,
                                      chunk_s, tm_s, min(128, tm_s))
    return (yp, ys, gla_p[None], gdn_p[None], conv_p[None], gla_s[None], gdn_s[None], conv_s[None])
```

```python
import functools

import jax
import jax.numpy as jnp
from jax import lax
from jax.experimental import pallas as pl
from jax.experimental.pallas import tpu as pltpu

F32 = jnp.float32
BF16 = jnp.bfloat16

NORM_EPS = 1e-6
CHUNK = 64
GLA_HEADS, GLA_DK, GLA_DV, GLA_LR, GLA_GATE_TAU = 4, 64, 128, 16, 16.0
GDN_HEADS, GDN_DK, GDN_DV, CONV_W = 4, 128, 128, 4
GLA_QK, GLA_V = GLA_HEADS * GLA_DK, GLA_HEADS * GLA_DV
GDN_QK, GDN_V = GDN_HEADS * GDN_DK, GDN_HEADS * GDN_DV
GDN_QKV = 2 * GDN_QK + GDN_V
PEER_HEADS, PEER_QDIM, N_KEYS, PEER_TOPK = 8, 256, 128, 16
PEER_HALF = PEER_QDIM // 2
N_SEL = PEER_HEADS * PEER_TOPK
HALF_EXPERTS = N_KEYS * N_KEYS // 2

LANES = 128
SUBLANES = 8
MIB = 1024 * 1024

SM_A0 = GLA_LR
SM_B0 = GLA_LR + GDN_HEADS


def _cparams(sem, vmem_mib):
    return pltpu.CompilerParams(dimension_semantics=sem, vmem_limit_bytes=vmem_mib * MIB)


def _softplus(x):
    return jnp.maximum(x, 0.0) + jnp.log(1.0 + jnp.exp(-jnp.abs(x)))


def _sigmoid(x):
    return 1.0 / (1.0 + jnp.exp(-x))


def _dot(a, b, prec=None):
    return jnp.dot(a, b, preferred_element_type=F32, precision=prec)


def _dot_nt(a, b, prec=None):
    return lax.dot_general(a, b, (((1,), (1,)), ((), ())), preferred_element_type=F32, precision=prec)


def _dot_tn(a, b, prec=None):
    return lax.dot_general(a, b, (((0,), (0,)), ((), ())), preferred_element_type=F32, precision=prec)


def _split_bf16(x):
    hi = x.astype(BF16)
    return hi, (x - hi.astype(F32)).astype(BF16)


def _split3_bf16(x):
    hi = x.astype(BF16)
    r = x - hi.astype(F32)
    mid = r.astype(BF16)
    return hi, mid, (r - mid.astype(F32)).astype(BF16)


def _dot3(a, b):
    ah, al = _split_bf16(a)
    bh, bl = _split_bf16(b)
    return _dot(jnp.concatenate([ah, ah, al], axis=1), jnp.concatenate([bh, bl, bh], axis=0))


def _dot3_nt(a, b):
    ah, al = _split_bf16(a)
    bh, bl = _split_bf16(b)
    return _dot_nt(jnp.concatenate([ah, ah, al], axis=1), jnp.concatenate([bh, bl, bh], axis=1))


def _dot3_tn(a, b):
    ah, al = _split_bf16(a)
    bh, bl = _split_bf16(b)
    return _dot_tn(jnp.concatenate([ah, ah, al], axis=0), jnp.concatenate([bh, bl, bh], axis=0))


def _dot_exact_lhs(a01, b):
    a16 = a01.astype(BF16)
    return _dot(jnp.concatenate([a16, a16, a16], axis=1), jnp.concatenate(_split3_bf16(b), axis=0))


def _dot_nt_exact_rhs(a, b01):
    b16 = b01.astype(BF16)
    return _dot_nt(jnp.concatenate(_split3_bf16(a), axis=1), jnp.concatenate([b16, b16, b16], axis=1))


_W_COLS = (("gq", GLA_QK), ("gk", GLA_QK), ("gv", GLA_V), ("gg", GLA_V), ("dqkv", GDN_QKV),
           ("dz", GDN_V), ("sm", LANES))


def _inproj_kernel(x_ref, nw_ref, w_ref, w2_ref, b2_ref, avec_ref, dtb_ref,
                   gq_ref, gk_ref, gv_ref, gg_ref, dqkv_ref, dz_ref, la_ref, sm_ref):
    x = x_ref[...]
    ms = jnp.mean(x * x, axis=-1, keepdims=True)
    h = (x * lax.rsqrt(ms + NORM_EPS) * nw_ref[...]).astype(BF16)
    outs = dict(gq=gq_ref, gk=gk_ref, gv=gv_ref, gg=gg_ref, dqkv=dqkv_ref, dz=dz_ref)
    off = 0
    ps = None
    for name, width in _W_COLS:
        p = _dot(h, w_ref[:, off:off + width])
        off += width
        if name == "sm":
            ps = p
        elif name == "gq":
            gq_ref[...] = p * (GLA_DK ** -0.5)
        else:
            outs[name][...] = p
    z = _dot(ps.astype(BF16), w2_ref[...]) + b2_ref[...]
    la_ref[...] = -_softplus(-z) * (1.0 / GLA_GATE_TAU)
    lane = lax.broadcasted_iota(jnp.int32, ps.shape, 1)
    log_a = avec_ref[...] * _softplus(ps + dtb_ref[...])
    beta = _sigmoid(ps)
    is_a = (lane >= SM_A0) & (lane < SM_B0)
    is_b = (lane >= SM_B0) & (lane < SM_B0 + GDN_HEADS)
    sm_ref[...] = jnp.where(is_a, log_a, jnp.where(is_b, beta, ps))


def _inproj(x, nw, wcat, w2p, b2, avec, dtb, tm):
    t, d = x.shape
    n_w = wcat.shape[1]
    widths = dict(_W_COLS)
    names = ("gq", "gk", "gv", "gg", "dqkv", "dz")
    out_shape = [jax.ShapeDtypeStruct((t, widths[n]), F32) for n in names]
    out_shape += [jax.ShapeDtypeStruct((t, GLA_QK), F32), jax.ShapeDtypeStruct((t, LANES), F32)]
    row = lambda w: pl.BlockSpec((tm, w), lambda i: (i, 0))
    full = lambda a: pl.BlockSpec(a.shape, lambda i: (0,) * a.ndim)
    return pl.pallas_call(
        _inproj_kernel,
        out_shape=out_shape,
        grid=(t // tm,),
        in_specs=[row(d), full(nw), full(wcat), full(w2p), full(b2), full(avec), full(dtb)],
        out_specs=[row(widths[n]) for n in names] + [row(GLA_QK), row(LANES)],
        compiler_params=_cparams(("parallel",), 48),
        name="inproj",
    )(x, nw, wcat, w2p, b2, avec, dtb)


def _iota2(n, m, axis):
    return lax.broadcasted_iota(jnp.int32, (n, m), axis)


def _gla_kernel(q_ref, k_ref, v_ref, la_ref, gg_ref, s0_ref, nw_ref, o_ref, sfin_ref, st_ref, *, chunk, n_sub):
    c = pl.program_id(1)
    nc = pl.num_programs(1)

    @pl.when(c == 0)
    def _():
        st_ref[...] = s0_ref[0]

    row = _iota2(chunk, chunk, 0)
    col = _iota2(chunk, chunk, 1)
    subs = range(n_sub)
    rows = [slice(j * chunk, (j + 1) * chunk) for j in subs]

    blocks = []
    b = chunk // 2
    while b >= 1:
        blocks.append(b)
        b //= 2
    sel = [row >= col]
    for b in blocks:
        sel.append(col <= (row // b) * b)
        sel.append(col <= jnp.minimum((row // b + 1) * b, chunk - 1))
    la_wide = jnp.concatenate([la_ref[r, :] for r in rows], axis=1)
    g_wide = _dot_exact_lhs(jnp.concatenate([m.astype(F32) for m in sel], axis=0), la_wide)

    heads = range(GLA_HEADS)
    dks = [slice(h * GLA_DK, (h + 1) * GLA_DK) for h in heads]
    dvs = [slice(h * GLA_DV, (h + 1) * GLA_DV) for h in heads]
    prep = []
    for j in subs:
        g_all = g_wide[:, j * GLA_QK:(j + 1) * GLA_QK]
        q = q_ref[rows[j], :]
        k = k_ref[rows[j], :]
        g = g_all[0:chunk]
        levels = []
        for li, b in enumerate(blocks):
            ref_r = g_all[(1 + 2 * li) * chunk:(2 + 2 * li) * chunk]
            ref_c = g_all[(2 + 2 * li) * chunk:(3 + 2 * li) * chunk]
            mask = (row // (2 * b) == col // (2 * b)) & (row // b == col // b + 1)
            levels.append((q * jnp.exp(g - ref_r), k * jnp.exp(ref_c - g), mask))
        g_last = g[chunk - 1:chunk, :]
        prep.append(dict(levels=levels, qk_diag=q * k, q_in=q * jnp.exp(g), k_out=k * jnp.exp(g_last - g),
                         decay_out=jnp.exp(g_last)))
    atts = [[jnp.zeros((chunk, chunk), F32) for _ in heads] for _ in subs]
    for li in range(len(blocks)):
        for j in subs:
            ql, kl, mask = prep[j]["levels"][li]
            atts[j] = [att + jnp.where(mask, _dot3_nt(ql[:, dk], kl[:, dk]), 0.0)
                       for att, dk in zip(atts[j], dks)]
    vhs = [[v_ref[rows[j], dv] for dv in dvs] for j in subs]
    intra = [[_dot3(atts[j][h], vhs[j][h])
              + jnp.sum(prep[j]["qk_diag"][:, dks[h]], axis=-1, keepdims=True) * vhs[j][h]
              for h in heads] for j in subs]
    kvs = [[_dot3_tn(vhs[j][h], prep[j]["k_out"][:, dks[h]]) for h in heads] for j in subs]
    sts = [st_ref[h] for h in heads]
    for j in subs:
        os_ = [_dot3_nt(prep[j]["q_in"][:, dks[h]], sts[h]) + intra[j][h] for h in heads]
        sts = [sts[h] * prep[j]["decay_out"][:, dks[h]] + kvs[j][h] for h in heads]
        for h in heads:
            o = os_[h]
            ms = jnp.mean(o * o, axis=-1, keepdims=True)
            on = o * lax.rsqrt(ms + NORM_EPS) * nw_ref[...]
            gate = gg_ref[rows[j], dvs[h]]
            o_ref[rows[j], dvs[h]] = on * (gate * _sigmoid(gate))
    for h in heads:
        st_ref[h] = sts[h]

    @pl.when(c == nc - 1)
    def _():
        sfin_ref[0] = st_ref[...]


def _chunks_per_step(seq_len, chunk):
    for n_sub in (4, 2):
        if seq_len % (n_sub * chunk) == 0:
            return n_sub
    return 1


def _gla(gq, gk, gv, la, gg, s0t, nw, n_seq, seq_len, chunk):
    n_sub = _chunks_per_step(seq_len, chunk)
    nc = seq_len // (chunk * n_sub)
    t = n_seq * seq_len
    blk = lambda w: pl.BlockSpec((chunk * n_sub, w), lambda b, c: (b * nc + c, 0))
    st_spec = pl.BlockSpec((1, GLA_HEADS, GLA_DV, GLA_DK), lambda b, c: (b, 0, 0, 0))
    return pl.pallas_call(
        functools.partial(_gla_kernel, chunk=chunk, n_sub=n_sub),
        out_shape=[jax.ShapeDtypeStruct((t, GLA_V), F32),
                   jax.ShapeDtypeStruct((n_seq, GLA_HEADS, GLA_DV, GLA_DK), F32)],
        grid=(n_seq, nc),
        in_specs=[blk(GLA_QK), blk(GLA_QK), blk(GLA_V), blk(GLA_QK), blk(GLA_V), st_spec,
                  pl.BlockSpec((1, GLA_DV), lambda b, c: (0, 0))],
        out_specs=[blk(GLA_V), st_spec],
        scratch_shapes=[pltpu.VMEM((GLA_HEADS, GLA_DV, GLA_DK), F32)],
        compiler_params=_cparams(("parallel", "arbitrary"), 32),
        name="gla_scan",
    )(gq, gk, gv, la, gg, s0t, nw)


_INV_BASE = 16


def _unit_lower_inverses(lows, row, col, n):
    eye = (row == col).astype(F32)
    base = min(_INV_BASE, n)
    diag_blk = row // base == col // base
    pws = [jnp.where(diag_blk, low, 0.0) for low in lows]
    invs = [eye - ld for ld in pws]
    p = 2
    while p < base:
        pws = [_dot3(pw, pw) for pw in pws]
        invs = [inv + _dot3(inv, pw) for inv, pw in zip(invs, pws)]
        p *= 2
    b = base
    while b < n:
        sib = (row // (2 * b) == col // (2 * b)) & (row // b != col // b)
        tmps = [_dot3(jnp.where(sib, low, 0.0), inv) for low, inv in zip(lows, invs)]
        invs = [inv - _dot3(inv, t) for inv, t in zip(invs, tmps)]
        b *= 2
    return invs


def _gdn_kernel(x_ref, ab_ref, dz_ref, tail0_ref, cw_ref, s0_ref, nw_ref,
                o_ref, sfin_ref, tail_out_ref, st_ref, tail_ref, *, chunk, n_sub):
    c = pl.program_id(1)
    nc = pl.num_programs(1)

    @pl.when(c == 0)
    def _():
        st_ref[...] = s0_ref[0]
        tail_ref[...] = tail0_ref[0]

    n_rows = chunk * n_sub
    x = x_ref[...]
    ext = jnp.concatenate([tail_ref[...], x], axis=0)
    base = SUBLANES - (CONV_W - 1)
    conv = ext[base:base + n_rows] * cw_ref[0:1, :]
    for i in range(1, CONV_W):
        conv = conv + ext[base + i:base + i + n_rows] * cw_ref[i:i + 1, :]
    tail_ref[...] = x[n_rows - SUBLANES:n_rows]
    conv = conv * _sigmoid(conv)

    row = _iota2(chunk, chunk, 0)
    col = _iota2(chunk, chunk, 1)
    lower = row >= col
    tril = lower.astype(F32)

    heads = range(GDN_HEADS)
    subs = range(n_sub)
    qs, ks, vs, gcs, betas, decs = [], [], [], [], [], []
    for j in subs:
        rows = slice(j * chunk, (j + 1) * chunk)
        ab = ab_ref[rows, :]
        g_col = _dot_exact_lhs(tril, ab)
        g_row = _dot_nt_exact_rhs(ab.T, tril)
        for h in heads:
            cq = conv[rows, h * GDN_DK:(h + 1) * GDN_DK]
            ck = conv[rows, GDN_QK + h * GDN_DK:GDN_QK + (h + 1) * GDN_DK]
            vs.append(conv[rows, 2 * GDN_QK + h * GDN_DV:2 * GDN_QK + (h + 1) * GDN_DV])
            qs.append(cq * lax.rsqrt(jnp.sum(cq * cq, axis=-1, keepdims=True) + NORM_EPS) * (GDN_DK ** -0.5))
            ks.append(ck * lax.rsqrt(jnp.sum(ck * ck, axis=-1, keepdims=True) + NORM_EPS))
            gc = g_col[:, SM_A0 + h:SM_A0 + h + 1]
            gr = g_row[SM_A0 + h:SM_A0 + h + 1, :]
            gcs.append(gc)
            betas.append(ab[:, SM_B0 + h:SM_B0 + h + 1])
            decs.append(jnp.where(lower, jnp.exp(jnp.where(lower, gc - gr, 0.0)), 0.0))
    kks = [_dot3_nt(k, k) for k in ks]
    qks = [_dot3_nt(q, k) * dec for q, k, dec in zip(qs, ks, decs)]
    lows = [jnp.where(row > col, beta * kk * dec, 0.0) for beta, kk, dec in zip(betas, kks, decs)]
    tinvs = _unit_lower_inverses(lows, row, col, chunk)
    egs = [jnp.exp(gc) for gc in gcs]
    uws = [_dot3(tinv, jnp.concatenate([v * beta, k * (beta * eg)], axis=1))
           for tinv, v, k, beta, eg in zip(tinvs, vs, ks, betas, egs)]
    sts = [st_ref[h] for h in heads]
    for j in subs:
        rows = slice(j * chunk, (j + 1) * chunk)
        ids = [j * GDN_HEADS + h for h in heads]
        v_news = [uws[i][:, :GDN_DV] - _dot3(uws[i][:, GDN_DV:], st) for i, st in zip(ids, sts)]
        os_ = [_dot3(qs[i] * egs[i], st) + _dot3(qks[i], v_new) for i, st, v_new in zip(ids, sts, v_news)]
        new_sts = []
        for h, i in enumerate(ids):
            g_last = gcs[i][chunk - 1:chunk, :]
            new_sts.append(sts[h] * jnp.exp(g_last) + _dot3_tn(ks[i] * jnp.exp(g_last - gcs[i]), v_news[h]))
        sts = new_sts
        for h in heads:
            o = os_[h]
            ms = jnp.mean(o * o, axis=-1, keepdims=True)
            on = o * lax.rsqrt(ms + NORM_EPS) * nw_ref[...]
            gate = dz_ref[rows, h * GDN_DV:(h + 1) * GDN_DV]
            o_ref[rows, h * GDN_DV:(h + 1) * GDN_DV] = on * (gate * _sigmoid(gate))
    for h in heads:
        st_ref[h] = sts[h]

    @pl.when(c == nc - 1)
    def _():
        sfin_ref[0] = st_ref[...]
        tail_out_ref[0] = tail_ref[...]


def _gdn(dqkv, ab, dz, tail0, cw, s0, nw, n_seq, seq_len, chunk):
    n_sub = _chunks_per_step(seq_len, chunk)
    nc = seq_len // (chunk * n_sub)
    t = n_seq * seq_len
    blk = lambda w: pl.BlockSpec((chunk * n_sub, w), lambda b, c: (b * nc + c, 0))
    st_spec = pl.BlockSpec((1, GDN_HEADS, GDN_DK, GDN_DV), lambda b, c: (b, 0, 0, 0))
    tail_spec = pl.BlockSpec((1, SUBLANES, GDN_QKV), lambda b, c: (b, 0, 0))
    return pl.pallas_call(
        functools.partial(_gdn_kernel, chunk=chunk, n_sub=n_sub),
        out_shape=[jax.ShapeDtypeStruct((t, GDN_V), F32),
                   jax.ShapeDtypeStruct((n_seq, GDN_HEADS, GDN_DK, GDN_DV), F32),
                   jax.ShapeDtypeStruct((n_seq, SUBLANES, GDN_QKV), F32)],
        grid=(n_seq, nc),
        in_specs=[blk(GDN_QKV), blk(LANES), blk(GDN_V), tail_spec,
                  pl.BlockSpec((SUBLANES, GDN_QKV), lambda b, c: (0, 0)), st_spec,
                  pl.BlockSpec((1, GDN_DV), lambda b, c: (0, 0))],
        out_specs=[blk(GDN_V), st_spec, tail_spec],
        scratch_shapes=[pltpu.VMEM((GDN_HEADS, GDN_DK, GDN_DV), F32),
                        pltpu.VMEM((SUBLANES, GDN_QKV), F32)],
        compiler_params=_cparams(("parallel", "arbitrary"), 32),
        name="gdn_scan",
    )(dqkv, ab, dz, tail0, cw, s0, nw)


def _topk_cols(s, k, payload=None):
    n = s.shape[0]
    rid = lax.broadcasted_iota(jnp.int32, s.shape, 0).astype(F32)
    vals, idxs = [], []
    for _ in range(k):
        m = jnp.max(s, axis=0, keepdims=True)
        am = jnp.min(jnp.where(s == m, rid, float(n)), axis=0, keepdims=True)
        sel = rid == am
        vals.append(m)
        if payload is None:
            idxs.append(am)
        else:
            idxs.append(jnp.sum(jnp.where(sel, payload, 0.0), axis=0, keepdims=True))
        s = jnp.where(sel, -jnp.inf, s)
    return jnp.concatenate(vals, axis=0), jnp.concatenate(idxs, axis=0)


def _retrieve(s1, s2):
    v1, i1 = _topk_cols(s1, PEER_TOPK)
    v2, i2 = _topk_cols(s2, PEER_TOPK)
    cand_rows, cidx_rows = [], []
    for a in range(PEER_TOPK // 2):
        n_valid = PEER_TOPK // (a + 1)
        n_rows = -(-n_valid // SUBLANES) * SUBLANES
        c = v1[a:a + 1, :] + v2[0:n_rows, :]
        if n_valid < n_rows:
            c = jnp.where(lax.broadcasted_iota(jnp.int32, c.shape, 0) < n_valid, c, -jnp.inf)
        cand_rows.append(c)
        cidx_rows.append(i1[a:a + 1, :] * float(N_KEYS) + i2[0:n_rows, :])
    cand_rows.append(v1[PEER_TOPK // 2:, :] + v2[0:1, :])
    cidx_rows.append(i1[PEER_TOPK // 2:, :] * float(N_KEYS) + i2[0:1, :])
    sc, eidx = _topk_cols(jnp.concatenate(cand_rows, axis=0), PEER_TOPK,
                          payload=jnp.concatenate(cidx_rows, axis=0))
    e = jnp.exp(sc - jnp.max(sc, axis=0, keepdims=True))
    return eidx, e / jnp.sum(e, axis=0, keepdims=True)


def _post_kernel(o1_ref, o2_ref, x_ref, wo_ref, nw_ref, wq_ref, k1_ref, k2_ref,
                 x2_ref, hn_ref, j8_ref, sh_ref, gate_ref):
    mix = (_dot(o1_ref[...].astype(BF16), wo_ref[0:GLA_V, :])
           + _dot(o2_ref[...].astype(BF16), wo_ref[GLA_V:GLA_V + GDN_V, :]))
    x2 = x_ref[...] + mix
    x2_ref[...] = x2
    ms = jnp.mean(x2 * x2, axis=-1, keepdims=True)
    hn = x2 * lax.rsqrt(ms + NORM_EPS) * nw_ref[...]
    n_chunk = hn.shape[1] // LANES
    for c in range(n_chunk):
        hn_ref[pl.ds(c, hn.shape[0], stride=n_chunk), :] = hn[:, c * LANES:(c + 1) * LANES]
    qry = _dot(hn.astype(BF16), wq_ref[...])
    idx_rows, gate_rows = [], []
    for h in range(PEER_HEADS):
        q1 = qry[:, h * PEER_QDIM:h * PEER_QDIM + PEER_HALF].astype(BF16)
        q2 = qry[:, h * PEER_QDIM + PEER_HALF:(h + 1) * PEER_QDIM].astype(BF16)
        s1 = _dot_nt(k1_ref[h], q1)
        s2 = _dot_nt(k2_ref[h], q2)
        parts = [_retrieve(s1[:, c0:c0 + LANES], s2[:, c0:c0 + LANES]) for c0 in range(0, s1.shape[1], LANES)]
        idx_rows.append(jnp.concatenate([p[0] for p in parts], axis=1))
        gate_rows.append(jnp.concatenate([p[1] for p in parts], axis=1))
    e = jnp.concatenate(idx_rows, axis=0).T.astype(jnp.int32)
    j8_ref[...] = (e & (HALF_EXPERTS - 1)) * SUBLANES
    sh_ref[...] = jnp.where(e >= HALF_EXPERTS, 0.0, 16.0)
    gate_ref[...] = jnp.concatenate(gate_rows, axis=0).T


def _post(o1, o2, x, wo, nw, wq, k1, k2, tm):
    t, d = x.shape
    row = lambda w: pl.BlockSpec((tm, w), lambda i: (i, 0))
    full = lambda a: pl.BlockSpec(a.shape, lambda i: (0,) * a.ndim)
    return pl.pallas_call(
        _post_kernel,
        out_shape=[jax.ShapeDtypeStruct((t, d), F32), jax.ShapeDtypeStruct((t * (d // LANES), LANES), F32),
                   jax.ShapeDtypeStruct((t, N_SEL), jnp.int32), jax.ShapeDtypeStruct((t, N_SEL), F32),
                   jax.ShapeDtypeStruct((t, N_SEL), F32)],
        grid=(t // tm,),
        in_specs=[row(GLA_V), row(GDN_V), row(d), full(wo), full(nw), full(wq), full(k1), full(k2)],
        out_specs=[row(d), pl.BlockSpec((tm * (d // LANES), LANES), lambda i: (i, 0)),
                   row(N_SEL), row(N_SEL), row(N_SEL)],
        compiler_params=_cparams(("parallel",), 48),
        name="post_topk",
    )(o1, o2, x, wo, nw, wq, k1, k2)


_HI_MASK = -65536
_IDX_GROUP = 16


def _diag_rows(vals, eye):
    n = vals.shape[0]
    return (vals[:, None, :] * eye[None, :, :]).reshape(n * N_SEL, N_SEL)


def _add_bf16_pairs(a, b):
    return pltpu.bitcast(pltpu.bitcast(a, BF16) + pltpu.bitcast(b, BF16), jnp.int32)


def _fold_sublanes(vs, sub, add):
    m4 = sub < 4
    cur = []
    for a, b in zip(vs[0::2], vs[1::2]):
        cur.append(add(jnp.where(m4, a, b), pltpu.roll(jnp.where(m4, b, a), 4, 0)))
    for half in (2, 1):
        m = (sub % (2 * half)) < half
        nxt = []
        for a, b in zip(cur[0::2], cur[1::2]):
            ta = add(a, pltpu.roll(a, SUBLANES - half, 0))
            tb = add(b, pltpu.roll(b, half, 0))
            nxt.append(jnp.where(m, ta, tb))
        cur = nxt
    return cur[0]


def _eye():
    return (_iota2(N_SEL, N_SEL, 0) == _iota2(N_SEL, N_SEL, 1)).astype(F32)


def _peer_u_kernel(j8_ref, sh_ref, hn_ref, gate_ref, tab_ref, w_ref, *scratch, tb):
    q_refs = scratch[:_IDX_GROUP]
    idx_ref, sem = scratch[_IDX_GROUP:]
    sub = lax.broadcasted_iota(jnp.int32, (SUBLANES, LANES), 0)
    ones = jnp.ones((LANES, LANES), BF16)
    perm = (0, 4, 2, 6, 1, 5, 3, 7)
    half_of_row = _iota2(2 * N_SEL, N_SEL, 0) - 2 * _iota2(2 * N_SEL, N_SEL, 1)

    def finish(src_ref, out_tok):
        rep = _dot(pltpu.bitcast(src_ref[...], BF16), ones)
        high = jnp.where(sh_ref[pl.ds(out_tok, 1), :] == 0.0, 1, 0)
        act = jnp.sum(jnp.where(half_of_row == high, rep, 0.0), axis=0, keepdims=True)
        gelu = 0.5 * act * (1.0 + jnp.tanh(0.7978845608028654 * (act + 0.044715 * act * act * act)))
        w_ref[pl.ds(out_tok, 1), :] = gate_ref[pl.ds(out_tok, 1), :] * gelu

    n_groups = tb // _IDX_GROUP

    def idx_copy(g, buf):
        src = j8_ref.at[pl.ds(g * _IDX_GROUP, _IDX_GROUP)]
        return pltpu.make_async_copy(src, idx_ref.at[buf], sem.at[buf])

    lag = 2
    for q_ref in q_refs[len(q_refs) - lag:]:
        q_ref[...] = jnp.zeros((N_SEL, LANES), jnp.int32)
    idx_copy(0, 0).start()

    def body(i, carry):
        for buf in range(2):
            g = 2 * i + buf
            idx_copy(g, buf).wait()

            @pl.when(g + 1 < n_groups)
            def _():
                idx_copy(g + 1, 1 - buf).start()

            for s in range(_IDX_GROUP):
                t = g * _IDX_GROUP + s
                finish(q_refs[(s - lag) % _IDX_GROUP], jnp.maximum(t - lag, 0))
                xi = lax.bitcast_convert_type(hn_ref[t].astype(BF16).astype(F32), jnp.int32)
                xx = pltpu.bitcast(xi | lax.shift_right_logical(xi, 16), BF16)
                folded = []
                for grp in range(N_SEL // SUBLANES):
                    prods = []
                    for r in range(SUBLANES):
                        k = grp * SUBLANES + perm[r]
                        row0 = pl.multiple_of(idx_ref[buf, s, 0, k], SUBLANES)
                        words = tab_ref[pl.ds(row0, SUBLANES), :]
                        prods.append(pltpu.bitcast(pltpu.bitcast(words, BF16) * xx, jnp.int32))
                    folded.append(_fold_sublanes(prods, sub, _add_bf16_pairs))
                q_refs[s][...] = jnp.concatenate(folded, axis=0)
        return carry

    lax.fori_loop(0, n_groups // 2, body, 0)
    for t in range(tb - lag, tb):
        finish(q_refs[t % _IDX_GROUP], t)


def _peer_u(j8, sh, hn3, gate, tab, tb):
    t = j8.shape[0]
    vec = lambda: pl.BlockSpec((tb, N_SEL), lambda i: (i, 0))
    return pl.pallas_call(
        functools.partial(_peer_u_kernel, tb=tb),
        out_shape=jax.ShapeDtypeStruct((t, N_SEL), F32),
        grid=(t // tb,),
        in_specs=[pl.BlockSpec((tb, 1, N_SEL), lambda i: (i, 0, 0)), vec(),
                  pl.BlockSpec((tb, SUBLANES, LANES), lambda i: (i, 0, 0)), vec(),
                  pl.BlockSpec(memory_space=pltpu.VMEM)],
        out_specs=vec(),
        scratch_shapes=[pltpu.VMEM((N_SEL, LANES), jnp.int32) for _ in range(_IDX_GROUP)]
        + [pltpu.SMEM((2, _IDX_GROUP, 1, N_SEL), jnp.int32), pltpu.SemaphoreType.DMA((2,))],
        compiler_params=_cparams(("arbitrary",), 56),
        name="peer_u",
    )(j8.reshape(t, 1, N_SEL), sh, hn3, gate, tab)


def _peer_v_kernel(j8_ref, sh_ref, wgt_ref, x2_ref, nw_ref, tab_ref, y_ref, wp_ref, acc_ref, idx_ref, sem,
                   *, tb, n_acc):
    eye = _eye()
    blk = (_iota2(2 * N_SEL, 2 * LANES, 0) // N_SEL == _iota2(2 * N_SEL, 2 * LANES, 1) // LANES).astype(BF16)

    def pre(c, carry):
        tok0 = pl.multiple_of(c * SUBLANES, SUBLANES)
        toks = pl.ds(tok0, SUBLANES)
        w = wgt_ref[toks, :]
        low = sh_ref[toks, :] > 0.0
        diag = jnp.concatenate([_diag_rows(jnp.where(low, w, 0.0), eye).astype(BF16),
                                _diag_rows(jnp.where(low, 0.0, w), eye).astype(BF16)], axis=1)
        rep = lax.bitcast_convert_type(_dot(diag, blk), jnp.int32)
        dst = pl.ds(pl.multiple_of(tok0 * N_SEL, SUBLANES * N_SEL), SUBLANES * N_SEL)
        wp_ref[dst, :] = rep[:, LANES:] | lax.shift_right_logical(rep[:, :LANES], 16)
        return carry

    n_groups = tb // _IDX_GROUP

    def idx_copy(g, buf):
        src = j8_ref.at[pl.ds(g * _IDX_GROUP, _IDX_GROUP)]
        return pltpu.make_async_copy(src, idx_ref.at[buf], sem.at[buf])

    idx_copy(0, 0).start()
    lax.fori_loop(0, tb // SUBLANES, pre, 0)
    group4 = 4

    def group_pair(i, carry):
        for buf in range(2):
            g = 2 * i + buf
            idx_copy(g, buf).wait()

            @pl.when(g + 1 < n_groups)
            def _():
                idx_copy(g + 1, 1 - buf).start()

            for s in range(_IDX_GROUP):
                t = g * _IDX_GROUP + s
                base = t * N_SEL
                accs = [jnp.zeros((SUBLANES, LANES), F32) for _ in range(n_acc)]
                for k0 in range(0, N_SEL, group4):
                    part = None
                    for k in range(k0, k0 + group4):
                        row0 = pl.multiple_of(idx_ref[buf, s, 0, k], SUBLANES)
                        words = pltpu.bitcast(tab_ref[pl.ds(row0, SUBLANES), :], BF16)
                        wpair = jnp.broadcast_to(wp_ref[pl.ds(base + k, 1), :], (SUBLANES, LANES))
                        prod = words * pltpu.bitcast(wpair, BF16)
                        part = prod if part is None else part + prod
                    bits = pltpu.bitcast(part, jnp.int32)
                    a = (k0 // group4) % (n_acc // 2)
                    accs[2 * a] = accs[2 * a] + lax.bitcast_convert_type(lax.shift_left(bits, 16), F32)
                    accs[2 * a + 1] = accs[2 * a + 1] + lax.bitcast_convert_type(bits & _HI_MASK, F32)
                tot = accs[0]
                for a in accs[1:]:
                    tot = tot + a
                acc_ref[pl.ds(pl.multiple_of(t * SUBLANES, SUBLANES), SUBLANES), :] = tot
        return carry

    lax.fori_loop(0, n_groups // 2, group_pair, 0)
    d = x2_ref.shape[1]
    x3 = [x2_ref[:, c * LANES:(c + 1) * LANES] + acc_ref[pl.ds(c, tb, stride=SUBLANES), :]
          for c in range(SUBLANES)]
    ss = x3[0] * x3[0]
    for xc in x3[1:]:
        ss = ss + xc * xc
    scale = lax.rsqrt(jnp.sum(ss, axis=-1, keepdims=True) * (1.0 / d) + NORM_EPS)
    for c, xc in enumerate(x3):
        y_ref[:, c * LANES:(c + 1) * LANES] = xc * scale * nw_ref[:, c * LANES:(c + 1) * LANES]


def _peer_v(j8, sh, wgt, x2, nw, tab, tb):
    t, d = x2.shape
    vec = lambda: pl.BlockSpec((tb, N_SEL), lambda i: (i, 0))
    return pl.pallas_call(
        functools.partial(_peer_v_kernel, tb=tb, n_acc=4),
        out_shape=jax.ShapeDtypeStruct((t, d), F32),
        grid=(t // tb,),
        in_specs=[pl.BlockSpec((tb, 1, N_SEL), lambda i: (i, 0, 0)), vec(), vec(),
                  pl.BlockSpec((tb, d), lambda i: (i, 0)),
                  pl.BlockSpec((1, d), lambda i: (0, 0)),
                  pl.BlockSpec(memory_space=pltpu.VMEM)],
        out_specs=pl.BlockSpec((tb, d), lambda i: (i, 0)),
        scratch_shapes=[pltpu.VMEM((tb * N_SEL, LANES), jnp.int32), pltpu.VMEM((tb * SUBLANES, LANES), F32),
                        pltpu.SMEM((2, _IDX_GROUP, 1, N_SEL), jnp.int32),
                        pltpu.SemaphoreType.DMA((2,))],
        compiler_params=_cparams(("arbitrary",), 56),
        name="peer_v",
    )(j8.reshape(t, 1, N_SEL), sh, wgt, x2, nw, tab)


_PACK_ROWS = 256


def _pack_kernel(lo_ref, hi_ref, o_ref):
    lo = lax.bitcast_convert_type(lo_ref[...].astype(BF16).astype(F32), jnp.int32)
    hi = lax.bitcast_convert_type(hi_ref[...].astype(BF16).astype(F32), jnp.int32)
    packed = hi | lax.shift_right_logical(lo, 16)
    rows, d = packed.shape
    for s in range(d // LANES):
        o_ref[pl.ds(s, rows, stride=d // LANES), :] = packed[:, s * LANES:(s + 1) * LANES]


def _pack_table(tab):
    n, d = tab.shape
    sub = d // LANES
    steps = n // 2 // _PACK_ROWS
    return pl.pallas_call(
        _pack_kernel,
        out_shape=jax.ShapeDtypeStruct((n // 2 * sub, LANES), jnp.int32),
        grid=(steps,),
        in_specs=[pl.BlockSpec((_PACK_ROWS, d), lambda i: (i, 0)),
                  pl.BlockSpec((_PACK_ROWS, d), lambda i: (i + steps, 0))],
        out_specs=pl.BlockSpec((_PACK_ROWS * sub, LANES), lambda i: (i, 0)),
        compiler_params=_cparams(("arbitrary",), 32),
        name="pack_table",
    )(tab, tab)


def _prep_weights(norm_mix_w, w_in, gla_w_gk2, gla_b_gk, gla_norm_w, gdn_conv_w, gdn_a_log,
                  gdn_dt_bias, gdn_norm_w, w_out, norm_ffn_w, peer_wq, peer_k1, peer_k2, peer_u, peer_v):
    d = w_in.shape[0]
    o_glr = 2 * GLA_QK + GLA_V
    o_gg = o_glr + GLA_LR
    o_qkv = o_gg + GLA_V
    o_da = o_qkv + GDN_QKV
    o_dz = o_da + 2 * GDN_HEADS
    small = jnp.concatenate([w_in[:, o_glr:o_gg], w_in[:, o_da:o_dz],
                             jnp.zeros((d, LANES - GLA_LR - 2 * GDN_HEADS), w_in.dtype)], axis=1)
    wcat = jnp.concatenate([w_in[:, :o_glr], w_in[:, o_gg:o_qkv], w_in[:, o_qkv:o_da],
                            w_in[:, o_dz:], small], axis=1).astype(BF16)
    w2p = jnp.zeros((LANES, GLA_QK), F32).at[:GLA_LR].set(gla_w_gk2).astype(BF16)
    avec = jnp.zeros((1, LANES), F32).at[0, SM_A0:SM_B0].set(-jnp.exp(gdn_a_log))
    dtb = jnp.zeros((1, LANES), F32).at[0, SM_A0:SM_B0].set(gdn_dt_bias)
    cw = jnp.zeros((SUBLANES, GDN_QKV), F32).at[:CONV_W].set(gdn_conv_w)
    return dict(
        nmix=norm_mix_w.reshape(1, d), wcat=wcat, w2p=w2p, b2=gla_b_gk.reshape(1, GLA_QK), avec=avec, dtb=dtb,
        gla_nw=gla_norm_w.reshape(1, GLA_DV), cw=cw, gdn_nw=gdn_norm_w.reshape(1, GDN_DV),
        wo=w_out.astype(BF16), nffn=norm_ffn_w.reshape(1, d), wq=peer_wq.astype(BF16),
        k1=peer_k1.astype(BF16), k2=peer_k2.astype(BF16), tab_u=_pack_table(peer_u), tab_v=_pack_table(peer_v))


def _layer(x, s_gla, s_gdn, conv_buf, w, nfw, chunk, tm, tb):
    n_seq, seq_len, d = x.shape
    t = n_seq * seq_len
    xf = x.reshape(t, d)
    gq, gk, gv, gg, dqkv, dz, la, sm = _inproj(xf, w["nmix"], w["wcat"], w["w2p"], w["b2"], w["avec"],
                                               w["dtb"], tm)
    o1, gla_t = _gla(gq, gk, gv, la, gg, jnp.swapaxes(s_gla, -1, -2), w["gla_nw"], n_seq, seq_len, chunk)
    tail0 = jnp.pad(conv_buf, ((0, 0), (SUBLANES - (CONV_W - 1), 0), (0, 0)))
    o2, gdn_new, tail = _gdn(dqkv, sm, dz, tail0, w["cw"], s_gdn, w["gdn_nw"], n_seq, seq_len, chunk)
    x2, hn, j8, sh, gate = _post(o1, o2, xf, w["wo"], w["nffn"], w["wq"], w["k1"], w["k2"], tm)
    wgt = _peer_u(j8, sh, hn.reshape(t, SUBLANES, LANES), gate, w["tab_u"], tb)
    y = _peer_v(j8, sh, wgt, x2, nfw, w["tab_v"], tb)
    return (y.reshape(n_seq, seq_len, d), jnp.swapaxes(gla_t, -1, -2), gdn_new,
            tail[:, SUBLANES - (CONV_W - 1):, :])


def kernel(x_prompt, x_sample, state_gla, state_gdn, state_gdn_conv, norm_mix_w, w_in, gla_w_gk2, gla_b_gk,
           gla_norm_w, gdn_conv_w, gdn_a_log, gdn_dt_bias, gdn_norm_w, w_out, norm_ffn_w, peer_wq, peer_k1,
           peer_k2, peer_u, peer_v, norm_final_w):
    depth = w_in.shape[0]
    assert depth == 1, "the final norm is fused into the last layer's PEER pass"
    n_p, l_p, d = x_prompt.shape
    n_s, l_s, _ = x_sample.shape
    nfw = norm_final_w.reshape(1, d)
    w = _prep_weights(norm_mix_w[0], w_in[0], gla_w_gk2[0], gla_b_gk[0], gla_norm_w[0], gdn_conv_w[0],
                      gdn_a_log[0], gdn_dt_bias[0], gdn_norm_w[0], w_out[0], norm_ffn_w[0], peer_wq[0],
                      peer_k1[0], peer_k2[0], peer_u[0], peer_v[0])
    z_gla = jnp.zeros((n_p, GLA_HEADS, GLA_DK, GLA_DV), F32)
    z_gdn = jnp.zeros((n_p, GDN_HEADS, GDN_DK, GDN_DV), F32)
    z_conv = jnp.zeros((n_p, CONV_W - 1, GDN_QKV), F32)
    chunk_p = min(CHUNK, l_p)
    chunk_s = min(CHUNK, l_s)
    tm_p = min(256, n_p * l_p)
    tm_s = min(256, n_s * l_s)
    yp, gla_p, gdn_p, conv_p = _layer(x_prompt, z_gla, z_gdn, z_conv, w, nfw, chunk_p, tm_p, min(128, tm_p))
    ys, gla_s, gdn_s, conv_s = _layer(x_sample, state_gla[0], state_gdn[0], state_gdn_conv[0], w, nfw,
                                      chunk_s, tm_s, min(128, tm_s))
    return (yp, ys, gla_p[None], gdn_p[None], conv_p[None], gla_s[None], gdn_s[None], conv_s[None])
```

```python
import functools

import jax
import jax.numpy as jnp
from jax import lax
from jax.experimental import pallas as pl
from jax.experimental.pallas import tpu as pltpu

F32 = jnp.float32
BF16 = jnp.bfloat16

NORM_EPS = 1e-6
CHUNK = 64
GLA_HEADS, GLA_DK, GLA_DV, GLA_LR, GLA_GATE_TAU = 4, 64, 128, 16, 16.0
GDN_HEADS, GDN_DK, GDN_DV, CONV_W = 4, 128, 128, 4
GLA_QK, GLA_V = GLA_HEADS * GLA_DK, GLA_HEADS * GLA_DV
GDN_QK, GDN_V = GDN_HEADS * GDN_DK, GDN_HEADS * GDN_DV
GDN_QKV = 2 * GDN_QK + GDN_V
PEER_HEADS, PEER_QDIM, N_KEYS, PEER_TOPK = 8, 256, 128, 16
PEER_HALF = PEER_QDIM // 2
N_SEL = PEER_HEADS * PEER_TOPK
HALF_EXPERTS = N_KEYS * N_KEYS // 2

LANES = 128
SUBLANES = 8
MIB = 1024 * 1024

SM_A0 = GLA_LR
SM_B0 = GLA_LR + GDN_HEADS


def _cparams(sem, vmem_mib):
    return pltpu.CompilerParams(dimension_semantics=sem, vmem_limit_bytes=vmem_mib * MIB)


def _softplus(x):
    return jnp.maximum(x, 0.0) + jnp.log(1.0 + jnp.exp(-jnp.abs(x)))


def _sigmoid(x):
    return 1.0 / (1.0 + jnp.exp(-x))


def _dot(a, b, prec=None):
    return jnp.dot(a, b, preferred_element_type=F32, precision=prec)


def _dot_nt(a, b, prec=None):
    return lax.dot_general(a, b, (((1,), (1,)), ((), ())), preferred_element_type=F32, precision=prec)


def _dot_tn(a, b, prec=None):
    return lax.dot_general(a, b, (((0,), (0,)), ((), ())), preferred_element_type=F32, precision=prec)


def _split_bf16(x):
    hi = x.astype(BF16)
    return hi, (x - hi.astype(F32)).astype(BF16)


def _split3_bf16(x):
    hi = x.astype(BF16)
    r = x - hi.astype(F32)
    mid = r.astype(BF16)
    return hi, mid, (r - mid.astype(F32)).astype(BF16)


def _dot3(a, b):
    ah, al = _split_bf16(a)
    bh, bl = _split_bf16(b)
    return _dot(jnp.concatenate([ah, ah, al], axis=1), jnp.concatenate([bh, bl, bh], axis=0))


def _dot3_nt(a, b):
    ah, al = _split_bf16(a)
    bh, bl = _split_bf16(b)
    return _dot_nt(jnp.concatenate([ah, ah, al], axis=1), jnp.concatenate([bh, bl, bh], axis=1))


def _dot3_tn(a, b):
    ah, al = _split_bf16(a)
    bh, bl = _split_bf16(b)
    return _dot_tn(jnp.concatenate([ah, ah, al], axis=0), jnp.concatenate([bh, bl, bh], axis=0))


def _dot_exact_lhs(a01, b):
    a16 = a01.astype(BF16)
    return _dot(jnp.concatenate([a16, a16, a16], axis=1), jnp.concatenate(_split3_bf16(b), axis=0))


def _dot_nt_exact_rhs(a, b01):
    b16 = b01.astype(BF16)
    return _dot_nt(jnp.concatenate(_split3_bf16(a), axis=1), jnp.concatenate([b16, b16, b16], axis=1))


_W_COLS = (("gq", GLA_QK), ("gk", GLA_QK), ("gv", GLA_V), ("gg", GLA_V), ("dqkv", GDN_QKV),
           ("dz", GDN_V), ("sm", LANES))


def _inproj_kernel(x_ref, nw_ref, w_ref, w2_ref, b2_ref, avec_ref, dtb_ref,
                   gq_ref, gk_ref, gv_ref, gg_ref, dqkv_ref, dz_ref, la_ref, sm_ref):
    x = x_ref[...]
    ms = jnp.mean(x * x, axis=-1, keepdims=True)
    h = (x * lax.rsqrt(ms + NORM_EPS) * nw_ref[...]).astype(BF16)
    outs = dict(gq=gq_ref, gk=gk_ref, gv=gv_ref, gg=gg_ref, dqkv=dqkv_ref, dz=dz_ref)
    off = 0
    ps = None
    for name, width in _W_COLS:
        p = _dot(h, w_ref[:, off:off + width])
        off += width
        if name == "sm":
            ps = p
        elif name == "gq":
            gq_ref[...] = p * (GLA_DK ** -0.5)
        else:
            outs[name][...] = p
    z = _dot(ps.astype(BF16), w2_ref[...]) + b2_ref[...]
    la_ref[...] = -_softplus(-z) * (1.0 / GLA_GATE_TAU)
    lane = lax.broadcasted_iota(jnp.int32, ps.shape, 1)
    log_a = avec_ref[...] * _softplus(ps + dtb_ref[...])
    beta = _sigmoid(ps)
    is_a = (lane >= SM_A0) & (lane < SM_B0)
    is_b = (lane >= SM_B0) & (lane < SM_B0 + GDN_HEADS)
    sm_ref[...] = jnp.where(is_a, log_a, jnp.where(is_b, beta, ps))


def _inproj(x, nw, wcat, w2p, b2, avec, dtb, tm):
    t, d = x.shape
    n_w = wcat.shape[1]
    widths = dict(_W_COLS)
    names = ("gq", "gk", "gv", "gg", "dqkv", "dz")
    out_shape = [jax.ShapeDtypeStruct((t, widths[n]), F32) for n in names]
    out_shape += [jax.ShapeDtypeStruct((t, GLA_QK), F32), jax.ShapeDtypeStruct((t, LANES), F32)]
    row = lambda w: pl.BlockSpec((tm, w), lambda i: (i, 0))
    full = lambda a: pl.BlockSpec(a.shape, lambda i: (0,) * a.ndim)
    return pl.pallas_call(
        _inproj_kernel,
        out_shape=out_shape,
        grid=(t // tm,),
        in_specs=[row(d), full(nw), full(wcat), full(w2p), full(b2), full(avec), full(dtb)],
        out_specs=[row(widths[n]) for n in names] + [row(GLA_QK), row(LANES)],
        compiler_params=_cparams(("parallel",), 48),
        name="inproj",
    )(x, nw, wcat, w2p, b2, avec, dtb)


def _iota2(n, m, axis):
    return lax.broadcasted_iota(jnp.int32, (n, m), axis)


def _gla_kernel(q_ref, k_ref, v_ref, la_ref, gg_ref, s0_ref, nw_ref, o_ref, sfin_ref, st_ref, *, chunk, n_sub):
    c = pl.program_id(1)
    nc = pl.num_programs(1)

    @pl.when(c == 0)
    def _():
        st_ref[...] = s0_ref[0]

    row = _iota2(chunk, chunk, 0)
    col = _iota2(chunk, chunk, 1)
    subs = range(n_sub)
    rows = [slice(j * chunk, (j + 1) * chunk) for j in subs]

    blocks = []
    b = chunk // 2
    while b >= 1:
        blocks.append(b)
        b //= 2
    sel = [row >= col]
    for b in blocks:
        sel.append(col <= (row // b) * b)
        sel.append(col <= jnp.minimum((row // b + 1) * b, chunk - 1))
    la_wide = jnp.concatenate([la_ref[r, :] for r in rows], axis=1)
    g_wide = _dot_exact_lhs(jnp.concatenate([m.astype(F32) for m in sel], axis=0), la_wide)

    heads = range(GLA_HEADS)
    dks = [slice(h * GLA_DK, (h + 1) * GLA_DK) for h in heads]
    dvs = [slice(h * GLA_DV, (h + 1) * GLA_DV) for h in heads]
    prep = []
    for j in subs:
        g_all = g_wide[:, j * GLA_QK:(j + 1) * GLA_QK]
        q = q_ref[rows[j], :]
        k = k_ref[rows[j], :]
        g = g_all[0:chunk]
        levels = []
        for li, b in enumerate(blocks):
            ref_r = g_all[(1 + 2 * li) * chunk:(2 + 2 * li) * chunk]
            ref_c = g_all[(2 + 2 * li) * chunk:(3 + 2 * li) * chunk]
            mask = (row // (2 * b) == col // (2 * b)) & (row // b == col // b + 1)
            levels.append((q * jnp.exp(g - ref_r), k * jnp.exp(ref_c - g), mask))
        g_last = g[chunk - 1:chunk, :]
        prep.append(dict(levels=levels, qk_diag=q * k, q_in=q * jnp.exp(g), k_out=k * jnp.exp(g_last - g),
                         decay_out=jnp.exp(g_last)))
    atts = [[jnp.zeros((chunk, chunk), F32) for _ in heads] for _ in subs]
    for li in range(len(blocks)):
        for j in subs:
            ql, kl, mask = prep[j]["levels"][li]
            atts[j] = [att + jnp.where(mask, _dot3_nt(ql[:, dk], kl[:, dk]), 0.0)
                       for att, dk in zip(atts[j], dks)]
    vhs = [[v_ref[rows[j], dv] for dv in dvs] for j in subs]
    intra = [[_dot3(atts[j][h], vhs[j][h])
              + jnp.sum(prep[j]["qk_diag"][:, dks[h]], axis=-1, keepdims=True) * vhs[j][h]
              for h in heads] for j in subs]
    kvs = [[_dot3_tn(vhs[j][h], prep[j]["k_out"][:, dks[h]]) for h in heads] for j in subs]
    sts = [st_ref[h] for h in heads]
    for j in subs:
        os_ = [_dot3_nt(prep[j]["q_in"][:, dks[h]], sts[h]) + intra[j][h] for h in heads]
        sts = [sts[h] * prep[j]["decay_out"][:, dks[h]] + kvs[j][h] for h in heads]
        for h in heads:
            o = os_[h]
            ms = jnp.mean(o * o, axis=-1, keepdims=True)
            on = o * lax.rsqrt(ms + NORM_EPS) * nw_ref[...]
            gate = gg_ref[rows[j], dvs[h]]
            o_ref[rows[j], dvs[h]] = on * (gate * _sigmoid(gate))
    for h in heads:
        st_ref[h] = sts[h]

    @pl.when(c == nc - 1)
    def _():
        sfin_ref[0] = st_ref[...]


def _chunks_per_step(seq_len, chunk):
    for n_sub in (4, 2):
        if seq_len % (n_sub * chunk) == 0:
            return n_sub
    return 1


def _gla(gq, gk, gv, la, gg, s0t, nw, n_seq, seq_len, chunk):
    n_sub = _chunks_per_step(seq_len, chunk)
    nc = seq_len // (chunk * n_sub)
    t = n_seq * seq_len
    blk = lambda w: pl.BlockSpec((chunk * n_sub, w), lambda b, c: (b * nc + c, 0))
    st_spec = pl.BlockSpec((1, GLA_HEADS, GLA_DV, GLA_DK), lambda b, c: (b, 0, 0, 0))
    return pl.pallas_call(
        functools.partial(_gla_kernel, chunk=chunk, n_sub=n_sub),
        out_shape=[jax.ShapeDtypeStruct((t, GLA_V), F32),
                   jax.ShapeDtypeStruct((n_seq, GLA_HEADS, GLA_DV, GLA_DK), F32)],
        grid=(n_seq, nc),
        in_specs=[blk(GLA_QK), blk(GLA_QK), blk(GLA_V), blk(GLA_QK), blk(GLA_V), st_spec,
                  pl.BlockSpec((1, GLA_DV), lambda b, c: (0, 0))],
        out_specs=[blk(GLA_V), st_spec],
        scratch_shapes=[pltpu.VMEM((GLA_HEADS, GLA_DV, GLA_DK), F32)],
        compiler_params=_cparams(("parallel", "arbitrary"), 32),
        name="gla_scan",
    )(gq, gk, gv, la, gg, s0t, nw)


_INV_BASE = 16


def _unit_lower_inverses(lows, row, col, n):
    eye = (row == col).astype(F32)
    base = min(_INV_BASE, n)
    diag_blk = row // base == col // base
    pws = [jnp.where(diag_blk, low, 0.0) for low in lows]
    invs = [eye - ld for ld in pws]
    p = 2
    while p < base:
        pws = [_dot3(pw, pw) for pw in pws]
        invs = [inv + _dot3(inv, pw) for inv, pw in zip(invs, pws)]
        p *= 2
    b = base
    while b < n:
        sib = (row // (2 * b) == col // (2 * b)) & (row // b != col // b)
        tmps = [_dot3(jnp.where(sib, low, 0.0), inv) for low, inv in zip(lows, invs)]
        invs = [inv - _dot3(inv, t) for inv, t in zip(invs, tmps)]
        b *= 2
    return invs


def _gdn_kernel(x_ref, ab_ref, dz_ref, tail0_ref, cw_ref, s0_ref, nw_ref,
                o_ref, sfin_ref, tail_out_ref, st_ref, tail_ref, *, chunk, n_sub):
    c = pl.program_id(1)
    nc = pl.num_programs(1)

    @pl.when(c == 0)
    def _():
        st_ref[...] = s0_ref[0]
        tail_ref[...] = tail0_ref[0]

    n_rows = chunk * n_sub
    x = x_ref[...]
    ext = jnp.concatenate([tail_ref[...], x], axis=0)
    base = SUBLANES - (CONV_W - 1)
    conv = ext[base:base + n_rows] * cw_ref[0:1, :]
    for i in range(1, CONV_W):
        conv = conv + ext[base + i:base + i + n_rows] * cw_ref[i:i + 1, :]
    tail_ref[...] = x[n_rows - SUBLANES:n_rows]
    conv = conv * _sigmoid(conv)

    row = _iota2(chunk, chunk, 0)
    col = _iota2(chunk, chunk, 1)
    lower = row >= col
    tril = lower.astype(F32)

    heads = range(GDN_HEADS)
    subs = range(n_sub)
    qs, ks, vs, gcs, betas, decs = [], [], [], [], [], []
    for j in subs:
        rows = slice(j * chunk, (j + 1) * chunk)
        ab = ab_ref[rows, :]
        g_col = _dot_exact_lhs(tril, ab)
        g_row = _dot_nt_exact_rhs(ab.T, tril)
        for h in heads:
            cq = conv[rows, h * GDN_DK:(h + 1) * GDN_DK]
            ck = conv[rows, GDN_QK + h * GDN_DK:GDN_QK + (h + 1) * GDN_DK]
            vs.append(conv[rows, 2 * GDN_QK + h * GDN_DV:2 * GDN_QK + (h + 1) * GDN_DV])
            qs.append(cq * lax.rsqrt(jnp.sum(cq * cq, axis=-1, keepdims=True) + NORM_EPS) * (GDN_DK ** -0.5))
            ks.append(ck * lax.rsqrt(jnp.sum(ck * ck, axis=-1, keepdims=True) + NORM_EPS))
            gc = g_col[:, SM_A0 + h:SM_A0 + h + 1]
            gr = g_row[SM_A0 + h:SM_A0 + h + 1, :]
            gcs.append(gc)
            betas.append(ab[:, SM_B0 + h:SM_B0 + h + 1])
            decs.append(jnp.where(lower, jnp.exp(jnp.where(lower, gc - gr, 0.0)), 0.0))
    kks = [_dot3_nt(k, k) for k in ks]
    qks = [_dot3_nt(q, k) * dec for q, k, dec in zip(qs, ks, decs)]
    lows = [jnp.where(row > col, beta * kk * dec, 0.0) for beta, kk, dec in zip(betas, kks, decs)]
    tinvs = _unit_lower_inverses(lows, row, col, chunk)
    egs = [jnp.exp(gc) for gc in gcs]
    uws = [_dot3(tinv, jnp.concatenate([v * beta, k * (beta * eg)], axis=1))
           for tinv, v, k, beta, eg in zip(tinvs, vs, ks, betas, egs)]
    sts = [st_ref[h] for h in heads]
    for j in subs:
        rows = slice(j * chunk, (j + 1) * chunk)
        ids = [j * GDN_HEADS + h for h in heads]
        v_news = [uws[i][:, :GDN_DV] - _dot3(uws[i][:, GDN_DV:], st) for i, st in zip(ids, sts)]
        os_ = [_dot3(qs[i] * egs[i], st) + _dot3(qks[i], v_new) for i, st, v_new in zip(ids, sts, v_news)]
        new_sts = []
        for h, i in enumerate(ids):
            g_last = gcs[i][chunk - 1:chunk, :]
            new_sts.append(sts[h] * jnp.exp(g_last) + _dot3_tn(ks[i] * jnp.exp(g_last - gcs[i]), v_news[h]))
        sts = new_sts
        for h in heads:
            o = os_[h]
            ms = jnp.mean(o * o, axis=-1, keepdims=True)
            on = o * lax.rsqrt(ms + NORM_EPS) * nw_ref[...]
            gate = dz_ref[rows, h * GDN_DV:(h + 1) * GDN_DV]
            o_ref[rows, h * GDN_DV:(h + 1) * GDN_DV] = on * (gate * _sigmoid(gate))
    for h in heads:
        st_ref[h] = sts[h]

    @pl.when(c == nc - 1)
    def _():
        sfin_ref[0] = st_ref[...]
        tail_out_ref[0] = tail_ref[...]


def _gdn(dqkv, ab, dz, tail0, cw, s0, nw, n_seq, seq_len, chunk):
    n_sub = _chunks_per_step(seq_len, chunk)
    nc = seq_len // (chunk * n_sub)
    t = n_seq * seq_len
    blk = lambda w: pl.BlockSpec((chunk * n_sub, w), lambda b, c: (b * nc + c, 0))
    st_spec = pl.BlockSpec((1, GDN_HEADS, GDN_DK, GDN_DV), lambda b, c: (b, 0, 0, 0))
    tail_spec = pl.BlockSpec((1, SUBLANES, GDN_QKV), lambda b, c: (b, 0, 0))
    return pl.pallas_call(
        functools.partial(_gdn_kernel, chunk=chunk, n_sub=n_sub),
        out_shape=[jax.ShapeDtypeStruct((t, GDN_V), F32),
                   jax.ShapeDtypeStruct((n_seq, GDN_HEADS, GDN_DK, GDN_DV), F32),
                   jax.ShapeDtypeStruct((n_seq, SUBLANES, GDN_QKV), F32)],
        grid=(n_seq, nc),
        in_specs=[blk(GDN_QKV), blk(LANES), blk(GDN_V), tail_spec,
                  pl.BlockSpec((SUBLANES, GDN_QKV), lambda b, c: (0, 0)), st_spec,
                  pl.BlockSpec((1, GDN_DV), lambda b, c: (0, 0))],
        out_specs=[blk(GDN_V), st_spec, tail_spec],
        scratch_shapes=[pltpu.VMEM((GDN_HEADS, GDN_DK, GDN_DV), F32),
                        pltpu.VMEM((SUBLANES, GDN_QKV), F32)],
        compiler_params=_cparams(("parallel", "arbitrary"), 32),
        name="gdn_scan",
    )(dqkv, ab, dz, tail0, cw, s0, nw)


def _topk_cols(s, k, payload=None):
    n = s.shape[0]
    rid = lax.broadcasted_iota(jnp.int32, s.shape, 0).astype(F32)
    vals, idxs = [], []
    for _ in range(k):
        m = jnp.max(s, axis=0, keepdims=True)
        am = jnp.min(jnp.where(s == m, rid, float(n)), axis=0, keepdims=True)
        sel = rid == am
        vals.append(m)
        if payload is None:
            idxs.append(am)
        else:
            idxs.append(jnp.sum(jnp.where(sel, payload, 0.0), axis=0, keepdims=True))
        s = jnp.where(sel, -jnp.inf, s)
    return jnp.concatenate(vals, axis=0), jnp.concatenate(idxs, axis=0)


def _retrieve(s1, s2):
    v1, i1 = _topk_cols(s1, PEER_TOPK)
    v2, i2 = _topk_cols(s2, PEER_TOPK)
    cand_rows, cidx_rows = [], []
    for a in range(PEER_TOPK // 2):
        n_valid = PEER_TOPK // (a + 1)
        n_rows = -(-n_valid // SUBLANES) * SUBLANES
        c = v1[a:a + 1, :] + v2[0:n_rows, :]
        if n_valid < n_rows:
            c = jnp.where(lax.broadcasted_iota(jnp.int32, c.shape, 0) < n_valid, c, -jnp.inf)
        cand_rows.append(c)
        cidx_rows.append(i1[a:a + 1, :] * float(N_KEYS) + i2[0:n_rows, :])
    cand_rows.append(v1[PEER_TOPK // 2:, :] + v2[0:1, :])
    cidx_rows.append(i1[PEER_TOPK // 2:, :] * float(N_KEYS) + i2[0:1, :])
    sc, eidx = _topk_cols(jnp.concatenate(cand_rows, axis=0), PEER_TOPK,
                          payload=jnp.concatenate(cidx_rows, axis=0))
    e = jnp.exp(sc - jnp.max(sc, axis=0, keepdims=True))
    return eidx, e / jnp.sum(e, axis=0, keepdims=True)


def _post_kernel(o1_ref, o2_ref, x_ref, wo_ref, nw_ref, wq_ref, k1_ref, k2_ref,
                 x2_ref, hn_ref, j8_ref, sh_ref, gate_ref):
    mix = (_dot(o1_ref[...].astype(BF16), wo_ref[0:GLA_V, :])
           + _dot(o2_ref[...].astype(BF16), wo_ref[GLA_V:GLA_V + GDN_V, :]))
    x2 = x_ref[...] + mix
    x2_ref[...] = x2
    ms = jnp.mean(x2 * x2, axis=-1, keepdims=True)
    hn = x2 * lax.rsqrt(ms + NORM_EPS) * nw_ref[...]
    n_chunk = hn.shape[1] // LANES
    for c in range(n_chunk):
        hn_ref[pl.ds(c, hn.shape[0], stride=n_chunk), :] = hn[:, c * LANES:(c + 1) * LANES]
    qry = _dot(hn.astype(BF16), wq_ref[...])
    idx_rows, gate_rows = [], []
    for h in range(PEER_HEADS):
        q1 = qry[:, h * PEER_QDIM:h * PEER_QDIM + PEER_HALF].astype(BF16)
        q2 = qry[:, h * PEER_QDIM + PEER_HALF:(h + 1) * PEER_QDIM].astype(BF16)
        s1 = _dot_nt(k1_ref[h], q1)
        s2 = _dot_nt(k2_ref[h], q2)
        parts = [_retrieve(s1[:, c0:c0 + LANES], s2[:, c0:c0 + LANES]) for c0 in range(0, s1.shape[1], LANES)]
        idx_rows.append(jnp.concatenate([p[0] for p in parts], axis=1))
        gate_rows.append(jnp.concatenate([p[1] for p in parts], axis=1))
    e = jnp.concatenate(idx_rows, axis=0).T.astype(jnp.int32)
    j8_ref[...] = (e & (HALF_EXPERTS - 1)) * SUBLANES
    sh_ref[...] = jnp.where(e >= HALF_EXPERTS, 0.0, 16.0)
    gate_ref[...] = jnp.concatenate(gate_rows, axis=0).T


def _post(o1, o2, x, wo, nw, wq, k1, k2, tm):
    t, d = x.shape
    row = lambda w: pl.BlockSpec((tm, w), lambda i: (i, 0))
    full = lambda a: pl.BlockSpec(a.shape, lambda i: (0,) * a.ndim)
    return pl.pallas_call(
        _post_kernel,
        out_shape=[jax.ShapeDtypeStruct((t, d), F32), jax.ShapeDtypeStruct((t * (d // LANES), LANES), F32),
                   jax.ShapeDtypeStruct((t, N_SEL), jnp.int32), jax.ShapeDtypeStruct((t, N_SEL), F32),
                   jax.ShapeDtypeStruct((t, N_SEL), F32)],
        grid=(t // tm,),
        in_specs=[row(GLA_V), row(GDN_V), row(d), full(wo), full(nw), full(wq), full(k1), full(k2)],
        out_specs=[row(d), pl.BlockSpec((tm * (d // LANES), LANES), lambda i: (i, 0)),
                   row(N_SEL), row(N_SEL), row(N_SEL)],
        compiler_params=_cparams(("parallel",), 48),
        name="post_topk",
    )(o1, o2, x, wo, nw, wq, k1, k2)


_HI_MASK = -65536
_IDX_GROUP = 32


def _diag_rows(vals, eye):
    n = vals.shape[0]
    return (vals[:, None, :] * eye[None, :, :]).reshape(n * N_SEL, N_SEL)


def _add_bf16_pairs(a, b):
    return pltpu.bitcast(pltpu.bitcast(a, BF16) + pltpu.bitcast(b, BF16), jnp.int32)


def _fold_sublanes(vs, sub, add):
    m4 = sub < 4
    cur = []
    for a, b in zip(vs[0::2], vs[1::2]):
        cur.append(add(jnp.where(m4, a, b), pltpu.roll(jnp.where(m4, b, a), 4, 0)))
    for half in (2, 1):
        m = (sub % (2 * half)) < half
        nxt = []
        for a, b in zip(cur[0::2], cur[1::2]):
            ta = add(a, pltpu.roll(a, SUBLANES - half, 0))
            tb = add(b, pltpu.roll(b, half, 0))
            nxt.append(jnp.where(m, ta, tb))
        cur = nxt
    return cur[0]


def _eye():
    return (_iota2(N_SEL, N_SEL, 0) == _iota2(N_SEL, N_SEL, 1)).astype(F32)


def _peer_u_kernel(j8_ref, sh_ref, hn_ref, gate_ref, tab_ref, w_ref, *scratch, tb):
    q_refs = scratch[:_IDX_GROUP]
    idx_ref, sem = scratch[_IDX_GROUP:]
    sub = lax.broadcasted_iota(jnp.int32, (SUBLANES, LANES), 0)
    ones = jnp.ones((LANES, LANES), BF16)
    perm = (0, 4, 2, 6, 1, 5, 3, 7)
    half_of_row = _iota2(2 * N_SEL, N_SEL, 0) - 2 * _iota2(2 * N_SEL, N_SEL, 1)

    def finish(src_ref, out_tok):
        rep = _dot(pltpu.bitcast(src_ref[...], BF16), ones)
        high = jnp.where(sh_ref[pl.ds(out_tok, 1), :] == 0.0, 1, 0)
        act = jnp.sum(jnp.where(half_of_row == high, rep, 0.0), axis=0, keepdims=True)
        gelu = 0.5 * act * (1.0 + jnp.tanh(0.7978845608028654 * (act + 0.044715 * act * act * act)))
        w_ref[pl.ds(out_tok, 1), :] = gate_ref[pl.ds(out_tok, 1), :] * gelu

    n_groups = tb // _IDX_GROUP

    def idx_copy(g, buf):
        src = j8_ref.at[pl.ds(g * _IDX_GROUP, _IDX_GROUP)]
        return pltpu.make_async_copy(src, idx_ref.at[buf], sem.at[buf])

    lag = 2
    for q_ref in q_refs[len(q_refs) - lag:]:
        q_ref[...] = jnp.zeros((N_SEL, LANES), jnp.int32)
    idx_copy(0, 0).start()

    def body(i, carry):
        for buf in range(2):
            g = 2 * i + buf
            idx_copy(g, buf).wait()

            @pl.when(g + 1 < n_groups)
            def _():
                idx_copy(g + 1, 1 - buf).start()

            for s in range(_IDX_GROUP):
                t = g * _IDX_GROUP + s
                finish(q_refs[(s - lag) % _IDX_GROUP], jnp.maximum(t - lag, 0))
                xi = lax.bitcast_convert_type(hn_ref[t].astype(BF16).astype(F32), jnp.int32)
                xx = pltpu.bitcast(xi | lax.shift_right_logical(xi, 16), BF16)
                folded = []
                for grp in range(N_SEL // SUBLANES):
                    prods = []
                    for r in range(SUBLANES):
                        k = grp * SUBLANES + perm[r]
                        row0 = pl.multiple_of(idx_ref[buf, s, 0, k], SUBLANES)
                        words = tab_ref[pl.ds(row0, SUBLANES), :]
                        prods.append(pltpu.bitcast(pltpu.bitcast(words, BF16) * xx, jnp.int32))
                    folded.append(_fold_sublanes(prods, sub, _add_bf16_pairs))
                q_refs[s][...] = jnp.concatenate(folded, axis=0)
        return carry

    lax.fori_loop(0, n_groups // 2, body, 0)
    for t in range(tb - lag, tb):
        finish(q_refs[t % _IDX_GROUP], t)


def _peer_u(j8, sh, hn3, gate, tab, tb):
    t = j8.shape[0]
    vec = lambda: pl.BlockSpec((tb, N_SEL), lambda i: (i, 0))
    return pl.pallas_call(
        functools.partial(_peer_u_kernel, tb=tb),
        out_shape=jax.ShapeDtypeStruct((t, N_SEL), F32),
        grid=(t // tb,),
        in_specs=[pl.BlockSpec((tb, 1, N_SEL), lambda i: (i, 0, 0)), vec(),
                  pl.BlockSpec((tb, SUBLANES, LANES), lambda i: (i, 0, 0)), vec(),
                  pl.BlockSpec(memory_space=pltpu.VMEM)],
        out_specs=vec(),
        scratch_shapes=[pltpu.VMEM((N_SEL, LANES), jnp.int32) for _ in range(_IDX_GROUP)]
        + [pltpu.SMEM((2, _IDX_GROUP, 1, N_SEL), jnp.int32), pltpu.SemaphoreType.DMA((2,))],
        compiler_params=_cparams(("arbitrary",), 56),
        name="peer_u",
    )(j8.reshape(t, 1, N_SEL), sh, hn3, gate, tab)


def _peer_v_kernel(j8_ref, sh_ref, wgt_ref, x2_ref, nw_ref, tab_ref, y_ref, wp_ref, acc_ref, idx_ref, sem,
                   *, tb, n_acc):
    eye = _eye()
    blk = (_iota2(2 * N_SEL, 2 * LANES, 0) // N_SEL == _iota2(2 * N_SEL, 2 * LANES, 1) // LANES).astype(BF16)

    def pre(c, carry):
        tok0 = pl.multiple_of(c * SUBLANES, SUBLANES)
        toks = pl.ds(tok0, SUBLANES)
        w = wgt_ref[toks, :]
        low = sh_ref[toks, :] > 0.0
        diag = jnp.concatenate([_diag_rows(jnp.where(low, w, 0.0), eye).astype(BF16),
                                _diag_rows(jnp.where(low, 0.0, w), eye).astype(BF16)], axis=1)
        rep = lax.bitcast_convert_type(_dot(diag, blk), jnp.int32)
        dst = pl.ds(pl.multiple_of(tok0 * N_SEL, SUBLANES * N_SEL), SUBLANES * N_SEL)
        wp_ref[dst, :] = rep[:, LANES:] | lax.shift_right_logical(rep[:, :LANES], 16)
        return carry

    n_groups = tb // _IDX_GROUP

    def idx_copy(g, buf):
        src = j8_ref.at[pl.ds(g * _IDX_GROUP, _IDX_GROUP)]
        return pltpu.make_async_copy(src, idx_ref.at[buf], sem.at[buf])

    idx_copy(0, 0).start()
    lax.fori_loop(0, tb // SUBLANES, pre, 0)
    group4 = 4

    def group_pair(i, carry):
        for buf in range(2):
            g = 2 * i + buf
            idx_copy(g, buf).wait()

            @pl.when(g + 1 < n_groups)
            def _():
                idx_copy(g + 1, 1 - buf).start()

            for s in range(_IDX_GROUP):
                t = g * _IDX_GROUP + s
                base = t * N_SEL
                accs = [jnp.zeros((SUBLANES, LANES), F32) for _ in range(n_acc)]
                for k0 in range(0, N_SEL, group4):
                    part = None
                    for k in range(k0, k0 + group4):
                        row0 = pl.multiple_of(idx_ref[buf, s, 0, k], SUBLANES)
                        words = pltpu.bitcast(tab_ref[pl.ds(row0, SUBLANES), :], BF16)
                        wpair = jnp.broadcast_to(wp_ref[pl.ds(base + k, 1), :], (SUBLANES, LANES))
                        prod = words * pltpu.bitcast(wpair, BF16)
                        part = prod if part is None else part + prod
                    bits = pltpu.bitcast(part, jnp.int32)
                    a = (k0 // group4) % (n_acc // 2)
                    accs[2 * a] = accs[2 * a] + lax.bitcast_convert_type(lax.shift_left(bits, 16), F32)
                    accs[2 * a + 1] = accs[2 * a + 1] + lax.bitcast_convert_type(bits & _HI_MASK, F32)
                tot = accs[0]
                for a in accs[1:]:
                    tot = tot + a
                acc_ref[pl.ds(pl.multiple_of(t * SUBLANES, SUBLANES), SUBLANES), :] = tot
        return carry

    lax.fori_loop(0, n_groups // 2, group_pair, 0)
    d = x2_ref.shape[1]
    x3 = [x2_ref[:, c * LANES:(c + 1) * LANES] + acc_ref[pl.ds(c, tb, stride=SUBLANES), :]
          for c in range(SUBLANES)]
    ss = x3[0] * x3[0]
    for xc in x3[1:]:
        ss = ss + xc * xc
    scale = lax.rsqrt(jnp.sum(ss, axis=-1, keepdims=True) * (1.0 / d) + NORM_EPS)
    for c, xc in enumerate(x3):
        y_ref[:, c * LANES:(c + 1) * LANES] = xc * scale * nw_ref[:, c * LANES:(c + 1) * LANES]


def _peer_v(j8, sh, wgt, x2, nw, tab, tb):
    t, d = x2.shape
    vec = lambda: pl.BlockSpec((tb, N_SEL), lambda i: (i, 0))
    return pl.pallas_call(
        functools.partial(_peer_v_kernel, tb=tb, n_acc=4),
        out_shape=jax.ShapeDtypeStruct((t, d), F32),
        grid=(t // tb,),
        in_specs=[pl.BlockSpec((tb, 1, N_SEL), lambda i: (i, 0, 0)), vec(), vec(),
                  pl.BlockSpec((tb, d), lambda i: (i, 0)),
                  pl.BlockSpec((1, d), lambda i: (0, 0)),
                  pl.BlockSpec(memory_space=pltpu.VMEM)],
        out_specs=pl.BlockSpec((tb, d), lambda i: (i, 0)),
        scratch_shapes=[pltpu.VMEM((tb * N_SEL, LANES), jnp.int32), pltpu.VMEM((tb * SUBLANES, LANES), F32),
                        pltpu.SMEM((2, _IDX_GROUP, 1, N_SEL), jnp.int32),
                        pltpu.SemaphoreType.DMA((2,))],
        compiler_params=_cparams(("arbitrary",), 56),
        name="peer_v",
    )(j8.reshape(t, 1, N_SEL), sh, wgt, x2, nw, tab)


_PACK_ROWS = 256


def _pack_kernel(lo_ref, hi_ref, o_ref):
    lo = lax.bitcast_convert_type(lo_ref[...].astype(BF16).astype(F32), jnp.int32)
    hi = lax.bitcast_convert_type(hi_ref[...].astype(BF16).astype(F32), jnp.int32)
    packed = hi | lax.shift_right_logical(lo, 16)
    rows, d = packed.shape
    for s in range(d // LANES):
        o_ref[pl.ds(s, rows, stride=d // LANES), :] = packed[:, s * LANES:(s + 1) * LANES]


def _pack_table(tab):
    n, d = tab.shape
    sub = d // LANES
    steps = n // 2 // _PACK_ROWS
    return pl.pallas_call(
        _pack_kernel,
        out_shape=jax.ShapeDtypeStruct((n // 2 * sub, LANES), jnp.int32),
        grid=(steps,),
        in_specs=[pl.BlockSpec((_PACK_ROWS, d), lambda i: (i, 0)),
                  pl.BlockSpec((_PACK_ROWS, d), lambda i: (i + steps, 0))],
        out_specs=pl.BlockSpec((_PACK_ROWS * sub, LANES), lambda i: (i, 0)),
        compiler_params=_cparams(("arbitrary",), 32),
        name="pack_table",
    )(tab, tab)


def _prep_weights(norm_mix_w, w_in, gla_w_gk2, gla_b_gk, gla_norm_w, gdn_conv_w, gdn_a_log,
                  gdn_dt_bias, gdn_norm_w, w_out, norm_ffn_w, peer_wq, peer_k1, peer_k2, peer_u, peer_v):
    d = w_in.shape[0]
    o_glr = 2 * GLA_QK + GLA_V
    o_gg = o_glr + GLA_LR
    o_qkv = o_gg + GLA_V
    o_da = o_qkv + GDN_QKV
    o_dz = o_da + 2 * GDN_HEADS
    small = jnp.concatenate([w_in[:, o_glr:o_gg], w_in[:, o_da:o_dz],
                             jnp.zeros((d, LANES - GLA_LR - 2 * GDN_HEADS), w_in.dtype)], axis=1)
    wcat = jnp.concatenate([w_in[:, :o_glr], w_in[:, o_gg:o_qkv], w_in[:, o_qkv:o_da],
                            w_in[:, o_dz:], small], axis=1).astype(BF16)
    w2p = jnp.zeros((LANES, GLA_QK), F32).at[:GLA_LR].set(gla_w_gk2).astype(BF16)
    avec = jnp.zeros((1, LANES), F32).at[0, SM_A0:SM_B0].set(-jnp.exp(gdn_a_log))
    dtb = jnp.zeros((1, LANES), F32).at[0, SM_A0:SM_B0].set(gdn_dt_bias)
    cw = jnp.zeros((SUBLANES, GDN_QKV), F32).at[:CONV_W].set(gdn_conv_w)
    return dict(
        nmix=norm_mix_w.reshape(1, d), wcat=wcat, w2p=w2p, b2=gla_b_gk.reshape(1, GLA_QK), avec=avec, dtb=dtb,
        gla_nw=gla_norm_w.reshape(1, GLA_DV), cw=cw, gdn_nw=gdn_norm_w.reshape(1, GDN_DV),
        wo=w_out.astype(BF16), nffn=norm_ffn_w.reshape(1, d), wq=peer_wq.astype(BF16),
        k1=peer_k1.astype(BF16), k2=peer_k2.astype(BF16), tab_u=_pack_table(peer_u), tab_v=_pack_table(peer_v))


def _layer(x, s_gla, s_gdn, conv_buf, w, nfw, chunk, tm, tb):
    n_seq, seq_len, d = x.shape
    t = n_seq * seq_len
    xf = x.reshape(t, d)
    gq, gk, gv, gg, dqkv, dz, la, sm = _inproj(xf, w["nmix"], w["wcat"], w["w2p"], w["b2"], w["avec"],
                                               w["dtb"], tm)
    o1, gla_t = _gla(gq, gk, gv, la, gg, jnp.swapaxes(s_gla, -1, -2), w["gla_nw"], n_seq, seq_len, chunk)
    tail0 = jnp.pad(conv_buf, ((0, 0), (SUBLANES - (CONV_W - 1), 0), (0, 0)))
    o2, gdn_new, tail = _gdn(dqkv, sm, dz, tail0, w["cw"], s_gdn, w["gdn_nw"], n_seq, seq_len, chunk)
    x2, hn, j8, sh, gate = _post(o1, o2, xf, w["wo"], w["nffn"], w["wq"], w["k1"], w["k2"], tm)
    wgt = _peer_u(j8, sh, hn.reshape(t, SUBLANES, LANES), gate, w["tab_u"], tb)
    y = _peer_v(j8, sh, wgt, x2, nfw, w["tab_v"], tb)
    return (y.reshape(n_seq, seq_len, d), jnp.swapaxes(gla_t, -1, -2), gdn_new,
            tail[:, SUBLANES - (CONV_W - 1):, :])


def kernel(x_prompt, x_sample, state_gla, state_gdn, state_gdn_conv, norm_mix_w, w_in, gla_w_gk2, gla_b_gk,
           gla_norm_w, gdn_conv_w, gdn_a_log, gdn_dt_bias, gdn_norm_w, w_out, norm_ffn_w, peer_wq, peer_k1,
           peer_k2, peer_u, peer_v, norm_final_w):
    depth = w_in.shape[0]
    assert depth == 1, "the final norm is fused into the last layer's PEER pass"
    n_p, l_p, d = x_prompt.shape
    n_s, l_s, _ = x_sample.shape
    nfw = norm_final_w.reshape(1, d)
    w = _prep_weights(norm_mix_w[0], w_in[0], gla_w_gk2[0], gla_b_gk[0], gla_norm_w[0], gdn_conv_w[0],
                      gdn_a_log[0], gdn_dt_bias[0], gdn_norm_w[0], w_out[0], norm_ffn_w[0], peer_wq[0],
                      peer_k1[0], peer_k2[0], peer_u[0], peer_v[0])
    z_gla = jnp.zeros((n_p, GLA_HEADS, GLA_DK, GLA_DV), F32)
    z_gdn = jnp.zeros((n_p, GDN_HEADS, GDN_DK, GDN_DV), F32)
    z_conv = jnp.zeros((n_p, CONV_W - 1, GDN_QKV), F32)
    chunk_p = min(CHUNK, l_p)
    chunk_s = min(CHUNK, l_s)
    tm_p = min(256, n_p * l_p)
    tm_s = min(256, n_s * l_s)
    yp, gla_p, gdn_p, conv_p = _layer(x_prompt, z_gla, z_gdn, z_conv, w, nfw, chunk_p, tm_p, min(128, tm_p))
    ys, gla_s, gdn_s, conv_s = _layer(x_sample, state_gla[0], state_gdn[0], state_gdn_conv[0], w, nfw,
                                      chunk_s, tm_s, min(128, tm_s))
    return (yp, ys, gla_p[None], gdn_p[None], conv_p[None], gla_s[None], gdn_s[None], conv_s[None])
```
